```python
import math
import jax, jax.numpy as jnp
from jax import lax
import numpy as np

D_MODEL = 1024
BATCH = 2
SEQ = 8192
DEPTH = 4
DEC_BATCH = 32
DEC_SEQ = 8
PAST_LEN = 8192
PAGE_SIZE = 128

N_A_LAYERS = DEPTH // 2
N_B_LAYERS = DEPTH - N_A_LAYERS
EXPAND = 2
D_INNER = EXPAND * D_MODEL
SSM_HEADDIM = 64
SSM_HEADS = D_INNER // SSM_HEADDIM
SSM_GROUPS = 4
D_STATE = 128
CONV_K = 4
CONV_DIM = D_INNER + 2 * SSM_GROUPS * D_STATE
SSD_CHUNK = 128
M_IN_DIM = D_INNER + CONV_DIM + SSM_HEADS
ATT_HEAD_DIM = 64
ATT_HEADS = D_MODEL // ATT_HEAD_DIM
ATT_KV_HEADS = 4
ATT_REP = ATT_HEADS // ATT_KV_HEADS
DILATION_GROUPS = ((128, 1), (512, 4), (2048, 16))
N_DGROUPS = len(DILATION_GROUPS)
ATT_WIDTH = ATT_HEADS * ATT_HEAD_DIM
KV_DIM = 2 * N_DGROUPS * ATT_KV_HEADS * ATT_HEAD_DIM
Q_BLOCK = 128
ROPE_THETA = 10000.0
PLE_DIM = 256
EPS = 1e-6

kernel_name = 'yoco_ssd_dilated_swa_decoder_step'


def rms_norm(x, w):
    xf = x.astype(jnp.float32)
    y = xf * lax.rsqrt(jnp.mean(xf * xf, axis=-1, keepdims=True) + EPS)
    return (y * w.astype(jnp.float32)).astype(x.dtype)


def rope(x, pos):
    half = x.shape[-1] // 2
    inv = 1.0 / (ROPE_THETA ** (jnp.arange(half, dtype=jnp.float32) / half))
    ang = pos.astype(jnp.float32)[:, None] * inv[None, :]
    ang = ang.reshape((ang.shape[0],) + (1,) * (x.ndim - 3) + (half,))
    cos, sin = jnp.cos(ang), jnp.sin(ang)
    xf = x.astype(jnp.float32)
    x1, x2 = xf[..., :half], xf[..., half:]
    return jnp.concatenate([x1 * cos - x2 * sin, x2 * cos + x1 * sin], axis=-1).astype(x.dtype)


def ssd(x, dt, A, B, C, h0, chunk):
    b, L, nh, p = x.shape
    g, n = B.shape[2], B.shape[3]
    r = nh // g
    c = L // chunk
    f32 = jnp.float32
    x = x.astype(f32).reshape(b, c, chunk, g, r, p)
    dt = dt.astype(f32).reshape(b, c, chunk, g, r)
    B = B.astype(f32).reshape(b, c, chunk, g, n)
    C = C.astype(f32).reshape(b, c, chunk, g, n)
    acum = jnp.cumsum(dt * A.reshape(g, r), axis=2)
    causal = jnp.tril(jnp.ones((chunk, chunk), dtype=bool))
    seg = acum[:, :, :, None] - acum[:, :, None, :]
    decay_ls = jnp.exp(jnp.where(causal[None, None, :, :, None, None], seg, -jnp.inf))
    xdt = x * dt[..., None]
    cb = jnp.einsum('bclgn,bcsgn->bclsg', C, B)
    y_diag = jnp.einsum('bclsgr,bcsgrp->bclgrp', cb[..., None] * decay_ls, xdt)
    decay_end = jnp.exp(acum[:, :, -1:] - acum)
    states = jnp.einsum('bcsgn,bcsgrp->bcgrpn', B, xdt * decay_end[..., None])
    chunk_decay = jnp.exp(acum[:, :, -1])

    def step(h, inp):
        dec, st = inp
        return h * dec[..., None, None] + st, h

    h_final, h_in = lax.scan(step, h0.astype(f32).reshape(b, g, r, p, n),
                             (jnp.moveaxis(chunk_decay, 1, 0), jnp.moveaxis(states, 1, 0)))
    y_off = jnp.einsum('bclgn,cbgrpn->bclgrp', C, h_in) * jnp.exp(acum)[..., None]
    y = (y_diag + y_off).reshape(b, L, nh, p)
    return y, h_final.reshape(b, nh, p, n)


def mamba_mixer(u, conv_state, ssm_state, in_w, conv_w, conv_b, dt_bias, A_log, d_skip, norm_w, out_w):
    b, L, _ = u.shape
    proj = u @ in_w
    z = proj[..., :D_INNER]
    xbc = proj[..., D_INNER:D_INNER + CONV_DIM]
    dt_raw = proj[..., D_INNER + CONV_DIM:]
    xpad = jnp.concatenate([conv_state.astype(xbc.dtype), xbc], axis=1)
    new_conv = xpad[:, xpad.shape[1] - (CONV_K - 1):]
    acc = conv_b + xpad[:, 0:L] * conv_w[0]
    for k in range(1, CONV_K):
        acc = acc + xpad[:, k:k + L] * conv_w[k]
    xbc = jax.nn.silu(acc)
    xs = xbc[..., :D_INNER]
    Bm = xbc[..., D_INNER:D_INNER + SSM_GROUPS * D_STATE].reshape(b, L, SSM_GROUPS, D_STATE)
    Cm = xbc[..., D_INNER + SSM_GROUPS * D_STATE:].reshape(b, L, SSM_GROUPS, D_STATE)
    dt = jax.nn.softplus((dt_raw + dt_bias).astype(jnp.float32))
    A = -jnp.exp(A_log.astype(jnp.float32))
    xh = xs.reshape(b, L, SSM_HEADS, SSM_HEADDIM)
    y, h = ssd(xh, dt, A, Bm, Cm, ssm_state, math.gcd(L, SSD_CHUNK))
    y = y + xh.astype(jnp.float32) * d_skip.astype(jnp.float32)[:, None]
    y = y.reshape(b, L, D_INNER) * jax.nn.silu(z.astype(jnp.float32))
    yg = y.reshape(b, L, SSM_GROUPS, D_INNER // SSM_GROUPS)
    yg = yg * lax.rsqrt(jnp.mean(yg * yg, axis=-1, keepdims=True) + EPS)
    y = yg.reshape(b, L, D_INNER) * norm_w.astype(jnp.float32)
    return y.astype(u.dtype) @ out_w, new_conv, h.astype(ssm_state.dtype)


def shared_kv(h, pos, kv_norm_w, kv_w, k_norm_w):
    b, L, _ = h.shape
    kv = (rms_norm(h, kv_norm_w) @ kv_w).reshape(b, L, 2, N_DGROUPS, ATT_KV_HEADS, ATT_HEAD_DIM)
    k = rope(rms_norm(kv[:, :, 0], k_norm_w), pos)
    return k, kv[:, :, 1]


def dilated_attend(q, kv_groups, rows_groups):
    b, Lq = q.shape[:2]
    outs, lses = [], []
    for gi in range(N_DGROUPS):
        window, dil = DILATION_GROUPS[gi]
        k, v = kv_groups[gi]
        n_keys = window // dil + 1
        idx = rows_groups[gi][:, None] - dil * jnp.arange(n_keys, dtype=jnp.int32)[None, :]
        valid = idx >= 0
        idx = jnp.maximum(idx, 0)
        kg = jnp.take(k, idx, axis=1)
        vg = jnp.take(v, idx, axis=1)
        qg = q[:, :, gi].reshape(b, Lq, ATT_KV_HEADS, ATT_REP, ATT_HEAD_DIM)
        s = jnp.einsum('bqkrd,bqjkd->bqkrj', qg, kg).astype(jnp.float32)
        s = jnp.where(valid[None, :, None, None, :], s, -jnp.inf)
        lse = jax.nn.logsumexp(s, axis=-1)
        pr = jnp.exp(s - lse[..., None])
        o = jnp.einsum('bqkrj,bqjkd->bqkrd', pr.astype(vg.dtype), vg)
        outs.append(o.astype(jnp.float32))
        lses.append(lse)
    wts = jax.nn.softmax(jnp.stack(lses, axis=0), axis=0)
    o = jnp.sum(jnp.stack(outs, axis=0) * wts[..., None], axis=0)
    return o.reshape(b, Lq, ATT_HEADS, ATT_HEAD_DIM)


def prompt_attend(q, k, v):
    b, L = q.shape[:2]
    nb = L // Q_BLOCK
    kvs = [(k[:, :, g], v[:, :, g]) for g in range(N_DGROUPS)]
    qb = jnp.moveaxis(q.reshape((b, nb, Q_BLOCK) + q.shape[2:]), 1, 0)
    starts = jnp.arange(nb, dtype=jnp.int32) * Q_BLOCK

    def blk(args):
        qi, st = args
        rows = st + jnp.arange(Q_BLOCK, dtype=jnp.int32)
        return dilated_attend(qi, kvs, [rows] * N_DGROUPS)

    o = lax.map(blk, (qb, starts))
    return jnp.moveaxis(o, 0, 1).reshape(b, L, ATT_HEADS, ATT_HEAD_DIM)


def attn_mixer(u, pos, attend, in_w, q_norm_w, out_w):
    b, L, _ = u.shape
    proj = u @ in_w
    q = proj[..., :N_DGROUPS * ATT_WIDTH].reshape(b, L, N_DGROUPS, ATT_HEADS, ATT_HEAD_DIM)
    gate = proj[..., N_DGROUPS * ATT_WIDTH:]
    q = rope(rms_norm(q, q_norm_w), pos) * (ATT_HEAD_DIM ** -0.5)
    o = attend(q)
    y = o.reshape(b, L, ATT_WIDTH) * jax.nn.silu(gate.astype(jnp.float32))
    return y.astype(u.dtype) @ out_w


def ple_add(h, p_i, ple_w_i, gate_w_i, ple_norm_i):
    gate = jax.nn.sigmoid((rms_norm(h, ple_norm_i) @ gate_w_i).astype(jnp.float32))
    return h + (gate * (p_i @ ple_w_i).astype(jnp.float32)).astype(h.dtype)


def trunk(x, p, pos, conv_states, ssm_states, make_attend, W):
    h = x
    new_conv, new_ssm = [], []
    k_sh, v_sh, attend = None, None, None
    for i in range(DEPTH):
        if i < N_A_LAYERS:
            u = rms_norm(h, W['norm_w'][i])
            out, c_new, s_new = mamba_mixer(u, conv_states[i], ssm_states[i], W['m_in_w'][i], W['m_conv_w'][i],
                                            W['m_conv_b'][i], W['m_dt_bias'][i], W['m_A_log'][i], W['m_D'][i],
                                            W['m_norm_w'][i], W['m_out_w'][i])
            new_conv.append(c_new)
            new_ssm.append(s_new)
        else:
            if i == N_A_LAYERS:
                k_sh, v_sh = shared_kv(h, pos, W['kv_norm_w'], W['kv_w'], W['k_norm_w'])
                attend = make_attend(k_sh, v_sh)
            j = i - N_A_LAYERS
            u = rms_norm(h, W['norm_w'][i])
            out = attn_mixer(u, pos, attend, W['a_in_w'][j], W['a_q_norm_w'][j], W['a_out_w'][j])
        h = h + out
        h = ple_add(h, p[i], W['ple_w'][i], W['ple_gate_w'][i], W['ple_norm_w'][i])
    return h, jnp.stack(new_conv, axis=0), jnp.stack(new_ssm, axis=0), k_sh, v_sh


def setup_inputs(seed: int = 0) -> dict:
    key = jax.random.key(seed)
    ks = jax.random.split(key, 32)
    f32 = jnp.float32

    def nrm(k, shape):
        return jax.random.normal(k, shape, f32)

    dt0 = jnp.exp(jax.random.uniform(ks[0], (N_A_LAYERS, SSM_HEADS), f32) * (math.log(0.1) - math.log(0.001)) + math.log(0.001))
    dt_bias = dt0 + jnp.log(-jnp.expm1(-dt0))
    A_log = jnp.log(jax.random.uniform(ks[1], (N_A_LAYERS, SSM_HEADS), f32, 1.0, 16.0))
    win_len = [min(w, PAST_LEN) for (w, _) in DILATION_GROUPS]
    return {
        'x_prompt': nrm(ks[2], (BATCH, SEQ, D_MODEL)),
        'x_sample': nrm(ks[3], (DEC_BATCH, DEC_SEQ, D_MODEL)),
        'state_ssm': 0.1 * nrm(ks[4], (N_A_LAYERS, DEC_BATCH, SSM_HEADS, SSM_HEADDIM, D_STATE)),
        'state_conv': nrm(ks[5], (N_A_LAYERS, DEC_BATCH, CONV_K - 1, CONV_DIM)),
        'cache_kv_g1': nrm(ks[6], (DEC_BATCH, win_len[0], 2, ATT_KV_HEADS, ATT_HEAD_DIM)),
        'cache_kv_g2': nrm(ks[7], (DEC_BATCH, win_len[1], 2, ATT_KV_HEADS, ATT_HEAD_DIM)),
        'cache_kv_g3': nrm(ks[8], (DEC_BATCH, win_len[2], 2, ATT_KV_HEADS, ATT_HEAD_DIM)),
        'p_prompt': nrm(ks[9], (DEPTH, BATCH, SEQ, PLE_DIM)),
        'p_sample': nrm(ks[10], (DEPTH, DEC_BATCH, DEC_SEQ, PLE_DIM)),
        'norm_w': 1.0 + 0.02 * nrm(ks[11], (DEPTH, D_MODEL)),
        'm_in_w': nrm(ks[12], (N_A_LAYERS, D_MODEL, M_IN_DIM)) * D_MODEL ** -0.5,
        'm_conv_w': nrm(ks[13], (N_A_LAYERS, CONV_K, CONV_DIM)) * CONV_K ** -0.5,
        'm_conv_b': 0.02 * nrm(ks[14], (N_A_LAYERS, CONV_DIM)),
        'm_dt_bias': dt_bias,
        'm_A_log': A_log,
        'm_D': 1.0 + 0.02 * nrm(ks[15], (N_A_LAYERS, SSM_HEADS)),
        'm_norm_w': 1.0 + 0.02 * nrm(ks[16], (N_A_LAYERS, D_INNER)),
        'm_out_w': nrm(ks[17], (N_A_LAYERS, D_INNER, D_MODEL)) * D_INNER ** -0.5,
        'kv_norm_w': 1.0 + 0.02 * nrm(ks[18], (D_MODEL,)),
        'kv_w': nrm(ks[19], (D_MODEL, KV_DIM)) * D_MODEL ** -0.5,
        'k_norm_w': 1.0 + 0.02 * nrm(ks[20], (ATT_HEAD_DIM,)),
        'a_in_w': nrm(ks[21], (N_B_LAYERS, D_MODEL, (N_DGROUPS + 1) * ATT_WIDTH)) * D_MODEL ** -0.5,
        'a_q_norm_w': 1.0 + 0.02 * nrm(ks[22], (N_B_LAYERS, ATT_HEAD_DIM)),
        'a_out_w': nrm(ks[23], (N_B_LAYERS, ATT_WIDTH, D_MODEL)) * ATT_WIDTH ** -0.5,
        'ple_w': nrm(ks[24], (DEPTH, PLE_DIM, D_MODEL)) * PLE_DIM ** -0.5,
        'ple_gate_w': nrm(ks[25], (DEPTH, D_MODEL, D_MODEL)) * D_MODEL ** -0.5,
        'ple_norm_w': 1.0 + 0.02 * nrm(ks[26], (DEPTH, D_MODEL)),
    }


def reference(x_prompt, x_sample, state_ssm, state_conv, cache_kv_g1, cache_kv_g2, cache_kv_g3,
              p_prompt, p_sample, norm_w, m_in_w, m_conv_w, m_conv_b, m_dt_bias, m_A_log, m_D,
              m_norm_w, m_out_w, kv_norm_w, kv_w, k_norm_w, a_in_w, a_q_norm_w, a_out_w,
              ple_w, ple_gate_w, ple_norm_w):
    W = {'norm_w': norm_w, 'm_in_w': m_in_w, 'm_conv_w': m_conv_w, 'm_conv_b': m_conv_b,
         'm_dt_bias': m_dt_bias, 'm_A_log': m_A_log, 'm_D': m_D, 'm_norm_w': m_norm_w,
         'm_out_w': m_out_w, 'kv_norm_w': kv_norm_w, 'kv_w': kv_w, 'k_norm_w': k_norm_w,
         'a_in_w': a_in_w, 'a_q_norm_w': a_q_norm_w, 'a_out_w': a_out_w, 'ple_w': ple_w,
         'ple_gate_w': ple_gate_w, 'ple_norm_w': ple_norm_w}

    b_p, seq = x_prompt.shape[0], x_prompt.shape[1]
    pos_p = jnp.arange(seq, dtype=jnp.int32)
    conv0 = jnp.zeros((N_A_LAYERS, b_p, CONV_K - 1, CONV_DIM), x_prompt.dtype)
    ssm0 = jnp.zeros((N_A_LAYERS, b_p, SSM_HEADS, SSM_HEADDIM, D_STATE), x_prompt.dtype)
    y_prompt, conv_p, ssm_p, k_p, v_p = trunk(x_prompt, p_prompt, pos_p, conv0, ssm0,
                                              lambda k, v: (lambda q: prompt_attend(q, k, v)), W)
    lw = [min(w, seq) for (w, _) in DILATION_GROUPS]
    kv_p1 = jnp.stack([k_p[:, seq - lw[0]:, 0], v_p[:, seq - lw[0]:, 0]], axis=2)
    kv_p2 = jnp.stack([k_p[:, seq - lw[1]:, 1], v_p[:, seq - lw[1]:, 1]], axis=2)
    kv_p3 = jnp.stack([k_p[:, seq - lw[2]:, 2], v_p[:, seq - lw[2]:, 2]], axis=2)

    dec_seq = x_sample.shape[1]
    pos_s = PAST_LEN + jnp.arange(dec_seq, dtype=jnp.int32)
    caches = (cache_kv_g1, cache_kv_g2, cache_kv_g3)

    def sample_make_attend(k, v):
        kvs = [(jnp.concatenate([caches[g][:, :, 0].astype(k.dtype), k[:, :, g]], axis=1),
                jnp.concatenate([caches[g][:, :, 1].astype(v.dtype), v[:, :, g]], axis=1))
               for g in range(N_DGROUPS)]
        rows = [caches[g].shape[1] + jnp.arange(dec_seq, dtype=jnp.int32) for g in range(N_DGROUPS)]
        return lambda q: dilated_attend(q, kvs, rows)

    y_sample, conv_s, ssm_s, k_s, v_s = trunk(x_sample, p_sample, pos_s, state_conv, state_ssm,
                                              sample_make_attend, W)
    kv_s1 = jnp.stack([k_s[:, :, 0], v_s[:, :, 0]], axis=2)
    kv_s2 = jnp.stack([k_s[:, :, 1], v_s[:, :, 1]], axis=2)
    kv_s3 = jnp.stack([k_s[:, :, 2], v_s[:, :, 2]], axis=2)
    return (y_prompt, y_sample, ssm_p, conv_p, ssm_s, conv_s, kv_p1, kv_p2, kv_p3, kv_s1, kv_s2, kv_s3)
```

```python
import functools

import jax
import jax.numpy as jnp
from jax import lax
from jax.experimental import pallas as pl
from jax.experimental.pallas import tpu as pltpu

F32 = jnp.float32
BF16 = jnp.bfloat16

D_MODEL = 1024
SEQ = 8192
DEPTH = 4
PAST_LEN = 8192
N_A_LAYERS = DEPTH // 2
D_INNER = 2048
SSM_HEADDIM = 64
SSM_HEADS = 32
SSM_GROUPS = 4
D_STATE = 128
CONV_K = 4
BC_DIM = 2 * SSM_GROUPS * D_STATE
CONV_DIM = D_INNER + BC_DIM
SSD_CHUNK = 128
ATT_HEAD_DIM = 64
ATT_HEADS = 16
ATT_KV_HEADS = 4
DILATION_GROUPS = ((128, 1), (512, 4), (2048, 16))
N_DGROUPS = 3
ATT_WIDTH = 1024
KV_HALF = N_DGROUPS * ATT_KV_HEADS * ATT_HEAD_DIM
GROUP_KV = ATT_KV_HEADS * ATT_HEAD_DIM
N_KEYS = 129
ROPE_THETA = 10000.0
PLE_DIM = 256
EPS = 1e-6

LANES = 128
SUBLANES = 8
VMEM_LIMIT_BYTES = 56 * 1024 * 1024
ROW_TILE = 256
NEG_BIG = -1e30

HEAD_ORDER = (0, 4, 1, 5, 2, 6, 3, 7, 8, 12, 9, 13, 10, 14, 11, 15)


def _cparams(semantics):
    return pltpu.CompilerParams(dimension_semantics=semantics,
                                vmem_limit_bytes=VMEM_LIMIT_BYTES)


def _const_spec(shape):
    nd = len(shape)
    return pl.BlockSpec(shape, lambda *_: (0,) * nd, pipeline_mode=pl.Buffered(1))


def _split_bf16(x, parts):
    out = []
    rem = x
    for _ in range(parts):
        hi = rem.astype(BF16)
        out.append(hi)
        rem = rem - hi.astype(F32)
    return out


def _dot(a, b):
    return jnp.dot(a, b, preferred_element_type=F32)


def _dot_nt(a, b):
    return lax.dot_general(a, b, (((1,), (1,)), ((), ())), preferred_element_type=F32)


def _sigmoid(x):
    return 1.0 / (1.0 + jnp.exp(-x))


def _lo_mask(rows):
    return lax.broadcasted_iota(jnp.int32, (rows, LANES), 1) < ATT_HEAD_DIM


def _head_norm_rope(y, hw, cos, sin, bd, scale):
    ss_hi, ss_lo = _split_bf16(y * y, 2)
    ms = _dot(ss_hi, bd) + _dot(ss_lo, bd)
    yn = y * lax.rsqrt(ms + EPS) * hw
    lane = lax.broadcasted_iota(jnp.int32, yn.shape, 1)
    partner = jnp.where((lane & 32) == 0, pltpu.roll(yn, 96, 1), pltpu.roll(yn, 32, 1))
    out = yn * cos + partner * sin
    if scale != 1.0:
        out = out * scale
    return out


def _norm_proj_kernel(*refs, segs, use_rope):
    h_ref, nw_ref, w_ref = refs[:3]
    pos = 3
    if use_rope:
        hw_ref, cos_ref, sin_ref, bd_ref = refs[3:7]
        pos = 7
    out_refs = refs[pos:]
    x = h_ref[...]
    ms = jnp.mean(x * x, axis=-1, keepdims=True)
    u = (x * lax.rsqrt(ms + EPS) * nw_ref[...]).astype(BF16)
    for (start, width, rope, scale), o_ref in zip(segs, out_refs):
        acc = _dot(u, w_ref[:, start:start + width])
        if rope:
            hw = hw_ref[...]
            cos = cos_ref[...]
            sin = sin_ref[...]
            bd = bd_ref[...]
            for cb in range(width // LANES):
                sl = slice(cb * LANES, (cb + 1) * LANES)
                o_ref[:, sl] = _head_norm_rope(acc[:, sl], hw, cos, sin, bd, scale).astype(o_ref.dtype)
        else:
            o_ref[...] = acc.astype(o_ref.dtype)


def _norm_proj(h, nw, w, segs, out_dtypes, rope_inputs=None, rope_rows=None):
    m = h.shape[0]
    tm = min(ROW_TILE, m)
    n = w.shape[1]
    use_rope = rope_inputs is not None
    in_specs = [pl.BlockSpec((tm, D_MODEL), lambda i: (i, 0)),
                _const_spec((1, D_MODEL)),
                _const_spec((D_MODEL, n))]
    args = [h, nw, w]
    if use_rope:
        hw, cos, sin, bd = rope_inputs
        nblk = rope_rows // tm
        in_specs += [_const_spec((1, LANES)),
                     pl.BlockSpec((tm, LANES), lambda i: (i % nblk, 0)),
                     pl.BlockSpec((tm, LANES), lambda i: (i % nblk, 0)),
                     _const_spec((LANES, LANES))]
        args += [hw, cos, sin, bd]
    out_shape = tuple(jax.ShapeDtypeStruct((m, s[1]), dt) for s, dt in zip(segs, out_dtypes))
    out_specs = tuple(pl.BlockSpec((tm, s[1]), lambda i: (i, 0)) for s in segs)
    return pl.pallas_call(
        functools.partial(_norm_proj_kernel, segs=segs, use_rope=use_rope),
        grid=(m // tm,),
        in_specs=in_specs,
        out_specs=out_specs,
        out_shape=out_shape,
        compiler_params=_cparams(("parallel",)),
        name="norm_proj",
    )(*args)


def _ssd_kernel(z_ref, xbc_ref, dt_ref, cw_ref, cb_ref, dtb_ref, alog_ref, dexp_ref, nw_ref,
                tril_ref, e_ref, h0_ref, c0_ref, y_ref, hout_ref, ht_scr, xpad_scr,
                *, rows_in, nchunks):
    q = SSD_CHUNK
    c = pl.program_id(1)
    gw = D_INNER // SSM_GROUPS

    @pl.when(c == 0)
    def _():
        ht_scr[...] = jnp.transpose(h0_ref[...])
        xpad_scr[0:SUBLANES, :] = c0_ref[...]

    def pad_rows(v):
        if rows_in == q:
            return v
        return jnp.concatenate([v, jnp.zeros((q - rows_in, v.shape[1]), v.dtype)], axis=0)

    xbc = pad_rows(xbc_ref[...])
    xpad_scr[SUBLANES:SUBLANES + q, :] = xbc
    conv = cb_ref[...] + cw_ref[CONV_K - 1:CONV_K, :] * xbc
    for k in range(CONV_K - 1):
        off = SUBLANES - (CONV_K - 1) + k
        conv = conv + cw_ref[k:k + 1, :] * xpad_scr[off:off + q, :]
    xpad_scr[0:SUBLANES, :] = xpad_scr[q:q + SUBLANES, :]
    act = conv * _sigmoid(conv)
    xs = act[:, :D_INNER]
    bm = act[:, D_INNER:D_INNER + SSM_GROUPS * D_STATE]
    cm = act[:, D_INNER + SSM_GROUPS * D_STATE:]

    dtr = pad_rows(dt_ref[...]) + dtb_ref[...]
    dt = jnp.maximum(dtr, 0.0) + jnp.log1p(jnp.exp(-jnp.abs(dtr)))
    if rows_in < q:
        row = lax.broadcasted_iota(jnp.int32, dt.shape, 0)
        dt = jnp.where(row < rows_in, dt, 0.0)
    a = dt * (-jnp.exp(alog_ref[...]))
    tril = tril_ref[...]
    acum = sum(_dot(tril, part) for part in _split_bf16(a, 3))
    acum_t = jnp.transpose(acum)

    e = e_ref[...]
    dt_e = sum(_dot(part, e) for part in _split_bf16(dt, 2))
    acum_e = sum(_dot(part, e) for part in _split_bf16(acum, 3))
    alast_e = acum_e[q - 1:q, :]
    exp_acum_e = jnp.exp(acum_e)
    decay_end_e = jnp.exp(alast_e - acum_e)
    chunk_decay_e = jnp.exp(alast_e)

    xdt = xs * dt_e
    xdt_bf = xdt.astype(BF16)
    xdtw_bf = (xdt * decay_end_e).astype(BF16)

    li = lax.broadcasted_iota(jnp.int32, (q, q), 0)
    si = lax.broadcasted_iota(jnp.int32, (q, q), 1)
    causal = li >= si
    lo = _lo_mask(q)
    zero_bf = jnp.zeros((q, LANES), BF16)

    y_parts = []
    for g in range(SSM_GROUPS):
        gs = slice(g * gw, (g + 1) * gw)
        bg = bm[:, g * D_STATE:(g + 1) * D_STATE]
        cg_bf = cm[:, g * D_STATE:(g + 1) * D_STATE].astype(BF16)
        cb = _dot_nt(cg_bf, bg.astype(BF16))
        htg = ht_scr[:, gs]
        y_off = _dot(cg_bf, htg.astype(BF16)) * exp_acum_e[:, gs]
        blocks = []
        for j in range(gw // LANES):
            hd = g * (SSM_HEADS // SSM_GROUPS) + 2 * j
            xpair = xdt_bf[:, hd * SSM_HEADDIM:hd * SSM_HEADDIM + LANES]
            acc = None
            for half in range(2):
                hh = hd + half
                seg = acum[:, hh:hh + 1] - acum_t[hh:hh + 1, :]
                mat = jnp.where(causal, cb * jnp.exp(jnp.minimum(seg, 0.0)), 0.0).astype(BF16)
                xh = jnp.where(lo, xpair, zero_bf) if half == 0 else jnp.where(lo, zero_bf, xpair)
                part = _dot(mat, xh)
                acc = part if acc is None else acc + part
            blocks.append(acc)
        y_diag = jnp.concatenate(blocks, axis=1)
        bgt_bf = jnp.transpose(bg).astype(BF16)
        st = _dot(bgt_bf, xdtw_bf[:, gs])
        ht_scr[:, gs] = htg * chunk_decay_e[:, gs] + st
        y_parts.append(y_diag + y_off)

    y = jnp.concatenate(y_parts, axis=1) + xs * dexp_ref[...]
    zz = pad_rows(z_ref[...])
    y = y * (zz * _sigmoid(zz))
    normed = []
    for g in range(SSM_GROUPS):
        yg = y[:, g * gw:(g + 1) * gw]
        ms = jnp.mean(yg * yg, axis=-1, keepdims=True)
        normed.append(yg * lax.rsqrt(ms + EPS))
    y = jnp.concatenate(normed, axis=1) * nw_ref[...]
    y_ref[...] = y[:rows_in].astype(y_ref.dtype)

    @pl.when(c == nchunks - 1)
    def _():
        hout_ref[...] = jnp.transpose(ht_scr[...])


def _ssd(z, xbc, dtp, lw, consts, h0, c0, *, nbatch, rows_in, nchunks, y_dtype):
    m = z.shape[0]
    row_map = lambda b, c: (b * nchunks + c, 0)
    batch_map = lambda b, c: (b, 0, 0)
    in_specs = [
        pl.BlockSpec((rows_in, D_INNER), row_map),
        pl.BlockSpec((rows_in, CONV_DIM), row_map),
        pl.BlockSpec((rows_in, LANES), row_map),
        _const_spec((CONV_K, CONV_DIM)),
        _const_spec((1, CONV_DIM)),
        _const_spec((1, LANES)),
        _const_spec((1, LANES)),
        _const_spec((1, D_INNER)),
        _const_spec((1, D_INNER)),
        _const_spec((SSD_CHUNK, SSD_CHUNK)),
        _const_spec((LANES, D_INNER)),
        pl.BlockSpec((None, D_INNER, D_STATE), batch_map),
        pl.BlockSpec((None, SUBLANES, CONV_DIM), batch_map),
    ]
    out_specs = (pl.BlockSpec((rows_in, D_INNER), row_map),
                 pl.BlockSpec((None, D_INNER, D_STATE), batch_map))
    out_shape = (jax.ShapeDtypeStruct((m, D_INNER), y_dtype),
                 jax.ShapeDtypeStruct((nbatch, D_INNER, D_STATE), F32))
    return pl.pallas_call(
        functools.partial(_ssd_kernel, rows_in=rows_in, nchunks=nchunks),
        grid=(nbatch, nchunks),
        in_specs=in_specs,
        out_specs=out_specs,
        out_shape=out_shape,
        scratch_shapes=[pltpu.VMEM((D_STATE, D_INNER), F32),
                        pltpu.VMEM((SSD_CHUNK + SUBLANES, CONV_DIM), F32)],
        compiler_params=_cparams(("parallel", "arbitrary")),
        name="ssd",
    )(z, xbc, dtp, lw["conv_w"], lw["conv_b"], lw["dt_bias"], lw["a_log"], lw["d_exp"],
      lw["norm_w"], consts["tril"], consts["expand"], h0, c0)


def _attn_prompt_kernel(q_ref, kcur_ref, kprev_ref, vcur_ref, vprev_ref, o_ref, l_ref):
    i = pl.program_id(2)
    t = SSD_CHUNK
    qi = lax.broadcasted_iota(jnp.int32, (t, 2 * t), 0)
    kk = lax.broadcasted_iota(jnp.int32, (t, 2 * t), 1)
    valid = (kk >= qi) & (kk <= qi + (N_KEYS - 1)) & ((kk >= t) | (i > 0))
    bias = jnp.where(valid, 0.0, NEG_BIG)
    lo = _lo_mask(t)
    lo2 = _lo_mask(2 * t)
    zero_bf = jnp.zeros((t, LANES), BF16)
    ones_stack = jnp.concatenate([jnp.where(lo2, 1.0, 0.0), jnp.where(lo2, 0.0, 1.0)],
                                 axis=0).astype(BF16)
    for kc in range(GROUP_KV // LANES):
        ks = slice(kc * LANES, (kc + 1) * LANES)
        k2 = jnp.concatenate([kprev_ref[:, ks], kcur_ref[:, ks]], axis=0).astype(BF16)
        v2 = jnp.concatenate([vprev_ref[:, ks], vcur_ref[:, ks]], axis=0)
        v_stack = jnp.concatenate([jnp.where(lo2, v2, 0.0), jnp.where(lo2, 0.0, v2)],
                                  axis=0).astype(BF16)
        for pb in range(4):
            blk = kc * 4 + pb
            bs = slice(blk * LANES, (blk + 1) * LANES)
            q2 = q_ref[:, bs]
            sa = _dot_nt(jnp.where(lo, q2, zero_bf), k2) + bias
            sb = _dot_nt(jnp.where(lo, zero_bf, q2), k2) + bias
            ma = jnp.max(sa, axis=-1, keepdims=True)
            mb = jnp.max(sb, axis=-1, keepdims=True)
            p_cat = jnp.concatenate([jnp.exp(sa - ma), jnp.exp(sb - mb)], axis=1).astype(BF16)
            num = _dot(p_cat, v_stack)
            den = _dot(p_cat, ones_stack)
            o_ref[:, bs] = num / den
            l_ref[:, bs] = jnp.where(lo, ma, mb) + jnp.log(den)


def _attn_prompt_group(q, k, v, gi, nbatch, seq):
    _, dil = DILATION_GROUPS[gi]
    m = q.shape[0]
    t = SSD_CHUNK
    nb = seq // dil // t
    qv = q.reshape(m // dil, dil * N_DGROUPS * ATT_WIDTH)
    kv_ = k.reshape(m // dil, dil * KV_HALF)
    vv = v.reshape(m // dil, dil * KV_HALF)
    cur = lambda b, r, i: (b * nb + i, r * N_DGROUPS + gi)
    prev = lambda b, r, i: (b * nb + jnp.maximum(i - 1, 0), r * N_DGROUPS + gi)
    out_map = lambda b, r, i: (b * nb + i, r)
    o, l = pl.pallas_call(
        _attn_prompt_kernel,
        grid=(nbatch, dil, nb),
        in_specs=[pl.BlockSpec((t, ATT_WIDTH), cur),
                  pl.BlockSpec((t, GROUP_KV), cur),
                  pl.BlockSpec((t, GROUP_KV), prev),
                  pl.BlockSpec((t, GROUP_KV), cur),
                  pl.BlockSpec((t, GROUP_KV), prev)],
        out_specs=(pl.BlockSpec((t, ATT_WIDTH), out_map),
                   pl.BlockSpec((t, ATT_WIDTH), out_map)),
        out_shape=(jax.ShapeDtypeStruct((m // dil, dil * ATT_WIDTH), F32),
                   jax.ShapeDtypeStruct((m // dil, dil * ATT_WIDTH), F32)),
        compiler_params=_cparams(("parallel", "parallel", "arbitrary")),
        name=f"attn_prompt_g{gi}",
    )(qv, kv_, kv_, vv, vv)
    return o.reshape(m, ATT_WIDTH), l.reshape(m, ATT_WIDTH)


def _attn_sample_kernel(q_ref, kn_ref, vn_ref, c1_ref, c2_ref, c3_ref, o_ref, *, dec_seq):
    caches = (c1_ref, c2_ref, c3_ref)
    nrow = 8 * dec_seq
    lo_row = _lo_mask(dec_seq)
    pad = LANES - dec_seq
    outs = [[None] * N_DGROUPS for _ in range(ATT_WIDTH // LANES)]
    lses = [[None] * N_DGROUPS for _ in range(ATT_WIDTH // LANES)]
    for gi, (_, dil) in enumerate(DILATION_GROUPS):
        cref = caches[gi]
        w = cref.shape[0]
        tq = lax.broadcasted_iota(jnp.int32, (nrow, w), 0) & (dec_seq - 1)
        rho = lax.broadcasted_iota(jnp.int32, (nrow, w), 1)
        delta = w + tq - rho
        valid_c = (rho >= tq) & ((delta & (dil - 1)) == 0) & (delta <= (N_KEYS - 1) * dil)
        bias_c = jnp.where(valid_c, 0.0, NEG_BIG)
        tq_n = lax.broadcasted_iota(jnp.int32, (nrow, LANES), 0) & (dec_seq - 1)
        tn = lax.broadcasted_iota(jnp.int32, (nrow, LANES), 1)
        valid_n = (tn <= tq_n) & (((tq_n - tn) & (dil - 1)) == 0)
        bias_n = jnp.where(valid_n, 0.0, NEG_BIG)
        gsl = slice(gi * GROUP_KV, (gi + 1) * GROUP_KV)
        k_new = jnp.concatenate([kn_ref[:, gsl], jnp.zeros((pad, GROUP_KV), F32)], axis=0)
        v_new = jnp.concatenate([vn_ref[:, gsl], jnp.zeros((pad, GROUP_KV), F32)], axis=0)
        for kc in range(GROUP_KV // LANES):
            ks = slice(kc * LANES, (kc + 1) * LANES)
            rows = []
            for pb in range(4):
                blk = kc * 4 + pb
                q2 = q_ref[:, gi * ATT_WIDTH + blk * LANES:gi * ATT_WIDTH + (blk + 1) * LANES]
                rows.append(jnp.where(lo_row, q2, 0.0))
                rows.append(jnp.where(lo_row, 0.0, q2))
            lhs = jnp.concatenate(rows, axis=0).astype(BF16)
            kcache = cref[:, ks].astype(BF16)
            vcache = cref[:, GROUP_KV + kc * LANES:GROUP_KV + (kc + 1) * LANES].astype(BF16)
            s_c = _dot_nt(lhs, kcache) + bias_c
            s_n = _dot_nt(lhs, k_new[:, ks].astype(BF16)) + bias_n
            mx = jnp.maximum(jnp.max(s_c, axis=-1, keepdims=True),
                             jnp.max(s_n, axis=-1, keepdims=True))
            p_c = jnp.exp(s_c - mx)
            p_n = jnp.exp(s_n - mx)
            den = jnp.sum(p_c, axis=-1, keepdims=True) + jnp.sum(p_n, axis=-1, keepdims=True)
            num = _dot(p_c.astype(BF16), vcache) + _dot(p_n.astype(BF16), v_new[:, ks].astype(BF16))
            on = num / den
            lse = mx + jnp.log(den)
            for pb in range(4):
                blk = kc * 4 + pb
                r0 = pb * 2 * dec_seq
                r1 = r0 + dec_seq
                outs[blk][gi] = jnp.where(lo_row, on[r0:r1], on[r1:r1 + dec_seq])
                lses[blk][gi] = jnp.where(lo_row, lse[r0:r1], lse[r1:r1 + dec_seq])
    for blk in range(ATT_WIDTH // LANES):
        ls = lses[blk]
        mx = jnp.maximum(jnp.maximum(ls[0], ls[1]), ls[2])
        ws = [jnp.exp(l - mx) for l in ls]
        tot = ws[0] + ws[1] + ws[2]
        o = (outs[blk][0] * ws[0] + outs[blk][1] * ws[1] + outs[blk][2] * ws[2]) / tot
        o_ref[:, blk * LANES:(blk + 1) * LANES] = o


def _attn_sample(q, k, v, caches, nbatch, dec_seq):
    m = q.shape[0]
    row_map = lambda b: (b, 0)
    in_specs = [pl.BlockSpec((dec_seq, N_DGROUPS * ATT_WIDTH), row_map),
                pl.BlockSpec((dec_seq, KV_HALF), row_map),
                pl.BlockSpec((dec_seq, KV_HALF), row_map)]
    for cch in caches:
        in_specs.append(pl.BlockSpec((None, cch.shape[1], 2 * GROUP_KV), lambda b: (b, 0, 0)))
    return pl.pallas_call(
        functools.partial(_attn_sample_kernel, dec_seq=dec_seq),
        grid=(nbatch,),
        in_specs=in_specs,
        out_specs=pl.BlockSpec((dec_seq, ATT_WIDTH), row_map),
        out_shape=jax.ShapeDtypeStruct((m, ATT_WIDTH), F32),
        compiler_params=_cparams(("parallel",)),
        name="attn_sample",
    )(q, k, v, *caches)


def _out_ple_kernel(*refs, mode):
    if mode == "mamba":
        (y_ref,) = refs[:1]
        pos = 1
        mix = y_ref[...].astype(BF16)
    else:
        if mode == "attn_merge":
            o_refs = refs[0:3]
            l_refs = refs[3:6]
            gate_ref = refs[6]
            pos = 7
            ls = [r[...] for r in l_refs]
            mx = jnp.maximum(jnp.maximum(ls[0], ls[1]), ls[2])
            ws = [jnp.exp(l - mx) for l in ls]
            tot = ws[0] + ws[1] + ws[2]
            o = (o_refs[0][...] * ws[0] + o_refs[1][...] * ws[1] + o_refs[2][...] * ws[2]) / tot
        else:
            o_ref_in, gate_ref = refs[:2]
            pos = 2
            o = o_ref_in[...]
        gate = gate_ref[...]
        mix = (o * (gate * _sigmoid(gate))).astype(BF16)
    h_ref, p_ref, wout_ref, pnw_ref, gw_ref, pw_ref, out_ref = refs[pos:]
    h1 = h_ref[...] + _dot(mix, wout_ref[...])
    ms = jnp.mean(h1 * h1, axis=-1, keepdims=True)
    un = (h1 * lax.rsqrt(ms + EPS) * pnw_ref[...]).astype(BF16)
    gate_p = _sigmoid(_dot(un, gw_ref[...]))
    pe = _dot(p_ref[...].astype(BF16), pw_ref[...])
    out_ref[...] = h1 + gate_p * pe


def _out_ple(mix_inputs, mode, h, p_all, layer, wout, pnw, gw, pw):
    m = h.shape[0]
    tm = min(ROW_TILE, m)
    row_map = lambda i: (i, 0)
    in_specs = [pl.BlockSpec((tm, a.shape[1]), row_map) for a in mix_inputs]
    in_specs += [pl.BlockSpec((tm, D_MODEL), row_map),
                 pl.BlockSpec((None, tm, PLE_DIM), lambda i: (layer, i, 0)),
                 _const_spec(wout.shape),
                 _const_spec((1, D_MODEL)),
                 _const_spec((D_MODEL, D_MODEL)),
                 _const_spec((PLE_DIM, D_MODEL))]
    return pl.pallas_call(
        functools.partial(_out_ple_kernel, mode=mode),
        grid=(m // tm,),
        in_specs=in_specs,
        out_specs=pl.BlockSpec((tm, D_MODEL), row_map),
        out_shape=jax.ShapeDtypeStruct((m, D_MODEL), F32),
        compiler_params=_cparams(("parallel",)),
        name=f"out_ple_{mode}",
    )(*mix_inputs, h, p_all, wout, pnw, gw, pw)


def _rope_tables(pos):
    half = ATT_HEAD_DIM // 2
    inv = 1.0 / (ROPE_THETA ** (jnp.arange(half, dtype=F32) / half))
    ang = pos.astype(F32)[:, None] * inv[None, :]
    cos, sin = jnp.cos(ang), jnp.sin(ang)
    cos128 = jnp.concatenate([cos, cos, cos, cos], axis=1)
    sin128 = jnp.concatenate([-sin, sin, -sin, sin], axis=1)
    return cos128, sin128


def _permute_heads(w, axis):
    shape = w.shape
    w = w.reshape(shape[:axis] + (ATT_HEADS, ATT_HEAD_DIM) + shape[axis + 1:])
    w = jnp.take(w, jnp.array(HEAD_ORDER, dtype=jnp.int32), axis=axis)
    return w.reshape(shape)


def _prep_weights(norm_w, m_in_w, m_conv_w, m_conv_b, m_dt_bias, m_A_log, m_D, m_norm_w, m_out_w,
                  kv_norm_w, kv_w, k_norm_w, a_in_w, a_q_norm_w, a_out_w, ple_w, ple_gate_w,
                  ple_norm_w):
    row = lambda v: v.reshape(1, -1).astype(F32)
    lane_pad = lambda v: jnp.pad(v.astype(F32), (0, LANES - v.shape[0])).reshape(1, LANES)
    mamba = []
    for i in range(N_A_LAYERS):
        w_in = jnp.pad(m_in_w[i], ((0, 0), (0, LANES - SSM_HEADS))).astype(BF16)
        mamba.append(dict(
            in_w=w_in,
            conv_w=m_conv_w[i].astype(F32),
            conv_b=row(m_conv_b[i]),
            dt_bias=lane_pad(m_dt_bias[i]),
            a_log=lane_pad(m_A_log[i]),
            d_exp=row(jnp.repeat(m_D[i], SSM_HEADDIM)),
            norm_w=row(m_norm_w[i]),
            out_w=m_out_w[i].astype(BF16),
        ))
    attn = []
    for j in range(DEPTH - N_A_LAYERS):
        w = a_in_w[j].reshape(D_MODEL, N_DGROUPS + 1, ATT_WIDTH)
        w = _permute_heads(w, 2).reshape(D_MODEL, (N_DGROUPS + 1) * ATT_WIDTH)
        attn.append(dict(
            in_w=w.astype(BF16),
            q_norm_w=row(jnp.tile(a_q_norm_w[j], LANES // ATT_HEAD_DIM)),
            out_w=_permute_heads(a_out_w[j], 0).astype(BF16),
        ))
    return dict(
        norm_w=[row(norm_w[i]) for i in range(DEPTH)],
        mamba=mamba,
        attn=attn,
        kv_norm_w=row(kv_norm_w),
        kv_w=kv_w.astype(BF16),
        k_norm_w=row(jnp.tile(k_norm_w, LANES // ATT_HEAD_DIM)),
        ple_w=[ple_w[i].astype(BF16) for i in range(DEPTH)],
        ple_gate_w=[ple_gate_w[i].astype(BF16) for i in range(DEPTH)],
        ple_norm_w=[row(ple_norm_w[i]) for i in range(DEPTH)],
    )


def _constants():
    t = SSD_CHUNK
    tril = jnp.tril(jnp.ones((t, t), F32)).astype(BF16)
    head = jnp.arange(LANES, dtype=jnp.int32)[:, None]
    col = jnp.arange(D_INNER, dtype=jnp.int32)[None, :]
    expand = (head == col // SSM_HEADDIM).astype(BF16)
    li = jnp.arange(LANES, dtype=jnp.int32)
    bd = ((li[:, None] // ATT_HEAD_DIM) == (li[None, :] // ATT_HEAD_DIM)).astype(F32) / ATT_HEAD_DIM
    return dict(tril=tril, expand=expand, bd=bd.astype(BF16))


MAMBA_SEGS = ((0, D_INNER, False, 1.0),
              (D_INNER, CONV_DIM, False, 1.0),
              (D_INNER + CONV_DIM, LANES, False, 1.0))
KV_SEGS = ((0, KV_HALF, True, 1.0), (KV_HALF, KV_HALF, False, 1.0))
ATTN_SEGS = ((0, N_DGROUPS * ATT_WIDTH, True, ATT_HEAD_DIM ** -0.5),
             (N_DGROUPS * ATT_WIDTH, ATT_WIDTH, False, 1.0))


def _trunk(x, p_all, wts, consts, rope_tabs, rope_rows, *, nbatch, seq, ssm0, conv0, caches):
    prompt = caches is None
    m = x.shape[0]
    h = x
    new_ssm, new_conv = [], []
    rows_in = SSD_CHUNK if prompt else seq
    nchunks = seq // rows_in
    for i in range(N_A_LAYERS):
        lw = wts["mamba"][i]
        z, xbc, dtp = _norm_proj(h, wts["norm_w"][i], lw["in_w"], MAMBA_SEGS, (F32, F32, F32))
        y, h_fin = _ssd(z, xbc, dtp, lw, consts, ssm0[i], conv0[i], nbatch=nbatch,
                        rows_in=rows_in, nchunks=nchunks, y_dtype=BF16 if prompt else F32)
        new_ssm.append(h_fin.reshape(nbatch, SSM_HEADS, SSM_HEADDIM, D_STATE))
        new_conv.append(xbc.reshape(nbatch, seq, CONV_DIM)[:, seq - (CONV_K - 1):])
        h = _out_ple([y], "mamba", h, p_all, i, lw["out_w"], wts["ple_norm_w"][i],
                     wts["ple_gate_w"][i], wts["ple_w"][i])
    cos, sin = rope_tabs
    k, v = _norm_proj(h, wts["kv_norm_w"], wts["kv_w"], KV_SEGS, (F32, F32),
                      rope_inputs=(wts["k_norm_w"], cos, sin, consts["bd"]), rope_rows=rope_rows)
    for j in range(DEPTH - N_A_LAYERS):
        i = N_A_LAYERS + j
        aw = wts["attn"][j]
        q, gate = _norm_proj(h, wts["norm_w"][i], aw["in_w"], ATTN_SEGS,
                             (BF16 if prompt else F32, F32),
                             rope_inputs=(aw["q_norm_w"], cos, sin, consts["bd"]), rope_rows=rope_rows)
        if prompt:
            os_, ls_ = [], []
            for gi in range(N_DGROUPS):
                o_g, l_g = _attn_prompt_group(q, k, v, gi, nbatch, seq)
                os_.append(o_g)
                ls_.append(l_g)
            mix_inputs, mode = os_ + ls_ + [gate], "attn_merge"
        else:
            o = _attn_sample(q, k, v, caches, nbatch, seq)
            mix_inputs, mode = [o, gate], "attn"
        h = _out_ple(mix_inputs, mode, h, p_all, i, aw["out_w"], wts["ple_norm_w"][i],
                     wts["ple_gate_w"][i], wts["ple_w"][i])
    return h, jnp.stack(new_ssm, axis=0), jnp.stack(new_conv, axis=0), k, v


def kernel(x_prompt, x_sample, state_ssm, state_conv, cache_kv_g1, cache_kv_g2, cache_kv_g3,
           p_prompt, p_sample, norm_w, m_in_w, m_conv_w, m_conv_b, m_dt_bias, m_A_log, m_D,
           m_norm_w, m_out_w, kv_norm_w, kv_w, k_norm_w, a_in_w, a_q_norm_w, a_out_w,
           ple_w, ple_gate_w, ple_norm_w):
    wts = _prep_weights(norm_w, m_in_w, m_conv_w, m_conv_b, m_dt_bias, m_A_log, m_D, m_norm_w,
                        m_out_w, kv_norm_w, kv_w, k_norm_w, a_in_w, a_q_norm_w, a_out_w, ple_w,
                        ple_gate_w, ple_norm_w)
    consts = _constants()

    b_p, seq = x_prompt.shape[0], x_prompt.shape[1]
    m_p = b_p * seq
    ssm0 = jnp.zeros((N_A_LAYERS, b_p, D_INNER, D_STATE), F32)
    conv0 = jnp.zeros((N_A_LAYERS, b_p, SUBLANES, CONV_DIM), F32)
    tabs_p = _rope_tables(jnp.arange(seq, dtype=jnp.int32))
    y_p, ssm_p, conv_p, k_p, v_p = _trunk(
        x_prompt.reshape(m_p, D_MODEL), p_prompt.reshape(DEPTH, m_p, PLE_DIM), wts, consts,
        tabs_p, seq, nbatch=b_p, seq=seq, ssm0=ssm0, conv0=conv0, caches=None)

    b_s, dec = x_sample.shape[0], x_sample.shape[1]
    m_s = b_s * dec
    pos_s = PAST_LEN + jnp.arange(dec, dtype=jnp.int32)
    tabs_s = tuple(jnp.tile(t, (b_s, 1)) for t in _rope_tables(pos_s))
    ssm0_s = state_ssm.reshape(N_A_LAYERS, b_s, D_INNER, D_STATE)
    conv0_s = jnp.pad(state_conv, ((0, 0), (0, 0), (SUBLANES - (CONV_K - 1), 0), (0, 0)))
    caches = tuple(cch.reshape(b_s, cch.shape[1], 2 * GROUP_KV)
                   for cch in (cache_kv_g1, cache_kv_g2, cache_kv_g3))
    y_s, ssm_s, conv_s, k_s, v_s = _trunk(
        x_sample.reshape(m_s, D_MODEL), p_sample.reshape(DEPTH, m_s, PLE_DIM), wts, consts,
        tabs_s, m_s, nbatch=b_s, seq=dec, ssm0=ssm0_s, conv0=conv0_s, caches=caches)

    def kv_out(k, v, nbatch, length, gi, keep):
        k5 = k.reshape(nbatch, length, N_DGROUPS, ATT_KV_HEADS, ATT_HEAD_DIM)
        v5 = v.reshape(nbatch, length, N_DGROUPS, ATT_KV_HEADS, ATT_HEAD_DIM)
        return jnp.stack([k5[:, length - keep:, gi], v5[:, length - keep:, gi]], axis=2)

    kv_p = [kv_out(k_p, v_p, b_p, seq, gi, min(w, seq)) for gi, (w, _) in enumerate(DILATION_GROUPS)]
    kv_s = [kv_out(k_s, v_s, b_s, dec, gi, dec) for gi in range(N_DGROUPS)]
    return (y_p.reshape(b_p, seq, D_MODEL), y_s.reshape(b_s, dec, D_MODEL),
            ssm_p, conv_p, ssm_s, conv_s, kv_p[0], kv_p[1], kv_p[2], kv_s[0], kv_s[1], kv_s[2])
```

```python
import functools

import jax
import jax.numpy as jnp
from jax import lax
from jax.experimental import pallas as pl
from jax.experimental.pallas import tpu as pltpu

F32 = jnp.float32
BF16 = jnp.bfloat16

D_MODEL = 1024
SEQ = 8192
DEPTH = 4
PAST_LEN = 8192
N_A_LAYERS = DEPTH // 2
D_INNER = 2048
SSM_HEADDIM = 64
SSM_HEADS = 32
SSM_GROUPS = 4
D_STATE = 128
CONV_K = 4
BC_DIM = 2 * SSM_GROUPS * D_STATE
CONV_DIM = D_INNER + BC_DIM
SSD_CHUNK = 128
ATT_HEAD_DIM = 64
ATT_HEADS = 16
ATT_KV_HEADS = 4
DILATION_GROUPS = ((128, 1), (512, 4), (2048, 16))
N_DGROUPS = 3
ATT_WIDTH = 1024
KV_HALF = N_DGROUPS * ATT_KV_HEADS * ATT_HEAD_DIM
GROUP_KV = ATT_KV_HEADS * ATT_HEAD_DIM
N_KEYS = 129
ROPE_THETA = 10000.0
PLE_DIM = 256
EPS = 1e-6

LANES = 128
SUBLANES = 8
VMEM_LIMIT_BYTES = 56 * 1024 * 1024
ROW_TILE = 256
NEG_BIG = -1e30
RELAYOUT_SLOTS = 4
ATTN_Q_BLOCKS = 2

HEAD_ORDER = (0, 4, 1, 5, 2, 6, 3, 7, 8, 12, 9, 13, 10, 14, 11, 15)


def _cparams(semantics):
    return pltpu.CompilerParams(dimension_semantics=semantics,
                                vmem_limit_bytes=VMEM_LIMIT_BYTES)


def _const_spec(shape):
    nd = len(shape)
    return pl.BlockSpec(shape, lambda *_: (0,) * nd, pipeline_mode=pl.Buffered(1))


def _split_bf16(x, parts):
    out = []
    rem = x
    for _ in range(parts):
        hi = rem.astype(BF16)
        out.append(hi)
        rem = rem - hi.astype(F32)
    return out


def _dot(a, b):
    return jnp.dot(a, b, preferred_element_type=F32)


def _dot_nt(a, b):
    return lax.dot_general(a, b, (((1,), (1,)), ((), ())), preferred_element_type=F32)


def _sigmoid(x):
    return 1.0 / (1.0 + jnp.exp(-x))


def _lo_mask(rows):
    return lax.broadcasted_iota(jnp.int32, (rows, LANES), 1) < ATT_HEAD_DIM


def _head_norm_rope(y, hw, cos, sin, bd, scale):
    ss_hi, ss_lo = _split_bf16(y * y, 2)
    ms = _dot(ss_hi, bd) + _dot(ss_lo, bd)
    yn = y * lax.rsqrt(ms + EPS) * hw
    lane = lax.broadcasted_iota(jnp.int32, yn.shape, 1)
    partner = jnp.where((lane & 32) == 0, pltpu.roll(yn, 96, 1), pltpu.roll(yn, 32, 1))
    out = yn * cos + partner * sin
    if scale != 1.0:
        out = out * scale
    return out


def _norm_proj_kernel(*refs, segs, use_rope, n_out, tm):
    h_ref, nw_ref, w_ref = refs[:3]
    pos = 3
    if use_rope:
        hw_ref, cos_ref, sin_ref, bd_ref = refs[3:7]
        pos = 7
    out_refs = refs[pos:pos + n_out]
    scr = refs[pos + n_out] if len(refs) > pos + n_out else None
    x = h_ref[...]
    ms = jnp.mean(x * x, axis=-1, keepdims=True)
    u = (x * lax.rsqrt(ms + EPS) * nw_ref[...]).astype(BF16)
    slot = 0
    for start, width, rope, scale, sinks in segs:
        acc = _dot(u, w_ref[:, start:start + width])
        if not rope and all(dil == 0 for _, _, dil in sinks):
            for oi, col_off, _ in sinks:
                out_refs[oi][:, col_off:col_off + width] = acc.astype(out_refs[oi].dtype)
            continue
        for cb in range(width // LANES):
            val = acc[:, cb * LANES:(cb + 1) * LANES]
            if rope:
                val = _head_norm_rope(val, hw_ref[...], cos_ref[...], sin_ref[...], bd_ref[...], scale)
            for oi, col_off, dil in sinks:
                o_ref = out_refs[oi]
                cs = slice(col_off + cb * LANES, col_off + (cb + 1) * LANES)
                if dil == 0:
                    o_ref[:, cs] = val.astype(o_ref.dtype)
                elif dil == 1:
                    o_ref[0, :, cs] = val.astype(o_ref.dtype)
                else:
                    s = slot % RELAYOUT_SLOTS
                    slot += 1
                    scr[s] = val
                    for r in range(dil):
                        o_ref[r, :, cs] = scr[s, pl.ds(r, tm // dil, stride=dil), :].astype(o_ref.dtype)


def _norm_proj(h, nw, w, segs, out_defs, rope_inputs=None, rope_rows=None, nbatch=None):
    m = h.shape[0]
    tm = min(ROW_TILE, m)
    n = w.shape[1]
    use_rope = rope_inputs is not None
    in_specs = [pl.BlockSpec((tm, D_MODEL), lambda i: (i, 0)),
                _const_spec((1, D_MODEL)),
                _const_spec((D_MODEL, n))]
    args = [h, nw, w]
    if use_rope:
        hw, cos, sin, bd = rope_inputs
        nblk = rope_rows // tm
        in_specs += [_const_spec((1, LANES)),
                     pl.BlockSpec((tm, LANES), lambda i: (i % nblk, 0)),
                     pl.BlockSpec((tm, LANES), lambda i: (i % nblk, 0)),
                     _const_spec((LANES, LANES))]
        args += [hw, cos, sin, bd]
    out_shape, out_specs = [], []
    for width, dt, dil in out_defs:
        if dil == 0:
            out_shape.append(jax.ShapeDtypeStruct((m, width), dt))
            out_specs.append(pl.BlockSpec((tm, width), lambda i: (i, 0)))
        else:
            tiles = m // nbatch // tm
            out_shape.append(jax.ShapeDtypeStruct((nbatch, dil, m // nbatch // dil, width), dt))
            out_specs.append(pl.BlockSpec((None, dil, tm // dil, width),
                                          lambda i, tiles=tiles: (i // tiles, 0, i % tiles, 0)))
    scratch = []
    if any(dil > 1 for _, _, dil in out_defs):
        scratch.append(pltpu.VMEM((RELAYOUT_SLOTS, tm, LANES), F32))
    return pl.pallas_call(
        functools.partial(_norm_proj_kernel, segs=segs, use_rope=use_rope, n_out=len(out_defs), tm=tm),
        grid=(m // tm,),
        in_specs=in_specs,
        out_specs=tuple(out_specs),
        out_shape=tuple(out_shape),
        scratch_shapes=scratch,
        compiler_params=_cparams(("parallel",)),
        name="norm_proj",
    )(*args)


def _ssd_kernel(z_ref, xbc_ref, dt_ref, cw_ref, cb_ref, dtb_ref, alog_ref, dexp_ref, nw_ref,
                tril_ref, e_ref, h0_ref, c0_ref, y_ref, hout_ref, ht_scr, xpad_scr,
                *, rows_in, nchunks):
    q = SSD_CHUNK
    c = pl.program_id(1)
    gw = D_INNER // SSM_GROUPS

    @pl.when(c == 0)
    def _():
        ht_scr[...] = jnp.transpose(h0_ref[...])
        xpad_scr[0:SUBLANES, :] = c0_ref[...]

    def pad_rows(v):
        if rows_in == q:
            return v
        return jnp.concatenate([v, jnp.zeros((q - rows_in, v.shape[1]), v.dtype)], axis=0)

    xbc = pad_rows(xbc_ref[...])
    xpad_scr[SUBLANES:SUBLANES + q, :] = xbc
    conv = cb_ref[...] + cw_ref[CONV_K - 1:CONV_K, :] * xbc
    for k in range(CONV_K - 1):
        off = SUBLANES - (CONV_K - 1) + k
        conv = conv + cw_ref[k:k + 1, :] * xpad_scr[off:off + q, :]
    xpad_scr[0:SUBLANES, :] = xpad_scr[q:q + SUBLANES, :]
    act = conv * _sigmoid(conv)
    xs = act[:, :D_INNER]
    bm = act[:, D_INNER:D_INNER + SSM_GROUPS * D_STATE]
    cm = act[:, D_INNER + SSM_GROUPS * D_STATE:]

    dtr = pad_rows(dt_ref[...]) + dtb_ref[...]
    dt = jnp.maximum(dtr, 0.0) + jnp.log1p(jnp.exp(-jnp.abs(dtr)))
    if rows_in < q:
        row = lax.broadcasted_iota(jnp.int32, dt.shape, 0)
        dt = jnp.where(row < rows_in, dt, 0.0)
    a = dt * (-jnp.exp(alog_ref[...]))
    tril = tril_ref[...]
    acum = sum(_dot(tril, part) for part in _split_bf16(a, 3))
    acum_t = jnp.transpose(acum)

    e = e_ref[...]
    dt_e = sum(_dot(part, e) for part in _split_bf16(dt, 2))
    acum_e = sum(_dot(part, e) for part in _split_bf16(acum, 3))
    alast_e = acum_e[q - 1:q, :]
    exp_acum_e = jnp.exp(acum_e)
    decay_end_e = jnp.exp(alast_e - acum_e)
    chunk_decay_e = jnp.exp(alast_e)

    xdt = xs * dt_e
    xdt_bf = xdt.astype(BF16)
    xdtw_bf = (xdt * decay_end_e).astype(BF16)

    li = lax.broadcasted_iota(jnp.int32, (q, q), 0)
    si = lax.broadcasted_iota(jnp.int32, (q, q), 1)
    causal = li >= si
    lo = _lo_mask(q)
    zero_bf = jnp.zeros((q, LANES), BF16)

    y_parts = []
    for g in range(SSM_GROUPS):
        gs = slice(g * gw, (g + 1) * gw)
        bg = bm[:, g * D_STATE:(g + 1) * D_STATE]
        cg_bf = cm[:, g * D_STATE:(g + 1) * D_STATE].astype(BF16)
        cb = _dot_nt(cg_bf, bg.astype(BF16))
        htg = ht_scr[:, gs]
        y_off = _dot(cg_bf, htg.astype(BF16)) * exp_acum_e[:, gs]
        blocks = []
        for j in range(gw // LANES):
            hd = g * (SSM_HEADS // SSM_GROUPS) + 2 * j
            xpair = xdt_bf[:, hd * SSM_HEADDIM:hd * SSM_HEADDIM + LANES]
            acc = None
            for half in range(2):
                hh = hd + half
                seg = acum[:, hh:hh + 1] - acum_t[hh:hh + 1, :]
                mat = jnp.where(causal, cb * jnp.exp(jnp.minimum(seg, 0.0)), 0.0).astype(BF16)
                xh = jnp.where(lo, xpair, zero_bf) if half == 0 else jnp.where(lo, zero_bf, xpair)
                part = _dot(mat, xh)
                acc = part if acc is None else acc + part
            blocks.append(acc)
        y_diag = jnp.concatenate(blocks, axis=1)
        bgt_bf = jnp.transpose(bg).astype(BF16)
        st = _dot(bgt_bf, xdtw_bf[:, gs])
        ht_scr[:, gs] = htg * chunk_decay_e[:, gs] + st
        y_parts.append(y_diag + y_off)

    y = jnp.concatenate(y_parts, axis=1) + xs * dexp_ref[...]
    zz = pad_rows(z_ref[...])
    y = y * (zz * _sigmoid(zz))
    normed = []
    for g in range(SSM_GROUPS):
        yg = y[:, g * gw:(g + 1) * gw]
        ms = jnp.mean(yg * yg, axis=-1, keepdims=True)
        normed.append(yg * lax.rsqrt(ms + EPS))
    y = jnp.concatenate(normed, axis=1) * nw_ref[...]
    y_ref[...] = y[:rows_in].astype(y_ref.dtype)

    @pl.when(c == nchunks - 1)
    def _():
        hout_ref[...] = jnp.transpose(ht_scr[...])


def _ssd(z, xbc, dtp, lw, consts, h0, c0, *, nbatch, rows_in, nchunks, y_dtype):
    m = z.shape[0]
    row_map = lambda b, c: (b * nchunks + c, 0)
    batch_map = lambda b, c: (b, 0, 0)
    in_specs = [
        pl.BlockSpec((rows_in, D_INNER), row_map),
        pl.BlockSpec((rows_in, CONV_DIM), row_map),
        pl.BlockSpec((rows_in, LANES), row_map),
        _const_spec((CONV_K, CONV_DIM)),
        _const_spec((1, CONV_DIM)),
        _const_spec((1, LANES)),
        _const_spec((1, LANES)),
        _const_spec((1, D_INNER)),
        _const_spec((1, D_INNER)),
        _const_spec((SSD_CHUNK, SSD_CHUNK)),
        _const_spec((LANES, D_INNER)),
        pl.BlockSpec((None, D_INNER, D_STATE), batch_map),
        pl.BlockSpec((None, SUBLANES, CONV_DIM), batch_map),
    ]
    out_specs = (pl.BlockSpec((rows_in, D_INNER), row_map),
                 pl.BlockSpec((None, D_INNER, D_STATE), batch_map))
    out_shape = (jax.ShapeDtypeStruct((m, D_INNER), y_dtype),
                 jax.ShapeDtypeStruct((nbatch, D_INNER, D_STATE), F32))
    return pl.pallas_call(
        functools.partial(_ssd_kernel, rows_in=rows_in, nchunks=nchunks),
        grid=(nbatch, nchunks),
        in_specs=in_specs,
        out_specs=out_specs,
        out_shape=out_shape,
        scratch_shapes=[pltpu.VMEM((D_STATE, D_INNER), F32),
                        pltpu.VMEM((SSD_CHUNK + SUBLANES, CONV_DIM), F32)],
        compiler_params=_cparams(("parallel", "arbitrary")),
        name="ssd",
    )(z, xbc, dtp, lw["conv_w"], lw["conv_b"], lw["dt_bias"], lw["a_log"], lw["d_exp"],
      lw["norm_w"], consts["tril"], consts["expand"], h0, c0)


def _attn_prompt_kernel(q_ref, kcur_ref, kprev_ref, vcur_ref, vprev_ref, o_ref, l_ref, *, nq):
    i = pl.program_id(2)
    t = SSD_CHUNK
    npair = GROUP_KV // ATT_HEAD_DIM
    qi = lax.broadcasted_iota(jnp.int32, (t, 2 * t), 0)
    kk = lax.broadcasted_iota(jnp.int32, (t, 2 * t), 1)
    band = (kk >= qi) & (kk <= qi + (N_KEYS - 1))
    bias_inner = jnp.where(band, 0.0, NEG_BIG)
    bias_first = jnp.where(band & ((kk >= t) | (i > 0)), 0.0, NEG_BIG)
    lo = _lo_mask(t)
    lo2 = _lo_mask(2 * t)
    zero_q = jnp.zeros((t, LANES), BF16)
    zero_v = jnp.zeros((2 * t, LANES), BF16)
    ones_stack = jnp.concatenate([jnp.where(lo2, 1.0, 0.0), jnp.where(lo2, 0.0, 1.0)],
                                 axis=0).astype(BF16)
    lane = lax.broadcasted_iota(jnp.int32, (t, LANES), 1)
    lane_slot = (lane & (ATT_HEAD_DIM - 1)) >> 3
    for s in range(nq):
        rows = slice(s * t, (s + 1) * t)
        bias = bias_first if s == 0 else bias_inner
        l_c = jnp.zeros((t, LANES), F32)
        for kc in range(GROUP_KV // LANES):
            ks = slice(kc * LANES, (kc + 1) * LANES)
            if s == 0:
                k2 = jnp.concatenate([kprev_ref[:, ks], kcur_ref[0:t, ks]], axis=0)
                v2 = jnp.concatenate([vprev_ref[:, ks], vcur_ref[0:t, ks]], axis=0)
            else:
                k2 = kcur_ref[(s - 1) * t:(s + 1) * t, ks]
                v2 = vcur_ref[(s - 1) * t:(s + 1) * t, ks]
            v_stack = jnp.concatenate([jnp.where(lo2, v2, zero_v), jnp.where(lo2, zero_v, v2)], axis=0)
            rhs = jnp.concatenate([v_stack, ones_stack], axis=1)
            lo_rows, hi_rows = [], []
            for pb in range(npair):
                q2 = q_ref[rows, (kc * npair + pb) * LANES:(kc * npair + pb + 1) * LANES]
                lo_rows.append(jnp.where(lo, q2, zero_q))
                hi_rows.append(jnp.where(lo, zero_q, q2))
            lhs = jnp.concatenate(lo_rows + hi_rows, axis=0)
            sc = _dot_nt(lhs, k2).reshape(2 * npair, t, 2 * t) + bias[None]
            mx = jnp.max(sc, axis=-1, keepdims=True)
            p = jnp.exp(sc - mx).astype(BF16).reshape(2 * npair * t, 2 * t)
            p_cat = jnp.concatenate([p[:npair * t], p[npair * t:]], axis=1)
            res = _dot(p_cat, rhs)
            for pb in range(npair):
                blk = kc * npair + pb
                num = res[pb * t:(pb + 1) * t, :LANES]
                den = res[pb * t:(pb + 1) * t, LANES:]
                o_ref[rows, blk * LANES:(blk + 1) * LANES] = num / den
                lse = jnp.where(lo, mx[pb], mx[npair + pb]) + jnp.log(den)
                l_c = jnp.where(lane_slot == blk, lse, l_c)
        l_ref[rows, :] = l_c


def _attn_prompt_group(q, k, v, gi):
    nbatch, dil, rows, _ = q.shape
    t = SSD_CHUNK
    nq = ATTN_Q_BLOCKS
    cur = lambda b, r, i: (b, r, i, 0)
    prev = lambda b, r, i: (b, r, jnp.maximum(nq * i - 1, 0), 0)
    return pl.pallas_call(
        functools.partial(_attn_prompt_kernel, nq=nq),
        grid=(nbatch, dil, rows // (nq * t)),
        in_specs=[pl.BlockSpec((None, None, nq * t, ATT_WIDTH), cur),
                  pl.BlockSpec((None, None, nq * t, GROUP_KV), cur),
                  pl.BlockSpec((None, None, t, GROUP_KV), prev),
                  pl.BlockSpec((None, None, nq * t, GROUP_KV), cur),
                  pl.BlockSpec((None, None, t, GROUP_KV), prev)],
        out_specs=(pl.BlockSpec((None, None, nq * t, ATT_WIDTH), cur),
                   pl.BlockSpec((None, None, nq * t, LANES), cur)),
        out_shape=(jax.ShapeDtypeStruct((nbatch, dil, rows, ATT_WIDTH), F32),
                   jax.ShapeDtypeStruct((nbatch, dil, rows, LANES), F32)),
        compiler_params=_cparams(("parallel", "parallel", "arbitrary")),
        name=f"attn_prompt_g{gi}",
    )(q, k, k, v, v)


def _attn_sample_kernel(q_ref, kn_ref, vn_ref, c1_ref, c2_ref, c3_ref, o_ref, *, dec_seq):
    caches = (c1_ref, c2_ref, c3_ref)
    nrow = 8 * dec_seq
    lo_row = _lo_mask(dec_seq)
    pad = LANES - dec_seq
    outs = [[None] * N_DGROUPS for _ in range(ATT_WIDTH // LANES)]
    lses = [[None] * N_DGROUPS for _ in range(ATT_WIDTH // LANES)]
    for gi, (_, dil) in enumerate(DILATION_GROUPS):
        cref = caches[gi]
        w = cref.shape[0]
        tq = lax.broadcasted_iota(jnp.int32, (nrow, w), 0) & (dec_seq - 1)
        rho = lax.broadcasted_iota(jnp.int32, (nrow, w), 1)
        delta = w + tq - rho
        valid_c = (rho >= tq) & ((delta & (dil - 1)) == 0) & (delta <= (N_KEYS - 1) * dil)
        bias_c = jnp.where(valid_c, 0.0, NEG_BIG)
        tq_n = lax.broadcasted_iota(jnp.int32, (nrow, LANES), 0) & (dec_seq - 1)
        tn = lax.broadcasted_iota(jnp.int32, (nrow, LANES), 1)
        valid_n = (tn <= tq_n) & (((tq_n - tn) & (dil - 1)) == 0)
        bias_n = jnp.where(valid_n, 0.0, NEG_BIG)
        gsl = slice(gi * GROUP_KV, (gi + 1) * GROUP_KV)
        k_new = jnp.concatenate([kn_ref[:, gsl], jnp.zeros((pad, GROUP_KV), F32)], axis=0)
        v_new = jnp.concatenate([vn_ref[:, gsl], jnp.zeros((pad, GROUP_KV), F32)], axis=0)
        for kc in range(GROUP_KV // LANES):
            ks = slice(kc * LANES, (kc + 1) * LANES)
            rows = []
            for pb in range(4):
                blk = kc * 4 + pb
                q2 = q_ref[:, gi * ATT_WIDTH + blk * LANES:gi * ATT_WIDTH + (blk + 1) * LANES]
                rows.append(jnp.where(lo_row, q2, 0.0))
                rows.append(jnp.where(lo_row, 0.0, q2))
            lhs = jnp.concatenate(rows, axis=0).astype(BF16)
            kcache = cref[:, ks].astype(BF16)
            vcache = cref[:, GROUP_KV + kc * LANES:GROUP_KV + (kc + 1) * LANES].astype(BF16)
            s_c = _dot_nt(lhs, kcache) + bias_c
            s_n = _dot_nt(lhs, k_new[:, ks].astype(BF16)) + bias_n
            mx = jnp.maximum(jnp.max(s_c, axis=-1, keepdims=True),
                             jnp.max(s_n, axis=-1, keepdims=True))
            p_c = jnp.exp(s_c - mx)
            p_n = jnp.exp(s_n - mx)
            den = jnp.sum(p_c, axis=-1, keepdims=True) + jnp.sum(p_n, axis=-1, keepdims=True)
            num = _dot(p_c.astype(BF16), vcache) + _dot(p_n.astype(BF16), v_new[:, ks].astype(BF16))
            on = num / den
            lse = mx + jnp.log(den)
            for pb in range(4):
                blk = kc * 4 + pb
                r0 = pb * 2 * dec_seq
                r1 = r0 + dec_seq
                outs[blk][gi] = jnp.where(lo_row, on[r0:r1], on[r1:r1 + dec_seq])
                lses[blk][gi] = jnp.where(lo_row, lse[r0:r1], lse[r1:r1 + dec_seq])
    for blk in range(ATT_WIDTH // LANES):
        ls = lses[blk]
        mx = jnp.maximum(jnp.maximum(ls[0], ls[1]), ls[2])
        ws = [jnp.exp(l - mx) for l in ls]
        tot = ws[0] + ws[1] + ws[2]
        o = (outs[blk][0] * ws[0] + outs[blk][1] * ws[1] + outs[blk][2] * ws[2]) / tot
        o_ref[:, blk * LANES:(blk + 1) * LANES] = o


def _attn_sample(q, k, v, caches, nbatch, dec_seq):
    m = q.shape[0]
    row_map = lambda b: (b, 0)
    in_specs = [pl.BlockSpec((dec_seq, N_DGROUPS * ATT_WIDTH), row_map),
                pl.BlockSpec((dec_seq, KV_HALF), row_map),
                pl.BlockSpec((dec_seq, KV_HALF), row_map)]
    for cch in caches:
        in_specs.append(pl.BlockSpec((None, cch.shape[1], 2 * GROUP_KV), lambda b: (b, 0, 0)))
    return pl.pallas_call(
        functools.partial(_attn_sample_kernel, dec_seq=dec_seq),
        grid=(nbatch,),
        in_specs=in_specs,
        out_specs=pl.BlockSpec((dec_seq, ATT_WIDTH), row_map),
        out_shape=jax.ShapeDtypeStruct((m, ATT_WIDTH), F32),
        compiler_params=_cparams(("parallel",)),
        name="attn_sample",
    )(q, k, v, *caches)


def _token_order(ref, scr, slot, cb, dil, tm):
    cs = slice(cb * LANES, (cb + 1) * LANES)
    if dil == 1:
        return ref[0, :, cs]
    for r in range(dil):
        scr[slot, pl.ds(r, tm // dil, stride=dil), :] = ref[r, :, cs]
    return scr[slot]


def _out_ple_kernel(*refs, mode, tm):
    if mode == "mamba":
        (y_ref,) = refs[:1]
        pos = 1
        mix = y_ref[...].astype(BF16)
    elif mode == "attn_merge":
        o_refs = refs[0:3]
        l_refs = refs[3:6]
        gate_ref, x_ref = refs[6:8]
        pos = 8
        scr, mix_scr = refs[-2:]
        refs = refs[:-2]
        dils = [d for _, d in DILATION_GROUPS]
        ls = [_token_order(l_refs[g], scr, g, 0, dils[g], tm) for g in range(N_DGROUPS)]
        mx = jnp.maximum(jnp.maximum(ls[0], ls[1]), ls[2])
        es = [jnp.exp(l - mx) for l in ls]
        tot = es[0] + es[1] + es[2]
        spread = x_ref[...]
        ws = [sum(_dot(part, spread) for part in _split_bf16(e / tot, 2)) for e in es]
        for cb in range(ATT_WIDTH // LANES):
            cs = slice(cb * LANES, (cb + 1) * LANES)
            o = None
            for g in range(N_DGROUPS):
                og = _token_order(o_refs[g], scr, N_DGROUPS + (cb * N_DGROUPS + g) % RELAYOUT_SLOTS,
                                  cb, dils[g], tm)
                term = og * ws[g][:, cs]
                o = term if o is None else o + term
            gate = gate_ref[:, cs]
            mix_scr[:, cs] = (o * (gate * _sigmoid(gate))).astype(BF16)
        mix = mix_scr[...]
    else:
        o_ref_in, gate_ref = refs[:2]
        pos = 2
        gate = gate_ref[...]
        mix = (o_ref_in[...] * (gate * _sigmoid(gate))).astype(BF16)
    h_ref, p_ref, wout_ref, pnw_ref, gw_ref, pw_ref, out_ref = refs[pos:]
    h1 = h_ref[...] + _dot(mix, wout_ref[...])
    ms = jnp.mean(h1 * h1, axis=-1, keepdims=True)
    un = (h1 * lax.rsqrt(ms + EPS) * pnw_ref[...]).astype(BF16)
    gate_p = _sigmoid(_dot(un, gw_ref[...]))
    pe = _dot(p_ref[...].astype(BF16), pw_ref[...])
    out_ref[...] = h1 + gate_p * pe


def _out_ple(mix_inputs, mode, h, p_all, layer, wout, pnw, gw, pw):
    m = h.shape[0]
    tm = min(ROW_TILE, m)
    row_map = lambda i: (i, 0)
    in_specs = []
    for a in mix_inputs:
        if a.ndim == 4:
            _, dil, rows, width = a.shape
            tiles = rows * dil // tm
            in_specs.append(pl.BlockSpec((None, dil, tm // dil, width),
                                         lambda i, tiles=tiles: (i // tiles, 0, i % tiles, 0)))
        elif a.shape[0] == m:
            in_specs.append(pl.BlockSpec((tm, a.shape[1]), row_map))
        else:
            in_specs.append(_const_spec(a.shape))
    scratch = []
    if mode == "attn_merge":
        scratch = [pltpu.VMEM((N_DGROUPS + RELAYOUT_SLOTS, tm, LANES), F32),
                   pltpu.VMEM((tm, ATT_WIDTH), BF16)]
    in_specs += [pl.BlockSpec((tm, D_MODEL), row_map),
                 pl.BlockSpec((None, tm, PLE_DIM), lambda i: (layer, i, 0)),
                 _const_spec(wout.shape),
                 _const_spec((1, D_MODEL)),
                 _const_spec((D_MODEL, D_MODEL)),
                 _const_spec((PLE_DIM, D_MODEL))]
    return pl.pallas_call(
        functools.partial(_out_ple_kernel, mode=mode, tm=tm),
        grid=(m // tm,),
        in_specs=in_specs,
        out_specs=pl.BlockSpec((tm, D_MODEL), row_map),
        out_shape=jax.ShapeDtypeStruct((m, D_MODEL), F32),
        scratch_shapes=scratch,
        compiler_params=_cparams(("parallel",)),
        name=f"out_ple_{mode}",
    )(*mix_inputs, h, p_all, wout, pnw, gw, pw)


def _rope_tables(pos):
    half = ATT_HEAD_DIM // 2
    inv = 1.0 / (ROPE_THETA ** (jnp.arange(half, dtype=F32) / half))
    ang = pos.astype(F32)[:, None] * inv[None, :]
    cos, sin = jnp.cos(ang), jnp.sin(ang)
    cos128 = jnp.concatenate([cos, cos, cos, cos], axis=1)
    sin128 = jnp.concatenate([-sin, sin, -sin, sin], axis=1)
    return cos128, sin128


def _permute_heads(w, axis):
    shape = w.shape
    w = w.reshape(shape[:axis] + (ATT_HEADS, ATT_HEAD_DIM) + shape[axis + 1:])
    w = jnp.take(w, jnp.array(HEAD_ORDER, dtype=jnp.int32), axis=axis)
    return w.reshape(shape)


def _prep_weights(norm_w, m_in_w, m_conv_w, m_conv_b, m_dt_bias, m_A_log, m_D, m_norm_w, m_out_w,
                  kv_norm_w, kv_w, k_norm_w, a_in_w, a_q_norm_w, a_out_w, ple_w, ple_gate_w,
                  ple_norm_w):
    row = lambda v: v.reshape(1, -1).astype(F32)
    lane_pad = lambda v: jnp.pad(v.astype(F32), (0, LANES - v.shape[0])).reshape(1, LANES)
    mamba = []
    for i in range(N_A_LAYERS):
        w_in = jnp.pad(m_in_w[i], ((0, 0), (0, LANES - SSM_HEADS))).astype(BF16)
        mamba.append(dict(
            in_w=w_in,
            conv_w=m_conv_w[i].astype(F32),
            conv_b=row(m_conv_b[i]),
            dt_bias=lane_pad(m_dt_bias[i]),
            a_log=lane_pad(m_A_log[i]),
            d_exp=row(jnp.repeat(m_D[i], SSM_HEADDIM)),
            norm_w=row(m_norm_w[i]),
            out_w=m_out_w[i].astype(BF16),
        ))
    attn = []
    for j in range(DEPTH - N_A_LAYERS):
        w = a_in_w[j].reshape(D_MODEL, N_DGROUPS + 1, ATT_WIDTH)
        w = _permute_heads(w, 2).reshape(D_MODEL, (N_DGROUPS + 1) * ATT_WIDTH)
        attn.append(dict(
            in_w=w.astype(BF16),
            q_norm_w=row(jnp.tile(a_q_norm_w[j], LANES // ATT_HEAD_DIM)),
            out_w=_permute_heads(a_out_w[j], 0).astype(BF16),
        ))
    return dict(
        norm_w=[row(norm_w[i]) for i in range(DEPTH)],
        mamba=mamba,
        attn=attn,
        kv_norm_w=row(kv_norm_w),
        kv_w=kv_w.astype(BF16),
        k_norm_w=row(jnp.tile(k_norm_w, LANES // ATT_HEAD_DIM)),
        ple_w=[ple_w[i].astype(BF16) for i in range(DEPTH)],
        ple_gate_w=[ple_gate_w[i].astype(BF16) for i in range(DEPTH)],
        ple_norm_w=[row(ple_norm_w[i]) for i in range(DEPTH)],
    )


def _constants():
    t = SSD_CHUNK
    tril = jnp.tril(jnp.ones((t, t), F32)).astype(BF16)
    head = jnp.arange(LANES, dtype=jnp.int32)[:, None]
    col = jnp.arange(D_INNER, dtype=jnp.int32)[None, :]
    expand = (head == col // SSM_HEADDIM).astype(BF16)
    li = jnp.arange(LANES, dtype=jnp.int32)
    bd = ((li[:, None] // ATT_HEAD_DIM) == (li[None, :] // ATT_HEAD_DIM)).astype(F32) / ATT_HEAD_DIM
    ccol = jnp.arange(ATT_WIDTH, dtype=jnp.int32)[None, :]
    src = 8 * (ccol // LANES) + jnp.where(ccol % LANES < ATT_HEAD_DIM, 0, ATT_HEAD_DIM)
    lse_spread = (li[:, None] == src).astype(BF16)
    return dict(tril=tril, expand=expand, bd=bd.astype(BF16), lse_spread=lse_spread)


_Q_SCALE = ATT_HEAD_DIM ** -0.5
_DILS = tuple(d for _, d in DILATION_GROUPS)
MAMBA_SEGS = ((0, D_INNER, False, 1.0, ((0, 0, 0),)),
              (D_INNER, CONV_DIM, False, 1.0, ((1, 0, 0),)),
              (D_INNER + CONV_DIM, LANES, False, 1.0, ((2, 0, 0),)))
MAMBA_OUTS = ((D_INNER, F32, 0), (CONV_DIM, F32, 0), (LANES, F32, 0))
KV_SEGS_TOK = ((0, KV_HALF, True, 1.0, ((0, 0, 0),)), (KV_HALF, KV_HALF, False, 1.0, ((1, 0, 0),)))
KV_OUTS_TOK = ((KV_HALF, F32, 0), (KV_HALF, F32, 0))
ATTN_SEGS_TOK = ((0, N_DGROUPS * ATT_WIDTH, True, _Q_SCALE, ((0, 0, 0),)),
                 (N_DGROUPS * ATT_WIDTH, ATT_WIDTH, False, 1.0, ((1, 0, 0),)))
ATTN_OUTS_TOK = ((N_DGROUPS * ATT_WIDTH, F32, 0), (ATT_WIDTH, F32, 0))
KV_SEGS_RES = tuple(
    (g * GROUP_KV, GROUP_KV, True, 1.0, ((0, g * GROUP_KV, 0), (2 + g, 0, _DILS[g])))
    for g in range(N_DGROUPS)) + tuple(
    (KV_HALF + g * GROUP_KV, GROUP_KV, False, 1.0, ((1, g * GROUP_KV, 0), (2 + N_DGROUPS + g, 0, _DILS[g])))
    for g in range(N_DGROUPS))
KV_OUTS_RES = KV_OUTS_TOK + tuple((GROUP_KV, BF16, d) for d in _DILS) * 2
ATTN_SEGS_RES = tuple((g * ATT_WIDTH, ATT_WIDTH, True, _Q_SCALE, ((g, 0, _DILS[g]),))
                      for g in range(N_DGROUPS)) + (
    (N_DGROUPS * ATT_WIDTH, ATT_WIDTH, False, 1.0, ((N_DGROUPS, 0, 0),)),)
ATTN_OUTS_RES = tuple((ATT_WIDTH, BF16, d) for d in _DILS) + ((ATT_WIDTH, F32, 0),)


def _trunk(x, p_all, wts, consts, rope_tabs, rope_rows, *, nbatch, seq, ssm0, conv0, caches):
    prompt = caches is None
    m = x.shape[0]
    h = x
    new_ssm, new_conv = [], []
    rows_in = SSD_CHUNK if prompt else seq
    nchunks = seq // rows_in
    for i in range(N_A_LAYERS):
        lw = wts["mamba"][i]
        z, xbc, dtp = _norm_proj(h, wts["norm_w"][i], lw["in_w"], MAMBA_SEGS, MAMBA_OUTS)
        y, h_fin = _ssd(z, xbc, dtp, lw, consts, ssm0[i], conv0[i], nbatch=nbatch,
                        rows_in=rows_in, nchunks=nchunks, y_dtype=BF16 if prompt else F32)
        new_ssm.append(h_fin.reshape(nbatch, SSM_HEADS, SSM_HEADDIM, D_STATE))
        new_conv.append(xbc.reshape(nbatch, seq, CONV_DIM)[:, seq - (CONV_K - 1):])
        h = _out_ple([y], "mamba", h, p_all, i, lw["out_w"], wts["ple_norm_w"][i],
                     wts["ple_gate_w"][i], wts["ple_w"][i])
    cos, sin = rope_tabs
    kv_outs = _norm_proj(h, wts["kv_norm_w"], wts["kv_w"],
                         KV_SEGS_RES if prompt else KV_SEGS_TOK,
                         KV_OUTS_RES if prompt else KV_OUTS_TOK,
                         rope_inputs=(wts["k_norm_w"], cos, sin, consts["bd"]), rope_rows=rope_rows,
                         nbatch=nbatch)
    k, v = kv_outs[:2]
    for j in range(DEPTH - N_A_LAYERS):
        i = N_A_LAYERS + j
        aw = wts["attn"][j]
        q_outs = _norm_proj(h, wts["norm_w"][i], aw["in_w"],
                            ATTN_SEGS_RES if prompt else ATTN_SEGS_TOK,
                            ATTN_OUTS_RES if prompt else ATTN_OUTS_TOK,
                            rope_inputs=(aw["q_norm_w"], cos, sin, consts["bd"]), rope_rows=rope_rows,
                            nbatch=nbatch)
        gate = q_outs[-1]
        if prompt:
            os_, ls_ = [], []
            for gi in range(N_DGROUPS):
                o_g, l_g = _attn_prompt_group(q_outs[gi], kv_outs[2 + gi], kv_outs[2 + N_DGROUPS + gi], gi)
                os_.append(o_g)
                ls_.append(l_g)
            mix_inputs, mode = os_ + ls_ + [gate, consts["lse_spread"]], "attn_merge"
        else:
            o = _attn_sample(q_outs[0], k, v, caches, nbatch, seq)
            mix_inputs, mode = [o, gate], "attn"
        h = _out_ple(mix_inputs, mode, h, p_all, i, aw["out_w"], wts["ple_norm_w"][i],
                     wts["ple_gate_w"][i], wts["ple_w"][i])
    return h, jnp.stack(new_ssm, axis=0), jnp.stack(new_conv, axis=0), k, v


def kernel(x_prompt, x_sample, state_ssm, state_conv, cache_kv_g1, cache_kv_g2, cache_kv_g3,
           p_prompt, p_sample, norm_w, m_in_w, m_conv_w, m_conv_b, m_dt_bias, m_A_log, m_D,
           m_norm_w, m_out_w, kv_norm_w, kv_w, k_norm_w, a_in_w, a_q_norm_w, a_out_w,
           ple_w, ple_gate_w, ple_norm_w):
    wts = _prep_weights(norm_w, m_in_w, m_conv_w, m_conv_b, m_dt_bias, m_A_log, m_D, m_norm_w,
                        m_out_w, kv_norm_w, kv_w, k_norm_w, a_in_w, a_q_norm_w, a_out_w, ple_w,
                        ple_gate_w, ple_norm_w)
    consts = _constants()

    b_p, seq = x_prompt.shape[0], x_prompt.shape[1]
    m_p = b_p * seq
    ssm0 = jnp.zeros((N_A_LAYERS, b_p, D_INNER, D_STATE), F32)
    conv0 = jnp.zeros((N_A_LAYERS, b_p, SUBLANES, CONV_DIM), F32)
    tabs_p = _rope_tables(jnp.arange(seq, dtype=jnp.int32))
    y_p, ssm_p, conv_p, k_p, v_p = _trunk(
        x_prompt.reshape(m_p, D_MODEL), p_prompt.reshape(DEPTH, m_p, PLE_DIM), wts, consts,
        tabs_p, seq, nbatch=b_p, seq=seq, ssm0=ssm0, conv0=conv0, caches=None)

    b_s, dec = x_sample.shape[0], x_sample.shape[1]
    m_s = b_s * dec
    pos_s = PAST_LEN + jnp.arange(dec, dtype=jnp.int32)
    tabs_s = tuple(jnp.tile(t, (b_s, 1)) for t in _rope_tables(pos_s))
    ssm0_s = state_ssm.reshape(N_A_LAYERS, b_s, D_INNER, D_STATE)
    conv0_s = jnp.pad(state_conv, ((0, 0), (0, 0), (SUBLANES - (CONV_K - 1), 0), (0, 0)))
    caches = tuple(cch.reshape(b_s, cch.shape[1], 2 * GROUP_KV)
                   for cch in (cache_kv_g1, cache_kv_g2, cache_kv_g3))
    y_s, ssm_s, conv_s, k_s, v_s = _trunk(
        x_sample.reshape(m_s, D_MODEL), p_sample.reshape(DEPTH, m_s, PLE_DIM), wts, consts,
        tabs_s, m_s, nbatch=b_s, seq=dec, ssm0=ssm0_s, conv0=conv0_s, caches=caches)

    def kv_out(k, v, nbatch, length, gi, keep):
        k5 = k.reshape(nbatch, length, N_DGROUPS, ATT_KV_HEADS, ATT_HEAD_DIM)
        v5 = v.reshape(nbatch, length, N_DGROUPS, ATT_KV_HEADS, ATT_HEAD_DIM)
        return jnp.stack([k5[:, length - keep:, gi], v5[:, length - keep:, gi]], axis=2)

    kv_p = [kv_out(k_p, v_p, b_p, seq, gi, min(w, seq)) for gi, (w, _) in enumerate(DILATION_GROUPS)]
    kv_s = [kv_out(k_s, v_s, b_s, dec, gi, dec) for gi in range(N_DGROUPS)]
    return (y_p.reshape(b_p, seq, D_MODEL), y_s.reshape(b_s, dec, D_MODEL),
            ssm_p, conv_p, ssm_s, conv_s, kv_p[0], kv_p[1], kv_p[2], kv_s[0], kv_s[1], kv_s[2])
```

```python
import functools

import jax
import jax.numpy as jnp
from jax import lax
from jax.experimental import pallas as pl
from jax.experimental.pallas import tpu as pltpu

F32 = jnp.float32
BF16 = jnp.bfloat16

D_MODEL = 1024
SEQ = 8192
DEPTH = 4
PAST_LEN = 8192
N_A_LAYERS = DEPTH // 2
D_INNER = 2048
SSM_HEADDIM = 64
SSM_HEADS = 32
SSM_GROUPS = 4
D_STATE = 128
CONV_K = 4
BC_DIM = 2 * SSM_GROUPS * D_STATE
CONV_DIM = D_INNER + BC_DIM
SSD_CHUNK = 128
ATT_HEAD_DIM = 64
ATT_HEADS = 16
ATT_KV_HEADS = 4
DILATION_GROUPS = ((128, 1), (512, 4), (2048, 16))
N_DGROUPS = 3
ATT_WIDTH = 1024
KV_HALF = N_DGROUPS * ATT_KV_HEADS * ATT_HEAD_DIM
GROUP_KV = ATT_KV_HEADS * ATT_HEAD_DIM
N_KEYS = 129
ROPE_THETA = 10000.0
PLE_DIM = 256
EPS = 1e-6
LOG2E = 1.4426950408889634
LN2 = 0.6931471805599453

LANES = 128
SUBLANES = 8
VMEM_LIMIT_BYTES = 56 * 1024 * 1024
ROW_TILE = 512
NEG_BIG = -1e30
RELAYOUT_SLOTS = 4
ATTN_Q_BLOCKS = 2
MAMBA_ROWS = 256
MAMBA_PROJ_PIECE = 256

HEAD_ORDER = (0, 4, 1, 5, 2, 6, 3, 7, 8, 12, 9, 13, 10, 14, 11, 15)


def _cparams(semantics):
    return pltpu.CompilerParams(dimension_semantics=semantics,
                                vmem_limit_bytes=VMEM_LIMIT_BYTES)


def _const_spec(shape):
    nd = len(shape)
    return pl.BlockSpec(shape, lambda *_: (0,) * nd, pipeline_mode=pl.Buffered(1))


def _split_bf16(x, parts):
    out = []
    rem = x
    for _ in range(parts):
        hi = rem.astype(BF16)
        out.append(hi)
        rem = rem - hi.astype(F32)
    return out


def _dot(a, b):
    return jnp.dot(a, b, preferred_element_type=F32)


def _dot_nt(a, b):
    return lax.dot_general(a, b, (((1,), (1,)), ((), ())), preferred_element_type=F32)


def _sigmoid(x):
    return 0.5 + 0.5 * jnp.tanh(0.5 * x)


def _silu(x):
    hx = 0.5 * x
    return hx + hx * jnp.tanh(hx)


def _lo_mask(rows):
    return lax.broadcasted_iota(jnp.int32, (rows, LANES), 1) < ATT_HEAD_DIM


def _head_norm_rope(y, hw, cos, sin, bd, scale):
    ms = _dot((y * y).astype(BF16), bd)
    yn = y * lax.rsqrt(ms + EPS) * (hw * scale)
    lane = lax.broadcasted_iota(jnp.int32, yn.shape, 1)
    partner = jnp.where((lane & 32) == 0, pltpu.roll(yn, 96, 1), pltpu.roll(yn, 32, 1))
    return yn * cos + partner * sin


def _norm_proj_kernel(*refs, segs, use_rope, n_out, tm):
    h_ref, nw_ref, w_ref = refs[:3]
    pos = 3
    if use_rope:
        hw_ref, cos_ref, sin_ref, bd_ref = refs[3:7]
        pos = 7
    out_refs = refs[pos:pos + n_out]
    scr = refs[pos + n_out] if len(refs) > pos + n_out else None
    x = h_ref[...]
    ms = jnp.mean(x * x, axis=-1, keepdims=True)
    u = (x * lax.rsqrt(ms + EPS) * nw_ref[...]).astype(BF16)
    slot = 0
    for start, width, rope, scale, sinks in segs:
        acc = _dot(u, w_ref[:, start:start + width])
        if not rope and all(dil == 0 for _, _, dil in sinks):
            for oi, col_off, _ in sinks:
                out_refs[oi][:, col_off:col_off + width] = acc.astype(out_refs[oi].dtype)
            continue
        for cb in range(width // LANES):
            val = acc[:, cb * LANES:(cb + 1) * LANES]
            if rope:
                val = _head_norm_rope(val, hw_ref[...], cos_ref[...], sin_ref[...], bd_ref[...], scale)
            for oi, col_off, dil in sinks:
                o_ref = out_refs[oi]
                cs = slice(col_off + cb * LANES, col_off + (cb + 1) * LANES)
                if dil == 0:
                    o_ref[:, cs] = val.astype(o_ref.dtype)
                elif dil == 1:
                    o_ref[0, :, cs] = val.astype(o_ref.dtype)
                else:
                    s = slot % RELAYOUT_SLOTS
                    slot += 1
                    scr[s] = val
                    for r in range(dil):
                        o_ref[r, :, cs] = scr[s, pl.ds(r, tm // dil, stride=dil), :].astype(o_ref.dtype)


def _norm_proj(h, nw, w, segs, out_defs, rope_inputs=None, rope_rows=None, nbatch=None):
    m = h.shape[0]
    tm = min(ROW_TILE, m)
    n = w.shape[1]
    use_rope = rope_inputs is not None
    in_specs = [pl.BlockSpec((tm, D_MODEL), lambda i: (i, 0)),
                _const_spec((1, D_MODEL)),
                _const_spec((D_MODEL, n))]
    args = [h, nw, w]
    if use_rope:
        hw, cos, sin, bd = rope_inputs
        nblk = rope_rows // tm
        in_specs += [_const_spec((1, LANES)),
                     pl.BlockSpec((tm, LANES), lambda i: (i % nblk, 0)),
                     pl.BlockSpec((tm, LANES), lambda i: (i % nblk, 0)),
                     _const_spec((LANES, LANES))]
        args += [hw, cos, sin, bd]
    out_shape, out_specs = [], []
    for width, dt, dil in out_defs:
        if dil == 0:
            out_shape.append(jax.ShapeDtypeStruct((m, width), dt))
            out_specs.append(pl.BlockSpec((tm, width), lambda i: (i, 0)))
        else:
            tiles = m // nbatch // tm
            out_shape.append(jax.ShapeDtypeStruct((nbatch, dil, m // nbatch // dil, width), dt))
            out_specs.append(pl.BlockSpec((None, dil, tm // dil, width),
                                          lambda i, tiles=tiles: (i // tiles, 0, i % tiles, 0)))
    scratch = []
    if any(dil > 1 for _, _, dil in out_defs):
        scratch.append(pltpu.VMEM((RELAYOUT_SLOTS, tm, LANES), F32))
    return pl.pallas_call(
        functools.partial(_norm_proj_kernel, segs=segs, use_rope=use_rope, n_out=len(out_defs), tm=tm),
        grid=(m // tm,),
        in_specs=in_specs,
        out_specs=tuple(out_specs),
        out_shape=tuple(out_shape),
        scratch_shapes=scratch,
        compiler_params=_cparams(("parallel",)),
        name="norm_proj",
    )(*args)


def _ssd_kernel(z_ref, xbc_ref, dt_ref, cw_ref, cb_ref, dtb_ref, alog_ref, dexp_ref, nw_ref,
                tril_ref, e_ref, h0_ref, c0_ref, y_ref, hout_ref, ht_scr, xpad_scr,
                *, rows_in, nchunks):
    q = SSD_CHUNK
    c = pl.program_id(1)

    @pl.when(c == 0)
    def _():
        ht_scr[...] = jnp.transpose(h0_ref[...])
        xpad_scr[...] = c0_ref[...]

    def pad_rows(v):
        if rows_in == q:
            return v
        return jnp.concatenate([v, jnp.zeros((q - rows_in, v.shape[1]), v.dtype)], axis=0)

    y = _ssd_chunk(pad_rows(z_ref[...]), pad_rows(xbc_ref[...]), pad_rows(dt_ref[...]),
                   cw_ref, cb_ref, dtb_ref, alog_ref, dexp_ref, nw_ref, tril_ref, e_ref,
                   ht_scr, xpad_scr, rows_in)
    y_ref[...] = y[:rows_in].astype(y_ref.dtype)

    @pl.when(c == nchunks - 1)
    def _():
        hout_ref[...] = jnp.transpose(ht_scr[...])


def _ssd_chunk(zz, xbc, dt_raw, cw_ref, cb_ref, dtb_ref, alog_ref, dexp_ref, nw_ref, tril_ref, e_ref,
               ht_scr, xpad_scr, rows_valid, side=None):
    q = SSD_CHUNK
    gw = D_INNER // SSM_GROUPS

    def run_side(n=1):
        for _ in range(n):
            if side:
                side.pop(0)()

    tail = xpad_scr[...]
    xbc3 = xbc.reshape(q // SUBLANES, SUBLANES, CONV_DIM)
    row8 = lax.broadcasted_iota(jnp.int32, (1, SUBLANES, CONV_DIM), 1)
    conv = cb_ref[...] + cw_ref[CONV_K - 1:CONV_K, :] * xbc
    for k in range(CONV_K - 1):
        sh = CONV_K - 1 - k
        rot3 = pltpu.roll(xbc3, sh, 1)
        prev3 = jnp.concatenate([pltpu.roll(tail, sh, 0)[None], rot3[:-1]], axis=0)
        shifted = jnp.where(row8 < sh, prev3, rot3).reshape(q, CONV_DIM)
        conv = conv + cw_ref[k:k + 1, :] * shifted
    xpad_scr[...] = xbc[q - SUBLANES:q]
    run_side()
    act = _silu(conv)
    xs = act[:, :D_INNER]
    bm = act[:, D_INNER:D_INNER + SSM_GROUPS * D_STATE]
    cm = act[:, D_INNER + SSM_GROUPS * D_STATE:]

    dtr = dt_raw + dtb_ref[...]
    dt = jnp.maximum(dtr, 0.0) + jnp.log1p(jnp.exp(-jnp.abs(dtr)))
    if rows_valid < q:
        row = lax.broadcasted_iota(jnp.int32, dt.shape, 0)
        dt = jnp.where(row < rows_valid, dt, 0.0)
    a = dt * (-LOG2E * jnp.exp(alog_ref[...]))
    tril = tril_ref[...]
    acum = sum(_dot(tril, part) for part in _split_bf16(a, 3))
    acum_t = jnp.transpose(acum)

    e = e_ref[...]
    grp = lax.broadcasted_iota(jnp.int32, (q, LANES), 1) >> 5
    zero_l = jnp.zeros((q, LANES), BF16)

    def packed(parts):
        out = zero_l
        for idx, part in enumerate(parts):
            out = jnp.where(grp == idx, part, out)
        return out

    dt_e = _dot(packed(_split_bf16(dt, 2)), e)
    acum_e = _dot(packed(_split_bf16(acum, 3)), e)
    alast_e = acum_e[q - 1:q, :]
    exp_acum_e = jnp.exp2(acum_e)
    decay_end_e = jnp.exp2(alast_e - acum_e)
    chunk_decay_e = jnp.exp2(alast_e)

    xdt = xs * dt_e
    xdt_bf = xdt.astype(BF16)
    xdtw_bf = (xdt * decay_end_e).astype(BF16)

    li = lax.broadcasted_iota(jnp.int32, (q, q), 0)
    si = lax.broadcasted_iota(jnp.int32, (q, q), 1)
    causal = li >= si
    lo = _lo_mask(q)
    zero_bf = jnp.zeros((q, LANES), BF16)

    y_parts = []
    for g in range(SSM_GROUPS):
        gs = slice(g * gw, (g + 1) * gw)
        bg = bm[:, g * D_STATE:(g + 1) * D_STATE]
        cg_bf = cm[:, g * D_STATE:(g + 1) * D_STATE].astype(BF16)
        cb = jnp.where(causal, _dot_nt(cg_bf, bg.astype(BF16)), 0.0)
        htg = ht_scr[:, gs]
        y_off = _dot(cg_bf, htg.astype(BF16)) * exp_acum_e[:, gs]
        blocks = []
        for j in range(gw // LANES):
            hd = g * (SSM_HEADS // SSM_GROUPS) + 2 * j
            xpair = xdt_bf[:, hd * SSM_HEADDIM:hd * SSM_HEADDIM + LANES]
            mats = []
            for hh in (hd, hd + 1):
                seg = acum[:, hh:hh + 1] - acum_t[hh:hh + 1, :]
                mats.append((cb * jnp.exp2(jnp.minimum(seg, 0.0))).astype(BF16))
            xstack = jnp.concatenate([jnp.where(lo, xpair, zero_bf), jnp.where(lo, zero_bf, xpair)], axis=0)
            blocks.append(_dot(jnp.concatenate(mats, axis=1), xstack))
            run_side()
        y_diag = jnp.concatenate(blocks, axis=1)
        bgt_bf = jnp.transpose(bg).astype(BF16)
        st = _dot(bgt_bf, xdtw_bf[:, gs])
        ht_scr[:, gs] = htg * chunk_decay_e[:, gs] + st
        y_parts.append(y_diag + y_off)

    y = jnp.concatenate(y_parts, axis=1) + xs * dexp_ref[...]
    y = y * _silu(zz)
    run_side()
    normed = []
    for g in range(SSM_GROUPS):
        yg = y[:, g * gw:(g + 1) * gw]
        ms = jnp.mean(yg * yg, axis=-1, keepdims=True)
        normed.append(yg * lax.rsqrt(ms + EPS))
    return jnp.concatenate(normed, axis=1) * nw_ref[...]


def _ssd(z, xbc, dtp, lw, consts, h0, c0, layer, *, nbatch, rows_in, nchunks, y_dtype):
    m = z.shape[0]
    row_map = lambda b, c: (b * nchunks + c, 0)
    batch_map = lambda b, c: (b, 0, 0)
    state_map = lambda b, c: (layer, b, 0, 0)
    in_specs = [
        pl.BlockSpec((rows_in, D_INNER), row_map),
        pl.BlockSpec((rows_in, CONV_DIM), row_map),
        pl.BlockSpec((rows_in, LANES), row_map),
        _const_spec((CONV_K, CONV_DIM)),
        _const_spec((1, CONV_DIM)),
        _const_spec((1, LANES)),
        _const_spec((1, LANES)),
        _const_spec((1, D_INNER)),
        _const_spec((1, D_INNER)),
        _const_spec((SSD_CHUNK, SSD_CHUNK)),
        _const_spec((LANES, D_INNER)),
        pl.BlockSpec((None, None, D_INNER, D_STATE), state_map),
        pl.BlockSpec((None, None, SUBLANES, CONV_DIM), state_map),
    ]
    out_specs = (pl.BlockSpec((rows_in, D_INNER), row_map),
                 pl.BlockSpec((None, D_INNER, D_STATE), batch_map))
    out_shape = (jax.ShapeDtypeStruct((m, D_INNER), y_dtype),
                 jax.ShapeDtypeStruct((nbatch, D_INNER, D_STATE), F32))
    return pl.pallas_call(
        functools.partial(_ssd_kernel, rows_in=rows_in, nchunks=nchunks),
        grid=(nbatch, nchunks),
        in_specs=in_specs,
        out_specs=out_specs,
        out_shape=out_shape,
        scratch_shapes=[pltpu.VMEM((D_STATE, D_INNER), F32),
                        pltpu.VMEM((SUBLANES, CONV_DIM), F32)],
        compiler_params=_cparams(("parallel", "arbitrary")),
        name="ssd",
    )(z, xbc, dtp, lw["conv_w"], lw["conv_b"], lw["dt_bias"], lw["a_log"], lw["d_exp"],
      lw["norm_w"], consts["tril"], consts["expand"], h0, c0)


def _mamba_layer_kernel(ha_ref, hc_ref, p_ref, nw_ref, win_ref, cw_ref, cb_ref, dtb_ref, alog_ref,
                        dexp_ref, mnw_ref, tril_ref, e_ref, wout_ref, pnw_ref, gw_ref, pw_ref,
                        out_ref, sout_ref, cout_ref,
                        proj0, proj1, y0, y1, ht_scr, xpad_scr, *, nsteps, rows):
    s = pl.program_id(1)
    q = SSD_CHUNK

    @pl.when(s == 0)
    def _():
        proj1[...] = jnp.zeros(proj1.shape, proj1.dtype)
        y0[...] = jnp.zeros(y0.shape, y0.dtype)
        y1[...] = jnp.zeros(y1.shape, y1.dtype)

    @pl.when(s <= 1)
    def _():
        ht_scr[...] = jnp.zeros(ht_scr.shape, ht_scr.dtype)
        xpad_scr[...] = jnp.zeros(xpad_scr.shape, xpad_scr.dtype)

    def stages(proj_w, proj_r, y_w, y_r):
        vals = {}

        def a_norm():
            x = ha_ref[...]
            ms = jnp.mean(x * x, axis=-1, keepdims=True)
            vals["u"] = (x * lax.rsqrt(ms + EPS) * nw_ref[...]).astype(BF16)

        def a_piece(c0, c1):
            def emit():
                proj_w[:, c0:c1] = _dot(vals["u"], win_ref[:, c0:c1])
            return emit

        def c_out(c0, c1):
            def emit():
                vals["h1", c0] = hc_ref[:, c0:c1] + _dot(y_r[...], wout_ref[:, c0:c1])
            return emit

        def c_norm():
            h1 = jnp.concatenate([vals["h1", c0] for c0 in c_cols], axis=1)
            vals["h1"] = h1
            ms = jnp.mean(h1 * h1, axis=-1, keepdims=True)
            vals["un"] = (h1 * lax.rsqrt(ms + EPS) * pnw_ref[...]).astype(BF16)

        def c_gate(c0, c1):
            def emit():
                gate = _sigmoid(_dot(vals["un"], gw_ref[:, c0:c1]))
                pe = _dot(p_ref[...].astype(BF16), pw_ref[:, c0:c1])
                out_ref[:, c0:c1] = vals["h1"][:, c0:c1] + gate * pe
            return emit

        n_in = proj_w.shape[1]
        step = MAMBA_PROJ_PIECE
        a_work = [a_piece(c0, min(n_in, c0 + step)) for c0 in range(0, n_in, step)]
        c_cols = list(range(0, D_MODEL, step))
        c_work = ([c_out(c0, c0 + step) for c0 in c_cols] + [c_norm]
                  + [c_gate(c0, c0 + step) for c0 in c_cols])
        side = [a_norm]
        while a_work or c_work:
            if c_work:
                side.append(c_work.pop(0))
            if a_work:
                side.append(a_work.pop(0))
            if a_work:
                side.append(a_work.pop(0))
        for j in range(rows // q):
            rs = slice(j * q, (j + 1) * q)
            y = _ssd_chunk(proj_r[rs, 0:D_INNER], proj_r[rs, D_INNER:D_INNER + CONV_DIM],
                           proj_r[rs, D_INNER + CONV_DIM:], cw_ref, cb_ref, dtb_ref, alog_ref,
                           dexp_ref, mnw_ref, tril_ref, e_ref, ht_scr, xpad_scr, q, side=side)
            y_w[rs, :] = y.astype(y_w.dtype)
        while side:
            side.pop(0)()

    @pl.when(s % 2 == 0)
    def _():
        stages(proj0, proj1, y1, y0)

    @pl.when(s % 2 == 1)
    def _():
        stages(proj1, proj0, y0, y1)

    @pl.when(s == nsteps)
    def _():
        sout_ref[...] = jnp.transpose(ht_scr[...])
        cout_ref[...] = xpad_scr[...]


def _mamba_layer(h, p_all, layer, nw, lw, consts, pnw, gw, pw, *, nbatch, seq):
    m = h.shape[0]
    rows = MAMBA_ROWS
    nsteps = seq // rows
    n_in = lw["in_w"].shape[1]
    a_map = lambda b, s: (b * nsteps + jnp.minimum(s, nsteps - 1), 0)
    c_map = lambda b, s: (b * nsteps + jnp.clip(s - 2, 0, nsteps - 1), 0)
    in_specs = [
        pl.BlockSpec((rows, D_MODEL), a_map),
        pl.BlockSpec((rows, D_MODEL), c_map),
        pl.BlockSpec((None, rows, PLE_DIM), lambda b, s: (layer, b * nsteps + jnp.clip(s - 2, 0, nsteps - 1), 0)),
        _const_spec((1, D_MODEL)),
        _const_spec((D_MODEL, n_in)),
        _const_spec((CONV_K, CONV_DIM)),
        _const_spec((1, CONV_DIM)),
        _const_spec((1, LANES)),
        _const_spec((1, LANES)),
        _const_spec((1, D_INNER)),
        _const_spec((1, D_INNER)),
        _const_spec((SSD_CHUNK, SSD_CHUNK)),
        _const_spec((LANES, D_INNER)),
        _const_spec((D_INNER, D_MODEL)),
        _const_spec((1, D_MODEL)),
        _const_spec((D_MODEL, D_MODEL)),
        _const_spec((PLE_DIM, D_MODEL)),
    ]
    out_specs = (pl.BlockSpec((rows, D_MODEL), c_map),
                 pl.BlockSpec((None, D_INNER, D_STATE), lambda b, s: (b, 0, 0)),
                 pl.BlockSpec((None, SUBLANES, CONV_DIM), lambda b, s: (b, 0, 0)))
    out_shape = (jax.ShapeDtypeStruct((m, D_MODEL), F32),
                 jax.ShapeDtypeStruct((nbatch, D_INNER, D_STATE), F32),
                 jax.ShapeDtypeStruct((nbatch, SUBLANES, CONV_DIM), F32))
    return pl.pallas_call(
        functools.partial(_mamba_layer_kernel, nsteps=nsteps, rows=rows),
        grid=(nbatch, nsteps + 2),
        in_specs=in_specs,
        out_specs=out_specs,
        out_shape=out_shape,
        scratch_shapes=[pltpu.VMEM((rows, n_in), F32), pltpu.VMEM((rows, n_in), F32),
                        pltpu.VMEM((rows, D_INNER), BF16), pltpu.VMEM((rows, D_INNER), BF16),
                        pltpu.VMEM((D_STATE, D_INNER), F32),
                        pltpu.VMEM((SUBLANES, CONV_DIM), F32)],
        compiler_params=_cparams(("parallel", "arbitrary")),
        name="mamba_layer",
    )(h, h, p_all, nw, lw["in_w"], lw["conv_w"], lw["conv_b"], lw["dt_bias"], lw["a_log"],
      lw["d_exp"], lw["norm_w"], consts["tril"], consts["expand"], lw["out_w"], pnw, gw, pw)


def _attn_prompt_kernel(q_ref, kcur_ref, kprev_ref, vcur_ref, vprev_ref, o_ref, l_ref, *, nq):
    i = pl.program_id(2)
    t = SSD_CHUNK
    npair = GROUP_KV // ATT_HEAD_DIM
    qi = lax.broadcasted_iota(jnp.int32, (t, 2 * t), 0)
    kk = lax.broadcasted_iota(jnp.int32, (t, 2 * t), 1)
    band = (kk >= qi) & (kk <= qi + (N_KEYS - 1))
    bias_inner = jnp.where(band, 0.0, NEG_BIG)
    bias_first = jnp.where(band & ((kk >= t) | (i > 0)), 0.0, NEG_BIG)
    lo = _lo_mask(t)
    lo2 = _lo_mask(2 * t)
    zero_v = jnp.zeros((2 * t, LANES), BF16)
    ones_stack = jnp.concatenate([jnp.where(lo2, 1.0, 0.0), jnp.where(lo2, 0.0, 1.0)],
                                 axis=0).astype(BF16)
    lane = lax.broadcasted_iota(jnp.int32, (t, LANES), 1)
    lane_slot = (lane & (ATT_HEAD_DIM - 1)) >> 3
    for s in range(nq):
        rows = slice(s * t, (s + 1) * t)
        bias = bias_first if s == 0 else bias_inner
        l_c = jnp.zeros((t, LANES), F32)
        for kc in range(GROUP_KV // LANES):
            ks = slice(kc * LANES, (kc + 1) * LANES)
            if s == 0:
                k2 = jnp.concatenate([kprev_ref[:, ks], kcur_ref[0:t, ks]], axis=0)
                v2 = jnp.concatenate([vprev_ref[:, ks], vcur_ref[0:t, ks]], axis=0)
            else:
                k2 = kcur_ref[(s - 1) * t:(s + 1) * t, ks]
                v2 = vcur_ref[(s - 1) * t:(s + 1) * t, ks]
            k_stack = jnp.concatenate([jnp.where(lo2, k2, zero_v), jnp.where(lo2, zero_v, k2)], axis=0)
            v_stack = jnp.concatenate([jnp.where(lo2, v2, zero_v), jnp.where(lo2, zero_v, v2)], axis=0)
            rhs = jnp.concatenate([v_stack, ones_stack], axis=1)
            lhs = jnp.concatenate(
                [q_ref[rows, (kc * npair + pb) * LANES:(kc * npair + pb + 1) * LANES] for pb in range(npair)],
                axis=0)
            sc = _dot_nt(lhs, k_stack).reshape(npair, t, 4 * t)
            sc_a = sc[:, :, :2 * t] + bias[None]
            sc_b = sc[:, :, 2 * t:] + bias[None]
            mx_a = jnp.max(sc_a, axis=-1, keepdims=True)
            mx_b = jnp.max(sc_b, axis=-1, keepdims=True)
            p_cat = jnp.concatenate([jnp.exp2(sc_a - mx_a), jnp.exp2(sc_b - mx_b)], axis=-1)
            res = _dot(p_cat.astype(BF16).reshape(npair * t, 4 * t), rhs)
            for pb in range(npair):
                blk = kc * npair + pb
                num = res[pb * t:(pb + 1) * t, :LANES]
                den = res[pb * t:(pb + 1) * t, LANES:]
                o_ref[rows, blk * LANES:(blk + 1) * LANES] = num / den
                lse = jnp.where(lo, mx_a[pb], mx_b[pb]) * LN2 + jnp.log(den)
                l_c = jnp.where(lane_slot == blk, lse, l_c)
        l_ref[rows, :] = l_c


def _attn_prompt_group(q, k, v, gi):
    nbatch, dil, rows, _ = q.shape
    t = SSD_CHUNK
    nq = ATTN_Q_BLOCKS
    cur = lambda b, r, i: (b, r, i, 0)
    prev = lambda b, r, i: (b, r, jnp.maximum(nq * i - 1, 0), 0)
    return pl.pallas_call(
        functools.partial(_attn_prompt_kernel, nq=nq),
        grid=(nbatch, dil, rows // (nq * t)),
        in_specs=[pl.BlockSpec((None, None, nq * t, ATT_WIDTH), cur),
                  pl.BlockSpec((None, None, nq * t, GROUP_KV), cur),
                  pl.BlockSpec((None, None, t, GROUP_KV), prev),
                  pl.BlockSpec((None, None, nq * t, GROUP_KV), cur),
                  pl.BlockSpec((None, None, t, GROUP_KV), prev)],
        out_specs=(pl.BlockSpec((None, None, nq * t, ATT_WIDTH), cur),
                   pl.BlockSpec((None, None, nq * t, LANES), cur)),
        out_shape=(jax.ShapeDtypeStruct((nbatch, dil, rows, ATT_WIDTH), F32),
                   jax.ShapeDtypeStruct((nbatch, dil, rows, LANES), F32)),
        compiler_params=_cparams(("parallel", "parallel", "arbitrary")),
        name=f"attn_prompt_g{gi}",
    )(q, k, k, v, v)


def _attn_sample_kernel(q_ref, kn_ref, vn_ref, c1_ref, c2_ref, c3_ref, o_ref, *, dec_seq):
    caches = (c1_ref, c2_ref, c3_ref)
    nrow = 8 * dec_seq
    lo_row = _lo_mask(dec_seq)
    pad = LANES - dec_seq
    outs = [[None] * N_DGROUPS for _ in range(ATT_WIDTH // LANES)]
    lses = [[None] * N_DGROUPS for _ in range(ATT_WIDTH // LANES)]
    for gi, (_, dil) in enumerate(DILATION_GROUPS):
        cref = caches[gi]
        w = cref.shape[1]
        tq = lax.broadcasted_iota(jnp.int32, (nrow, w), 0) & (dec_seq - 1)
        rho = lax.broadcasted_iota(jnp.int32, (nrow, w), 1)
        delta = w + tq - rho
        valid_c = (rho >= tq) & ((delta & (dil - 1)) == 0) & (delta <= (N_KEYS - 1) * dil)
        bias_c = jnp.where(valid_c, 0.0, NEG_BIG)
        tq_n = lax.broadcasted_iota(jnp.int32, (nrow, LANES), 0) & (dec_seq - 1)
        tn = lax.broadcasted_iota(jnp.int32, (nrow, LANES), 1)
        valid_n = (tn <= tq_n) & (((tq_n - tn) & (dil - 1)) == 0)
        bias_n = jnp.where(valid_n, 0.0, NEG_BIG)
        gsl = slice(gi * GROUP_KV, (gi + 1) * GROUP_KV)
        k_new = jnp.concatenate([kn_ref[:, gsl], jnp.zeros((pad, GROUP_KV), F32)], axis=0)
        v_new = jnp.concatenate([vn_ref[:, gsl], jnp.zeros((pad, GROUP_KV), F32)], axis=0)
        for kc in range(GROUP_KV // LANES):
            ks = slice(kc * LANES, (kc + 1) * LANES)
            rows = []
            for pb in range(4):
                blk = kc * 4 + pb
                q2 = q_ref[:, gi * ATT_WIDTH + blk * LANES:gi * ATT_WIDTH + (blk + 1) * LANES]
                rows.append(jnp.where(lo_row, q2, 0.0))
                rows.append(jnp.where(lo_row, 0.0, q2))
            lhs = jnp.concatenate(rows, axis=0).astype(BF16)
            kcache_t = cref[ks, :].astype(BF16)
            vcache_t = cref[GROUP_KV + kc * LANES:GROUP_KV + (kc + 1) * LANES, :].astype(BF16)
            s_c = _dot(lhs, kcache_t) + bias_c
            s_n = _dot_nt(lhs, k_new[:, ks].astype(BF16)) + bias_n
            mx = jnp.maximum(jnp.max(s_c, axis=-1, keepdims=True),
                             jnp.max(s_n, axis=-1, keepdims=True))
            p_c = jnp.exp(s_c - mx)
            p_n = jnp.exp(s_n - mx)
            den = jnp.sum(p_c, axis=-1, keepdims=True) + jnp.sum(p_n, axis=-1, keepdims=True)
            num = _dot_nt(p_c.astype(BF16), vcache_t) + _dot(p_n.astype(BF16), v_new[:, ks].astype(BF16))
            on = num / den
            lse = mx + jnp.log(den)
            for pb in range(4):
                blk = kc * 4 + pb
                r0 = pb * 2 * dec_seq
                r1 = r0 + dec_seq
                outs[blk][gi] = jnp.where(lo_row, on[r0:r1], on[r1:r1 + dec_seq])
                lses[blk][gi] = jnp.where(lo_row, lse[r0:r1], lse[r1:r1 + dec_seq])
    for blk in range(ATT_WIDTH // LANES):
        ls = lses[blk]
        mx = jnp.maximum(jnp.maximum(ls[0], ls[1]), ls[2])
        ws = [jnp.exp(l - mx) for l in ls]
        tot = ws[0] + ws[1] + ws[2]
        o = (outs[blk][0] * ws[0] + outs[blk][1] * ws[1] + outs[blk][2] * ws[2]) / tot
        o_ref[:, blk * LANES:(blk + 1) * LANES] = o


def _attn_sample(q, k, v, caches, nbatch, dec_seq):
    m = q.shape[0]
    row_map = lambda b: (b, 0)
    in_specs = [pl.BlockSpec((dec_seq, N_DGROUPS * ATT_WIDTH), row_map),
                pl.BlockSpec((dec_seq, KV_HALF), row_map),
                pl.BlockSpec((dec_seq, KV_HALF), row_map)]
    for cch in caches:
        in_specs.append(pl.BlockSpec((None, 2 * GROUP_KV, cch.shape[2]), lambda b: (b, 0, 0)))
    return pl.pallas_call(
        functools.partial(_attn_sample_kernel, dec_seq=dec_seq),
        grid=(nbatch,),
        in_specs=in_specs,
        out_specs=pl.BlockSpec((dec_seq, ATT_WIDTH), row_map),
        out_shape=jax.ShapeDtypeStruct((m, ATT_WIDTH), F32),
        compiler_params=_cparams(("parallel",)),
        name="attn_sample",
    )(q, k, v, *caches)


def _token_order(ref, scr, slot, cb, dil, tm):
    cs = slice(cb * LANES, (cb + 1) * LANES)
    if dil == 1:
        return ref[0, :, cs]
    for r in range(dil):
        scr[slot, pl.ds(r, tm // dil, stride=dil), :] = ref[r, :, cs]
    return scr[slot]


def _out_ple_kernel(*refs, mode, tm):
    if mode == "mamba":
        (y_ref,) = refs[:1]
        pos = 1
        mix = y_ref[...].astype(BF16)
    elif mode == "attn_merge":
        o_refs = refs[0:3]
        l_refs = refs[3:6]
        gate_ref, x_ref = refs[6:8]
        pos = 8
        scr, mix_scr = refs[-2:]
        refs = refs[:-2]
        dils = [d for _, d in DILATION_GROUPS]
        ls = [_token_order(l_refs[g], scr, g, 0, dils[g], tm) for g in range(N_DGROUPS)]
        mx = jnp.maximum(jnp.maximum(ls[0], ls[1]), ls[2])
        es = [jnp.exp(l - mx) for l in ls]
        tot = es[0] + es[1] + es[2]
        sub = lax.broadcasted_iota(jnp.int32, (tm, LANES), 1) & 7
        packed = jnp.zeros((tm, LANES), BF16)
        for g in range(N_DGROUPS):
            for idx, part in enumerate(_split_bf16(es[g] / tot, 2)):
                packed = jnp.where(sub == 2 * g + idx, part, packed)
        ws_all = _dot(packed, x_ref[...])
        ws = [ws_all[:, g * ATT_WIDTH:(g + 1) * ATT_WIDTH] for g in range(N_DGROUPS)]
        for cb in range(ATT_WIDTH // LANES):
            cs = slice(cb * LANES, (cb + 1) * LANES)
            o = None
            for g in range(N_DGROUPS):
                og = _token_order(o_refs[g], scr, N_DGROUPS + (cb * N_DGROUPS + g) % RELAYOUT_SLOTS,
                                  cb, dils[g], tm)
                term = og * ws[g][:, cs]
                o = term if o is None else o + term
            gate = gate_ref[:, cs]
            mix_scr[:, cs] = (o * _silu(gate)).astype(BF16)
        mix = mix_scr[...]
    else:
        o_ref_in, gate_ref = refs[:2]
        pos = 2
        gate = gate_ref[...]
        mix = (o_ref_in[...] * _silu(gate)).astype(BF16)
    h_ref, p_ref, wout_ref, pnw_ref, gw_ref, pw_ref, out_ref = refs[pos:]
    out_ref[...] = _residual_ple(mix, h_ref[...], p_ref[...], wout_ref, pnw_ref, gw_ref, pw_ref)


def _residual_ple(mix, h, p, wout_ref, pnw_ref, gw_ref, pw_ref):
    h1 = h + _dot(mix, wout_ref[...])
    ms = jnp.mean(h1 * h1, axis=-1, keepdims=True)
    un = (h1 * lax.rsqrt(ms + EPS) * pnw_ref[...]).astype(BF16)
    gate_p = _sigmoid(_dot(un, gw_ref[...]))
    pe = _dot(p.astype(BF16), pw_ref[...])
    return h1 + gate_p * pe


def _out_ple(mix_inputs, mode, h, p_all, layer, wout, pnw, gw, pw):
    m = h.shape[0]
    tm = min(ROW_TILE, m)
    row_map = lambda i: (i, 0)
    in_specs = []
    for a in mix_inputs:
        if a.ndim == 4:
            _, dil, rows, width = a.shape
            tiles = rows * dil // tm
            in_specs.append(pl.BlockSpec((None, dil, tm // dil, width),
                                         lambda i, tiles=tiles: (i // tiles, 0, i % tiles, 0)))
        elif a.shape[0] == m:
            in_specs.append(pl.BlockSpec((tm, a.shape[1]), row_map))
        else:
            in_specs.append(_const_spec(a.shape))
    scratch = []
    if mode == "attn_merge":
        scratch = [pltpu.VMEM((N_DGROUPS + RELAYOUT_SLOTS, tm, LANES), F32),
                   pltpu.VMEM((tm, ATT_WIDTH), BF16)]
    in_specs += [pl.BlockSpec((tm, D_MODEL), row_map),
                 pl.BlockSpec((None, tm, PLE_DIM), lambda i: (layer, i, 0)),
                 _const_spec(wout.shape),
                 _const_spec((1, D_MODEL)),
                 _const_spec((D_MODEL, D_MODEL)),
                 _const_spec((PLE_DIM, D_MODEL))]
    return pl.pallas_call(
        functools.partial(_out_ple_kernel, mode=mode, tm=tm),
        grid=(m // tm,),
        in_specs=in_specs,
        out_specs=pl.BlockSpec((tm, D_MODEL), row_map),
        out_shape=jax.ShapeDtypeStruct((m, D_MODEL), F32),
        scratch_shapes=scratch,
        compiler_params=_cparams(("parallel",)),
        name=f"out_ple_{mode}",
    )(*mix_inputs, h, p_all, wout, pnw, gw, pw)


def _rope_tables(pos):
    half = ATT_HEAD_DIM // 2
    inv = 1.0 / (ROPE_THETA ** (jnp.arange(half, dtype=F32) / half))
    ang = pos.astype(F32)[:, None] * inv[None, :]
    cos, sin = jnp.cos(ang), jnp.sin(ang)
    cos128 = jnp.concatenate([cos, cos, cos, cos], axis=1)
    sin128 = jnp.concatenate([-sin, sin, -sin, sin], axis=1)
    return cos128, sin128


def _permute_heads(w, axis):
    shape = w.shape
    w = w.reshape(shape[:axis] + (ATT_HEADS, ATT_HEAD_DIM) + shape[axis + 1:])
    w = jnp.take(w, jnp.array(HEAD_ORDER, dtype=jnp.int32), axis=axis)
    return w.reshape(shape)


def _prep_weights(norm_w, m_in_w, m_conv_w, m_conv_b, m_dt_bias, m_A_log, m_D, m_norm_w, m_out_w,
                  kv_norm_w, kv_w, k_norm_w, a_in_w, a_q_norm_w, a_out_w, ple_w, ple_gate_w,
                  ple_norm_w):
    row = lambda v: v.reshape(1, -1).astype(F32)
    reps = LANES // SSM_HEADS
    lane_pad = lambda v: jnp.tile(v.astype(F32), reps).reshape(1, LANES)
    mamba = []
    for i in range(N_A_LAYERS):
        n_main = D_INNER + CONV_DIM
        w_in = jnp.concatenate([m_in_w[i][:, :n_main], jnp.tile(m_in_w[i][:, n_main:], (1, reps))],
                               axis=1).astype(BF16)
        mamba.append(dict(
            in_w=w_in,
            conv_w=m_conv_w[i].astype(F32),
            conv_b=row(m_conv_b[i]),
            dt_bias=lane_pad(m_dt_bias[i]),
            a_log=lane_pad(m_A_log[i]),
            d_exp=row(jnp.repeat(m_D[i], SSM_HEADDIM)),
            norm_w=row(m_norm_w[i]),
            out_w=m_out_w[i].astype(BF16),
        ))
    attn = []
    for j in range(DEPTH - N_A_LAYERS):
        w = a_in_w[j].reshape(D_MODEL, N_DGROUPS + 1, ATT_WIDTH)
        w = _permute_heads(w, 2).reshape(D_MODEL, (N_DGROUPS + 1) * ATT_WIDTH)
        attn.append(dict(
            in_w=w.astype(BF16),
            q_norm_w=row(jnp.tile(a_q_norm_w[j], LANES // ATT_HEAD_DIM)),
            out_w=_permute_heads(a_out_w[j], 0).astype(BF16),
        ))
    return dict(
        norm_w=[row(norm_w[i]) for i in range(DEPTH)],
        mamba=mamba,
        attn=attn,
        kv_norm_w=row(kv_norm_w),
        kv_w=kv_w.astype(BF16),
        k_norm_w=row(jnp.tile(k_norm_w, LANES // ATT_HEAD_DIM)),
        ple_w=[ple_w[i].astype(BF16) for i in range(DEPTH)],
        ple_gate_w=[ple_gate_w[i].astype(BF16) for i in range(DEPTH)],
        ple_norm_w=[row(ple_norm_w[i]) for i in range(DEPTH)],
    )


def _constants():
    t = SSD_CHUNK
    tril = jnp.tril(jnp.ones((t, t), F32)).astype(BF16)
    head = jnp.arange(LANES, dtype=jnp.int32)[:, None]
    col = jnp.arange(D_INNER, dtype=jnp.int32)[None, :]
    expand = (head % SSM_HEADS == col // SSM_HEADDIM).astype(BF16)
    li = jnp.arange(LANES, dtype=jnp.int32)
    bd = ((li[:, None] // ATT_HEAD_DIM) == (li[None, :] // ATT_HEAD_DIM)).astype(F32) / ATT_HEAD_DIM
    ccol = jnp.arange(N_DGROUPS * ATT_WIDTH, dtype=jnp.int32)[None, :]
    cgrp, cc = ccol // ATT_WIDTH, ccol % ATT_WIDTH
    slot = cc // LANES + jnp.where(cc % LANES < ATT_HEAD_DIM, 0, 8)
    lrow = li[:, None]
    lse_spread = ((lrow >> 3 == slot) & ((lrow & 7) >> 1 == cgrp)).astype(BF16)
    return dict(tril=tril, expand=expand, bd=bd.astype(BF16), lse_spread=lse_spread)


_Q_SCALE = ATT_HEAD_DIM ** -0.5
_DILS = tuple(d for _, d in DILATION_GROUPS)
MAMBA_SEGS = ((0, D_INNER, False, 1.0, ((0, 0, 0),)),
              (D_INNER, CONV_DIM, False, 1.0, ((1, 0, 0),)),
              (D_INNER + CONV_DIM, LANES, False, 1.0, ((2, 0, 0),)))
MAMBA_OUTS = ((D_INNER, F32, 0), (CONV_DIM, F32, 0), (LANES, F32, 0))
KV_SEGS_TOK = ((0, KV_HALF, True, 1.0, ((0, 0, 0),)), (KV_HALF, KV_HALF, False, 1.0, ((1, 0, 0),)))
KV_OUTS_TOK = ((KV_HALF, F32, 0), (KV_HALF, F32, 0))
ATTN_SEGS_TOK = ((0, N_DGROUPS * ATT_WIDTH, True, _Q_SCALE, ((0, 0, 0),)),
                 (N_DGROUPS * ATT_WIDTH, ATT_WIDTH, False, 1.0, ((1, 0, 0),)))
ATTN_OUTS_TOK = ((N_DGROUPS * ATT_WIDTH, F32, 0), (ATT_WIDTH, F32, 0))
KV_SEGS_RES = tuple(
    (g * GROUP_KV, GROUP_KV, True, 1.0, ((0, g * GROUP_KV, 0), (2 + g, 0, _DILS[g])))
    for g in range(N_DGROUPS)) + tuple(
    (KV_HALF + g * GROUP_KV, GROUP_KV, False, 1.0, ((1, g * GROUP_KV, 0), (2 + N_DGROUPS + g, 0, _DILS[g])))
    for g in range(N_DGROUPS))
KV_OUTS_RES = KV_OUTS_TOK + tuple((GROUP_KV, BF16, d) for d in _DILS) * 2
ATTN_SEGS_RES = tuple((g * ATT_WIDTH, ATT_WIDTH, True, _Q_SCALE * LOG2E, ((g, 0, _DILS[g]),))
                      for g in range(N_DGROUPS)) + (
    (N_DGROUPS * ATT_WIDTH, ATT_WIDTH, False, 1.0, ((N_DGROUPS, 0, 0),)),)
ATTN_OUTS_RES = tuple((ATT_WIDTH, BF16, d) for d in _DILS) + ((ATT_WIDTH, F32, 0),)


def _trunk(x, p_all, wts, consts, rope_tabs, rope_rows, *, nbatch, seq, ssm0, conv0, caches):
    prompt = caches is None
    m = x.shape[0]
    h = x
    new_ssm, new_conv = [], []
    for i in range(N_A_LAYERS):
        lw = wts["mamba"][i]
        ple = (wts["ple_norm_w"][i], wts["ple_gate_w"][i], wts["ple_w"][i])
        if prompt:
            h, h_fin, xbc_tail = _mamba_layer(h, p_all, i, wts["norm_w"][i], lw, consts, *ple,
                                              nbatch=nbatch, seq=seq)
            new_conv.append(xbc_tail[:, SUBLANES - (CONV_K - 1):])
        else:
            z, xbc, dtp = _norm_proj(h, wts["norm_w"][i], lw["in_w"], MAMBA_SEGS, MAMBA_OUTS)
            y, h_fin = _ssd(z, xbc, dtp, lw, consts, ssm0, conv0, i, nbatch=nbatch,
                            rows_in=seq, nchunks=1, y_dtype=F32)
            new_conv.append(xbc.reshape(nbatch, seq, CONV_DIM)[:, seq - (CONV_K - 1):])
            h = _out_ple([y], "mamba", h, p_all, i, lw["out_w"], *ple)
        new_ssm.append(h_fin.reshape(nbatch, SSM_HEADS, SSM_HEADDIM, D_STATE))
    cos, sin = rope_tabs
    kv_outs = _norm_proj(h, wts["kv_norm_w"], wts["kv_w"],
                         KV_SEGS_RES if prompt else KV_SEGS_TOK,
                         KV_OUTS_RES if prompt else KV_OUTS_TOK,
                         rope_inputs=(wts["k_norm_w"], cos, sin, consts["bd"]), rope_rows=rope_rows,
                         nbatch=nbatch)
    k, v = kv_outs[:2]
    for j in range(DEPTH - N_A_LAYERS):
        i = N_A_LAYERS + j
        aw = wts["attn"][j]
        q_outs = _norm_proj(h, wts["norm_w"][i], aw["in_w"],
                            ATTN_SEGS_RES if prompt else ATTN_SEGS_TOK,
                            ATTN_OUTS_RES if prompt else ATTN_OUTS_TOK,
                            rope_inputs=(aw["q_norm_w"], cos, sin, consts["bd"]), rope_rows=rope_rows,
                            nbatch=nbatch)
        gate = q_outs[-1]
        if prompt:
            os_, ls_ = [], []
            for gi in range(N_DGROUPS):
                o_g, l_g = _attn_prompt_group(q_outs[gi], kv_outs[2 + gi], kv_outs[2 + N_DGROUPS + gi], gi)
                os_.append(o_g)
                ls_.append(l_g)
            mix_inputs, mode = os_ + ls_ + [gate, consts["lse_spread"]], "attn_merge"
        else:
            o = _attn_sample(q_outs[0], k, v, caches, nbatch, seq)
            mix_inputs, mode = [o, gate], "attn"
        h = _out_ple(mix_inputs, mode, h, p_all, i, aw["out_w"], wts["ple_norm_w"][i],
                     wts["ple_gate_w"][i], wts["ple_w"][i])
    return h, jnp.stack(new_ssm, axis=0), jnp.stack(new_conv, axis=0), k, v


def kernel(x_prompt, x_sample, state_ssm, state_conv, cache_kv_g1, cache_kv_g2, cache_kv_g3,
           p_prompt, p_sample, norm_w, m_in_w, m_conv_w, m_conv_b, m_dt_bias, m_A_log, m_D,
           m_norm_w, m_out_w, kv_norm_w, kv_w, k_norm_w, a_in_w, a_q_norm_w, a_out_w,
           ple_w, ple_gate_w, ple_norm_w):
    wts = _prep_weights(norm_w, m_in_w, m_conv_w, m_conv_b, m_dt_bias, m_A_log, m_D, m_norm_w,
                        m_out_w, kv_norm_w, kv_w, k_norm_w, a_in_w, a_q_norm_w, a_out_w, ple_w,
                        ple_gate_w, ple_norm_w)
    consts = _constants()

    b_p, seq = x_prompt.shape[0], x_prompt.shape[1]
    m_p = b_p * seq
    tabs_p = _rope_tables(jnp.arange(seq, dtype=jnp.int32))
    y_p, ssm_p, conv_p, k_p, v_p = _trunk(
        x_prompt.reshape(m_p, D_MODEL), p_prompt.reshape(DEPTH, m_p, PLE_DIM), wts, consts,
        tabs_p, seq, nbatch=b_p, seq=seq, ssm0=None, conv0=None, caches=None)

    b_s, dec = x_sample.shape[0], x_sample.shape[1]
    m_s = b_s * dec
    pos_s = PAST_LEN + jnp.arange(dec, dtype=jnp.int32)
    tabs_s = tuple(jnp.tile(t, (b_s, 1)) for t in _rope_tables(pos_s))
    ssm0_s = state_ssm.reshape(N_A_LAYERS, b_s, D_INNER, D_STATE)
    conv0_s = jnp.pad(state_conv, ((0, 0), (0, 0), (SUBLANES - (CONV_K - 1), 0), (0, 0)))
    caches = tuple(jnp.transpose(cch, (0, 2, 3, 4, 1)).reshape(b_s, 2 * GROUP_KV, cch.shape[1])
                   for cch in (cache_kv_g1, cache_kv_g2, cache_kv_g3))
    y_s, ssm_s, conv_s, k_s, v_s = _trunk(
        x_sample.reshape(m_s, D_MODEL), p_sample.reshape(DEPTH, m_s, PLE_DIM), wts, consts,
        tabs_s, m_s, nbatch=b_s, seq=dec, ssm0=ssm0_s, conv0=conv0_s, caches=caches)

    def kv_out(k, v, nbatch, length, gi, keep):
        def tail(a):
            a = a.reshape(nbatch, length, KV_HALF)[:, length - keep:, gi * GROUP_KV:(gi + 1) * GROUP_KV]
            return a.reshape(nbatch, keep, ATT_KV_HEADS, ATT_HEAD_DIM)
        return jnp.stack([tail(k), tail(v)], axis=2)

    kv_p = [kv_out(k_p, v_p, b_p, seq, gi, min(w, seq)) for gi, (w, _) in enumerate(DILATION_GROUPS)]
    kv_s = [kv_out(k_s, v_s, b_s, dec, gi, dec) for gi in range(N_DGROUPS)]
    return (y_p.reshape(b_p, seq, D_MODEL), y_s.reshape(b_s, dec, D_MODEL),
            ssm_p, conv_p, ssm_s, conv_s, kv_p[0], kv_p[1], kv_p[2], kv_s[0], kv_s[1], kv_s[2])
```

```python
import functools

import jax
import jax.numpy as jnp
from jax import lax
from jax.experimental import pallas as pl
from jax.experimental.pallas import tpu as pltpu

F32 = jnp.float32
BF16 = jnp.bfloat16

D_MODEL = 1024
SEQ = 8192
DEPTH = 4
PAST_LEN = 8192
N_A_LAYERS = DEPTH // 2
D_INNER = 2048
SSM_HEADDIM = 64
SSM_HEADS = 32
SSM_GROUPS = 4
D_STATE = 128
CONV_K = 4
BC_DIM = 2 * SSM_GROUPS * D_STATE
CONV_DIM = D_INNER + BC_DIM
SSD_CHUNK = 128
ATT_HEAD_DIM = 64
ATT_HEADS = 16
ATT_KV_HEADS = 4
DILATION_GROUPS = ((128, 1), (512, 4), (2048, 16))
N_DGROUPS = 3
ATT_WIDTH = 1024
KV_HALF = N_DGROUPS * ATT_KV_HEADS * ATT_HEAD_DIM
GROUP_KV = ATT_KV_HEADS * ATT_HEAD_DIM
N_KEYS = 129
ROPE_THETA = 10000.0
PLE_DIM = 256
EPS = 1e-6
LOG2E = 1.4426950408889634
LN2 = 0.6931471805599453

LANES = 128
SUBLANES = 8
VMEM_LIMIT_BYTES = 56 * 1024 * 1024
ROW_TILE = 512
NEG_BIG = -1e30
RELAYOUT_SLOTS = 4
ATTN_Q_BLOCKS = 2
MAMBA_ROWS = 256
MAMBA_PROJ_PIECE = 256

HEAD_ORDER = (0, 4, 1, 5, 2, 6, 3, 7, 8, 12, 9, 13, 10, 14, 11, 15)


def _cparams(semantics):
    return pltpu.CompilerParams(dimension_semantics=semantics,
                                vmem_limit_bytes=VMEM_LIMIT_BYTES)


def _const_spec(shape):
    nd = len(shape)
    return pl.BlockSpec(shape, lambda *_: (0,) * nd, pipeline_mode=pl.Buffered(1))


def _split_bf16(x, parts):
    out = []
    rem = x
    for _ in range(parts):
        hi = rem.astype(BF16)
        out.append(hi)
        rem = rem - hi.astype(F32)
    return out


def _dot(a, b):
    return jnp.dot(a, b, preferred_element_type=F32)


def _dot_nt(a, b):
    return lax.dot_general(a, b, (((1,), (1,)), ((), ())), preferred_element_type=F32)


def _sigmoid(x):
    return 0.5 + 0.5 * jnp.tanh(0.5 * x)


def _silu(x):
    hx = 0.5 * x
    return hx + hx * jnp.tanh(hx)


def _lo_mask(rows):
    return lax.broadcasted_iota(jnp.int32, (rows, LANES), 1) < ATT_HEAD_DIM


def _rope_partner(v):
    lane = lax.broadcasted_iota(jnp.int32, v.shape, 1)
    return jnp.where((lane & 32) == 0, pltpu.roll(v, 96, 1), pltpu.roll(v, 32, 1))


def _head_norm_rope(y, ms, hw, cos, sin, scale):
    yn = y * lax.rsqrt(ms + EPS) * (hw * scale)
    return yn * cos + _rope_partner(yn) * sin


def _norm_proj_kernel(*refs, segs, use_rope, n_out, tm):
    h_ref, nw_ref, w_ref = refs[:3]
    pos = 3
    if use_rope:
        hw_ref, cos_ref, sin_ref, bd_ref = refs[3:7]
        pos = 7
    out_refs = refs[pos:pos + n_out]
    scr = refs[pos + n_out] if len(refs) > pos + n_out else None
    x = h_ref[...]
    ms = jnp.mean(x * x, axis=-1, keepdims=True)
    u = (x * lax.rsqrt(ms + EPS) * nw_ref[...]).astype(BF16)
    slot = 0
    for start, width, rope, scale, sinks in segs:
        acc = _dot(u, w_ref[:, start:start + width])
        if not rope and all(dil == 0 for _, _, dil in sinks):
            for oi, col_off, _ in sinks:
                out_refs[oi][:, col_off:col_off + width] = acc.astype(out_refs[oi].dtype)
            continue
        for cb in range(width // LANES):
            val = acc[:, cb * LANES:(cb + 1) * LANES]
            if rope:
                if cb % 2 == 0:
                    y2 = acc[:, cb * LANES:(cb + 2) * LANES]
                    ms2 = _dot((y2 * y2).astype(BF16), bd_ref[...])
                val = _head_norm_rope(val, ms2[:, (cb % 2) * LANES:(cb % 2 + 1) * LANES],
                                      hw_ref[...], cos_ref[...], sin_ref[...], scale)
            for oi, col_off, dil in sinks:
                o_ref = out_refs[oi]
                cs = slice(col_off + cb * LANES, col_off + (cb + 1) * LANES)
                if dil == 0:
                    o_ref[:, cs] = val.astype(o_ref.dtype)
                elif dil == 1:
                    o_ref[0, :, cs] = val.astype(o_ref.dtype)
                else:
                    s = slot % RELAYOUT_SLOTS
                    slot += 1
                    scr[s] = val
                    for r in range(dil):
                        o_ref[r, :, cs] = scr[s, pl.ds(r, tm // dil, stride=dil), :].astype(o_ref.dtype)


def _norm_proj(h, nw, w, segs, out_defs, rope_inputs=None, rope_rows=None, nbatch=None):
    m = h.shape[0]
    tm = min(ROW_TILE, m)
    n = w.shape[1]
    use_rope = rope_inputs is not None
    in_specs = [pl.BlockSpec((tm, D_MODEL), lambda i: (i, 0)),
                _const_spec((1, D_MODEL)),
                _const_spec((D_MODEL, n))]
    args = [h, nw, w]
    if use_rope:
        hw, cos, sin, bd = rope_inputs
        nblk = rope_rows // tm
        in_specs += [_const_spec((1, LANES)),
                     pl.BlockSpec((tm, LANES), lambda i: (i % nblk, 0)),
                     pl.BlockSpec((tm, LANES), lambda i: (i % nblk, 0)),
                     _const_spec((2 * LANES, 2 * LANES))]
        args += [hw, cos, sin, bd]
    out_shape, out_specs = [], []
    for width, dt, dil in out_defs:
        if dil == 0:
            out_shape.append(jax.ShapeDtypeStruct((m, width), dt))
            out_specs.append(pl.BlockSpec((tm, width), lambda i: (i, 0)))
        else:
            tiles = m // nbatch // tm
            out_shape.append(jax.ShapeDtypeStruct((nbatch, dil, m // nbatch // dil, width), dt))
            out_specs.append(pl.BlockSpec((None, dil, tm // dil, width),
                                          lambda i, tiles=tiles: (i // tiles, 0, i % tiles, 0)))
    scratch = []
    if any(dil > 1 for _, _, dil in out_defs):
        scratch.append(pltpu.VMEM((RELAYOUT_SLOTS, tm, LANES), F32))
    return pl.pallas_call(
        functools.partial(_norm_proj_kernel, segs=segs, use_rope=use_rope, n_out=len(out_defs), tm=tm),
        grid=(m // tm,),
        in_specs=in_specs,
        out_specs=tuple(out_specs),
        out_shape=tuple(out_shape),
        scratch_shapes=scratch,
        compiler_params=_cparams(("parallel",)),
        name="norm_proj",
    )(*args)


def _ssd_kernel(z_ref, xbc_ref, dt_ref, cw_ref, cb_ref, dtb_ref, alog_ref, dexp_ref, nw_ref,
                tril_ref, e_ref, h0_ref, c0_ref, y_ref, hout_ref, ht_scr, xpad_scr,
                *, rows_in, nchunks):
    q = SSD_CHUNK
    c = pl.program_id(1)

    @pl.when(c == 0)
    def _():
        ht_scr[...] = jnp.transpose(h0_ref[...])
        xpad_scr[...] = c0_ref[...]

    def pad_rows(v):
        if rows_in == q:
            return v
        return jnp.concatenate([v, jnp.zeros((q - rows_in, v.shape[1]), v.dtype)], axis=0)

    y = _ssd_chunk(pad_rows(z_ref[...]), pad_rows(xbc_ref[...]), pad_rows(dt_ref[...]),
                   cw_ref, cb_ref, dtb_ref, alog_ref, dexp_ref, nw_ref, tril_ref, e_ref,
                   ht_scr, xpad_scr, rows_in)
    y_ref[...] = y[:rows_in].astype(y_ref.dtype)

    @pl.when(c == nchunks - 1)
    def _():
        hout_ref[...] = jnp.transpose(ht_scr[...])


def _ssd_chunk(zz, xbc, dt_raw, cw_ref, cb_ref, dtb_ref, alog_ref, dexp_ref, nw_ref, tril_ref, e_ref,
               ht_scr, xpad_scr, rows_valid, side=None):
    q = SSD_CHUNK
    gw = D_INNER // SSM_GROUPS

    def run_side(n=1):
        for _ in range(n):
            if side:
                side.pop(0)()

    tail = xpad_scr[...]
    xbc3 = xbc.reshape(q // SUBLANES, SUBLANES, CONV_DIM)
    row8 = lax.broadcasted_iota(jnp.int32, (1, SUBLANES, CONV_DIM), 1)
    conv = cb_ref[...] + cw_ref[CONV_K - 1:CONV_K, :] * xbc
    for k in range(CONV_K - 1):
        sh = CONV_K - 1 - k
        rot3 = pltpu.roll(xbc3, sh, 1)
        prev3 = jnp.concatenate([pltpu.roll(tail, sh, 0)[None], rot3[:-1]], axis=0)
        shifted = jnp.where(row8 < sh, prev3, rot3).reshape(q, CONV_DIM)
        conv = conv + cw_ref[k:k + 1, :] * shifted
    xpad_scr[...] = xbc[q - SUBLANES:q]
    run_side()
    act = _silu(conv)
    xs = act[:, :D_INNER]
    bm = act[:, D_INNER:D_INNER + SSM_GROUPS * D_STATE]
    cm = act[:, D_INNER + SSM_GROUPS * D_STATE:]

    dtr = dt_raw + dtb_ref[...]
    dt = jnp.maximum(dtr, 0.0) + jnp.log1p(jnp.exp(-jnp.abs(dtr)))
    if rows_valid < q:
        row = lax.broadcasted_iota(jnp.int32, dt.shape, 0)
        dt = jnp.where(row < rows_valid, dt, 0.0)
    a = dt * (-LOG2E * jnp.exp(alog_ref[...]))
    tril = tril_ref[...]
    acum = sum(_dot(tril, part) for part in _split_bf16(a, 3))
    acum_t = jnp.transpose(acum)

    e = e_ref[...]
    grp = lax.broadcasted_iota(jnp.int32, (q, LANES), 1) >> 5
    zero_l = jnp.zeros((q, LANES), BF16)

    def packed(parts):
        out = zero_l
        for idx, part in enumerate(parts):
            out = jnp.where(grp == idx, part, out)
        return out

    dt_e = _dot(packed(_split_bf16(dt, 2)), e)
    acum_e = _dot(packed(_split_bf16(acum, 3)), e)
    alast_e = acum_e[q - 1:q, :]
    exp_acum_e = jnp.exp2(acum_e)
    decay_end_e = jnp.exp2(alast_e - acum_e)
    chunk_decay_e = jnp.exp2(alast_e)

    xdt = xs * dt_e
    xdt_bf = xdt.astype(BF16)
    xdtw_bf = (xdt * decay_end_e).astype(BF16)

    li = lax.broadcasted_iota(jnp.int32, (q, q), 0)
    si = lax.broadcasted_iota(jnp.int32, (q, q), 1)
    causal = li >= si
    lo = _lo_mask(q)
    zero_bf = jnp.zeros((q, LANES), BF16)

    y_parts = []
    for g in range(SSM_GROUPS):
        gs = slice(g * gw, (g + 1) * gw)
        bg = bm[:, g * D_STATE:(g + 1) * D_STATE]
        cg_bf = cm[:, g * D_STATE:(g + 1) * D_STATE].astype(BF16)
        cb = jnp.where(causal, _dot_nt(cg_bf, bg.astype(BF16)), 0.0)
        htg = ht_scr[:, gs]
        y_off = _dot(cg_bf, htg.astype(BF16)) * exp_acum_e[:, gs]
        blocks = []
        for j in range(gw // LANES):
            hd = g * (SSM_HEADS // SSM_GROUPS) + 2 * j
            xpair = xdt_bf[:, hd * SSM_HEADDIM:hd * SSM_HEADDIM + LANES]
            mats = []
            for hh in (hd, hd + 1):
                seg = acum[:, hh:hh + 1] - acum_t[hh:hh + 1, :]
                mats.append((cb * jnp.exp2(jnp.minimum(seg, 0.0))).astype(BF16))
            xstack = jnp.concatenate([jnp.where(lo, xpair, zero_bf), jnp.where(lo, zero_bf, xpair)], axis=0)
            blocks.append(_dot(jnp.concatenate(mats, axis=1), xstack))
            run_side()
        y_diag = jnp.concatenate(blocks, axis=1)
        bgt_bf = jnp.transpose(bg).astype(BF16)
        st = _dot(bgt_bf, xdtw_bf[:, gs])
        ht_scr[:, gs] = htg * chunk_decay_e[:, gs] + st
        y_parts.append(y_diag + y_off)

    y = jnp.concatenate(y_parts, axis=1) + xs * dexp_ref[...]
    y = y * _silu(zz)
    run_side()
    normed = []
    for g in range(SSM_GROUPS):
        yg = y[:, g * gw:(g + 1) * gw]
        ms = jnp.mean(yg * yg, axis=-1, keepdims=True)
        normed.append(yg * lax.rsqrt(ms + EPS))
    return jnp.concatenate(normed, axis=1) * nw_ref[...]


def _ssd(z, xbc, dtp, lw, consts, h0, c0, layer, *, nbatch, rows_in, nchunks, y_dtype):
    m = z.shape[0]
    row_map = lambda b, c: (b * nchunks + c, 0)
    batch_map = lambda b, c: (b, 0, 0)
    state_map = lambda b, c: (layer, b, 0, 0)
    in_specs = [
        pl.BlockSpec((rows_in, D_INNER), row_map),
        pl.BlockSpec((rows_in, CONV_DIM), row_map),
        pl.BlockSpec((rows_in, LANES), row_map),
        _const_spec((CONV_K, CONV_DIM)),
        _const_spec((1, CONV_DIM)),
        _const_spec((1, LANES)),
        _const_spec((1, LANES)),
        _const_spec((1, D_INNER)),
        _const_spec((1, D_INNER)),
        _const_spec((SSD_CHUNK, SSD_CHUNK)),
        _const_spec((LANES, D_INNER)),
        pl.BlockSpec((None, None, D_INNER, D_STATE), state_map),
        pl.BlockSpec((None, None, SUBLANES, CONV_DIM), state_map),
    ]
    out_specs = (pl.BlockSpec((rows_in, D_INNER), row_map),
                 pl.BlockSpec((None, D_INNER, D_STATE), batch_map))
    out_shape = (jax.ShapeDtypeStruct((m, D_INNER), y_dtype),
                 jax.ShapeDtypeStruct((nbatch, D_INNER, D_STATE), F32))
    return pl.pallas_call(
        functools.partial(_ssd_kernel, rows_in=rows_in, nchunks=nchunks),
        grid=(nbatch, nchunks),
        in_specs=in_specs,
        out_specs=out_specs,
        out_shape=out_shape,
        scratch_shapes=[pltpu.VMEM((D_STATE, D_INNER), F32),
                        pltpu.VMEM((SUBLANES, CONV_DIM), F32)],
        compiler_params=_cparams(("parallel", "arbitrary")),
        name="ssd",
    )(z, xbc, dtp, lw["conv_w"], lw["conv_b"], lw["dt_bias"], lw["a_log"], lw["d_exp"],
      lw["norm_w"], consts["tril"], consts["expand"], h0, c0)


def _mamba_layer_kernel(ha_ref, hc_ref, p_ref, nw_ref, win_ref, wdt_ref, cw_ref, cb_ref, dtb_ref, alog_ref,
                        dexp_ref, mnw_ref, tril_ref, e_ref, wout_ref, pnw_ref, gw_ref, pw_ref,
                        out_ref, sout_ref, cout_ref,
                        proj0, proj1, y0, y1, ht_scr, xpad_scr, *, nsteps, rows):
    s = pl.program_id(1)
    q = SSD_CHUNK

    @pl.when(s == 0)
    def _():
        proj1[...] = jnp.zeros(proj1.shape, proj1.dtype)
        y0[...] = jnp.zeros(y0.shape, y0.dtype)
        y1[...] = jnp.zeros(y1.shape, y1.dtype)

    @pl.when(s <= 1)
    def _():
        ht_scr[...] = jnp.zeros(ht_scr.shape, ht_scr.dtype)
        xpad_scr[...] = jnp.zeros(xpad_scr.shape, xpad_scr.dtype)

    def stages(proj_w, proj_r, y_w, y_r):
        vals = {}

        def a_norm():
            x = ha_ref[...]
            ms = jnp.mean(x * x, axis=-1, keepdims=True)
            vals["u"] = (x * lax.rsqrt(ms + EPS) * nw_ref[...]).astype(BF16)

        def a_piece(c0, c1):
            def emit():
                proj_w[:, c0:c1] = _dot(vals["u"], win_ref[:, c0:c1])
            return emit

        def c_out(c0, c1):
            def emit():
                vals["h1", c0] = hc_ref[:, c0:c1] + _dot(y_r[...], wout_ref[:, c0:c1])
            return emit

        def c_norm():
            h1 = jnp.concatenate([vals["h1", c0] for c0 in c_cols], axis=1)
            vals["h1"] = h1
            ms = jnp.mean(h1 * h1, axis=-1, keepdims=True)
            vals["un"] = (h1 * lax.rsqrt(ms + EPS) * pnw_ref[...]).astype(BF16)

        def c_gate(c0, c1):
            def emit():
                gate = _sigmoid(_dot(vals["un"], gw_ref[:, c0:c1]))
                pe = _dot(p_ref[...].astype(BF16), pw_ref[:, c0:c1])
                out_ref[:, c0:c1] = vals["h1"][:, c0:c1] + gate * pe
            return emit

        def a_dt():
            proj_w[:, n_main:] = _dot(vals["u"], wdt_ref[...])

        n_main = D_INNER + CONV_DIM
        step = MAMBA_PROJ_PIECE
        a_work = [a_piece(c0, c0 + step) for c0 in range(0, n_main, step)] + [a_dt]
        c_cols = list(range(0, D_MODEL, step))
        c_work = ([c_out(c0, c0 + step) for c0 in c_cols] + [c_norm]
                  + [c_gate(c0, c0 + step) for c0 in c_cols])
        side = [a_norm]
        while a_work or c_work:
            if c_work:
                side.append(c_work.pop(0))
            if a_work:
                side.append(a_work.pop(0))
            if a_work:
                side.append(a_work.pop(0))
        for j in range(rows // q):
            rs = slice(j * q, (j + 1) * q)
            y = _ssd_chunk(proj_r[rs, 0:D_INNER], proj_r[rs, D_INNER:D_INNER + CONV_DIM],
                           proj_r[rs, D_INNER + CONV_DIM:], cw_ref, cb_ref, dtb_ref, alog_ref,
                           dexp_ref, mnw_ref, tril_ref, e_ref, ht_scr, xpad_scr, q, side=side)
            y_w[rs, :] = y.astype(y_w.dtype)
        while side:
            side.pop(0)()

    @pl.when(s % 2 == 0)
    def _():
        stages(proj0, proj1, y1, y0)

    @pl.when(s % 2 == 1)
    def _():
        stages(proj1, proj0, y0, y1)

    @pl.when(s == nsteps)
    def _():
        sout_ref[...] = jnp.transpose(ht_scr[...])
        cout_ref[...] = xpad_scr[...]


def _mamba_layer(h, p_all, layer, nw, lw, consts, pnw, gw, pw, *, nbatch, seq):
    m = h.shape[0]
    rows = MAMBA_ROWS
    nsteps = seq // rows
    n_proj = D_INNER + CONV_DIM + LANES
    a_map = lambda b, s: (b * nsteps + jnp.minimum(s, nsteps - 1), 0)
    c_map = lambda b, s: (b * nsteps + jnp.clip(s - 2, 0, nsteps - 1), 0)
    in_specs = [
        pl.BlockSpec((rows, D_MODEL), a_map),
        pl.BlockSpec((rows, D_MODEL), c_map),
        pl.BlockSpec((None, rows, PLE_DIM), lambda b, s: (layer, b * nsteps + jnp.clip(s - 2, 0, nsteps - 1), 0)),
        _const_spec((1, D_MODEL)),
        _const_spec(lw["in_w"].shape),
        _const_spec((D_MODEL, LANES)),
        _const_spec((CONV_K, CONV_DIM)),
        _const_spec((1, CONV_DIM)),
        _const_spec((1, LANES)),
        _const_spec((1, LANES)),
        _const_spec((1, D_INNER)),
        _const_spec((1, D_INNER)),
        _const_spec((SSD_CHUNK, SSD_CHUNK)),
        _const_spec((LANES, D_INNER)),
        _const_spec((D_INNER, D_MODEL)),
        _const_spec((1, D_MODEL)),
        _const_spec((D_MODEL, D_MODEL)),
        _const_spec((PLE_DIM, D_MODEL)),
    ]
    out_specs = (pl.BlockSpec((rows, D_MODEL), c_map),
                 pl.BlockSpec((None, D_INNER, D_STATE), lambda b, s: (b, 0, 0)),
                 pl.BlockSpec((None, SUBLANES, CONV_DIM), lambda b, s: (b, 0, 0)))
    out_shape = (jax.ShapeDtypeStruct((m, D_MODEL), F32),
                 jax.ShapeDtypeStruct((nbatch, D_INNER, D_STATE), F32),
                 jax.ShapeDtypeStruct((nbatch, SUBLANES, CONV_DIM), F32))
    return pl.pallas_call(
        functools.partial(_mamba_layer_kernel, nsteps=nsteps, rows=rows),
        grid=(nbatch, nsteps + 2),
        in_specs=in_specs,
        out_specs=out_specs,
        out_shape=out_shape,
        scratch_shapes=[pltpu.VMEM((rows, n_proj), F32), pltpu.VMEM((rows, n_proj), F32),
                        pltpu.VMEM((rows, D_INNER), BF16), pltpu.VMEM((rows, D_INNER), BF16),
                        pltpu.VMEM((D_STATE, D_INNER), F32),
                        pltpu.VMEM((SUBLANES, CONV_DIM), F32)],
        compiler_params=_cparams(("parallel", "arbitrary")),
        name="mamba_layer",
    )(h, h, p_all, nw, lw["in_w"], lw["dt_w"], lw["conv_w"], lw["conv_b"], lw["dt_bias"], lw["a_log"],
      lw["d_exp"], lw["norm_w"], consts["tril"], consts["expand"], lw["out_w"], pnw, gw, pw)


def _attn_prompt_kernel(q_ref, kcur_ref, kprev_ref, vcur_ref, vprev_ref, o_ref, l_ref, *, nq):
    i = pl.program_id(2)
    t = SSD_CHUNK
    npair = GROUP_KV // ATT_HEAD_DIM
    qi = lax.broadcasted_iota(jnp.int32, (t, 2 * t), 0)
    kk = lax.broadcasted_iota(jnp.int32, (t, 2 * t), 1)
    band = (kk >= qi) & (kk <= qi + (N_KEYS - 1))
    bias_inner = jnp.where(band, 0.0, NEG_BIG)
    bias_first = jnp.where(band & ((kk >= t) | (i > 0)), 0.0, NEG_BIG)
    lo = _lo_mask(t)
    lo2 = _lo_mask(2 * t)
    zero_v = jnp.zeros((2 * t, LANES), BF16)
    ones_stack = jnp.concatenate([jnp.where(lo2, 1.0, 0.0), jnp.where(lo2, 0.0, 1.0)],
                                 axis=0).astype(BF16)
    lane = lax.broadcasted_iota(jnp.int32, (t, LANES), 1)
    lane_slot = (lane & (ATT_HEAD_DIM - 1)) >> 3
    for s in range(nq):
        rows = slice(s * t, (s + 1) * t)
        bias = bias_first if s == 0 else bias_inner
        l_c = jnp.zeros((t, LANES), F32)
        for kc in range(GROUP_KV // LANES):
            ks = slice(kc * LANES, (kc + 1) * LANES)
            if s == 0:
                k2 = jnp.concatenate([kprev_ref[:, ks], kcur_ref[0:t, ks]], axis=0)
                v2 = jnp.concatenate([vprev_ref[:, ks], vcur_ref[0:t, ks]], axis=0)
            else:
                k2 = kcur_ref[(s - 1) * t:(s + 1) * t, ks]
                v2 = vcur_ref[(s - 1) * t:(s + 1) * t, ks]
            k_stack = jnp.concatenate([jnp.where(lo2, k2, zero_v), jnp.where(lo2, zero_v, k2)], axis=0)
            v_stack = jnp.concatenate([jnp.where(lo2, v2, zero_v), jnp.where(lo2, zero_v, v2)], axis=0)
            rhs = jnp.concatenate([v_stack, ones_stack], axis=1)
            lhs = jnp.concatenate(
                [q_ref[rows, (kc * npair + pb) * LANES:(kc * npair + pb + 1) * LANES] for pb in range(npair)],
                axis=0)
            sc = _dot_nt(lhs, k_stack).reshape(npair, t, 4 * t)
            sc_a = sc[:, :, :2 * t] + bias[None]
            sc_b = sc[:, :, 2 * t:] + bias[None]
            mx_a = jnp.max(sc_a, axis=-1, keepdims=True)
            mx_b = jnp.max(sc_b, axis=-1, keepdims=True)
            p_cat = jnp.concatenate([jnp.exp2(sc_a - mx_a), jnp.exp2(sc_b - mx_b)], axis=-1)
            res = _dot(p_cat.astype(BF16).reshape(npair * t, 4 * t), rhs)
            for pb in range(npair):
                blk = kc * npair + pb
                num = res[pb * t:(pb + 1) * t, :LANES]
                den = res[pb * t:(pb + 1) * t, LANES:]
                o_ref[rows, blk * LANES:(blk + 1) * LANES] = (num / den).astype(o_ref.dtype)
                lse = jnp.where(lo, mx_a[pb], mx_b[pb]) * LN2 + jnp.log(den)
                l_c = jnp.where(lane_slot == blk, lse, l_c)
        l_ref[rows, :] = l_c


def _attn_prompt_group(q, k, v, gi):
    nbatch, dil, rows, _ = q.shape
    t = SSD_CHUNK
    nq = ATTN_Q_BLOCKS
    cur = lambda b, r, i: (b, r, i, 0)
    prev = lambda b, r, i: (b, r, jnp.maximum(nq * i - 1, 0), 0)
    return pl.pallas_call(
        functools.partial(_attn_prompt_kernel, nq=nq),
        grid=(nbatch, dil, rows // (nq * t)),
        in_specs=[pl.BlockSpec((None, None, nq * t, ATT_WIDTH), cur),
                  pl.BlockSpec((None, None, nq * t, GROUP_KV), cur),
                  pl.BlockSpec((None, None, t, GROUP_KV), prev),
                  pl.BlockSpec((None, None, nq * t, GROUP_KV), cur),
                  pl.BlockSpec((None, None, t, GROUP_KV), prev)],
        out_specs=(pl.BlockSpec((None, None, nq * t, ATT_WIDTH), cur),
                   pl.BlockSpec((None, None, nq * t, LANES), cur)),
        out_shape=(jax.ShapeDtypeStruct((nbatch, dil, rows, ATT_WIDTH), BF16),
                   jax.ShapeDtypeStruct((nbatch, dil, rows, LANES), F32)),
        compiler_params=_cparams(("parallel", "parallel", "arbitrary")),
        name=f"attn_prompt_g{gi}",
    )(q, k, k, v, v)


def _attn_sample_kernel(q_ref, kn_ref, vn_ref, c1_ref, c2_ref, c3_ref, o_ref, *, dec_seq):
    caches = (c1_ref, c2_ref, c3_ref)
    nrow = 8 * dec_seq
    lo_row = _lo_mask(dec_seq)
    pad = LANES - dec_seq
    outs = [[None] * N_DGROUPS for _ in range(ATT_WIDTH // LANES)]
    lses = [[None] * N_DGROUPS for _ in range(ATT_WIDTH // LANES)]
    for gi, (_, dil) in enumerate(DILATION_GROUPS):
        cref = caches[gi]
        w = cref.shape[1]
        tq = lax.broadcasted_iota(jnp.int32, (nrow, w), 0) & (dec_seq - 1)
        rho = lax.broadcasted_iota(jnp.int32, (nrow, w), 1)
        delta = w + tq - rho
        valid_c = (rho >= tq) & ((delta & (dil - 1)) == 0) & (delta <= (N_KEYS - 1) * dil)
        bias_c = jnp.where(valid_c, 0.0, NEG_BIG)
        tq_n = lax.broadcasted_iota(jnp.int32, (nrow, LANES), 0) & (dec_seq - 1)
        tn = lax.broadcasted_iota(jnp.int32, (nrow, LANES), 1)
        valid_n = (tn <= tq_n) & (((tq_n - tn) & (dil - 1)) == 0)
        bias_n = jnp.where(valid_n, 0.0, NEG_BIG)
        gsl = slice(gi * GROUP_KV, (gi + 1) * GROUP_KV)
        k_new = jnp.concatenate([kn_ref[:, gsl], jnp.zeros((pad, GROUP_KV), F32)], axis=0)
        v_new = jnp.concatenate([vn_ref[:, gsl], jnp.zeros((pad, GROUP_KV), F32)], axis=0)
        for kc in range(GROUP_KV // LANES):
            ks = slice(kc * LANES, (kc + 1) * LANES)
            rows = []
            for pb in range(4):
                blk = kc * 4 + pb
                q2 = q_ref[:, gi * ATT_WIDTH + blk * LANES:gi * ATT_WIDTH + (blk + 1) * LANES]
                rows.append(jnp.where(lo_row, q2, 0.0))
                rows.append(jnp.where(lo_row, 0.0, q2))
            lhs = jnp.concatenate(rows, axis=0).astype(BF16)
            kcache_t = cref[ks, :].astype(BF16)
            vcache_t = cref[GROUP_KV + kc * LANES:GROUP_KV + (kc + 1) * LANES, :].astype(BF16)
            s_c = _dot(lhs, kcache_t) + bias_c
            s_n = _dot_nt(lhs, k_new[:, ks].astype(BF16)) + bias_n
            mx = jnp.maximum(jnp.max(s_c, axis=-1, keepdims=True),
                             jnp.max(s_n, axis=-1, keepdims=True))
            p_c = jnp.exp(s_c - mx)
            p_n = jnp.exp(s_n - mx)
            den = jnp.sum(p_c, axis=-1, keepdims=True) + jnp.sum(p_n, axis=-1, keepdims=True)
            num = _dot_nt(p_c.astype(BF16), vcache_t) + _dot(p_n.astype(BF16), v_new[:, ks].astype(BF16))
            on = num / den
            lse = mx + jnp.log(den)
            for pb in range(4):
                blk = kc * 4 + pb
                r0 = pb * 2 * dec_seq
                r1 = r0 + dec_seq
                outs[blk][gi] = jnp.where(lo_row, on[r0:r1], on[r1:r1 + dec_seq])
                lses[blk][gi] = jnp.where(lo_row, lse[r0:r1], lse[r1:r1 + dec_seq])
    for blk in range(ATT_WIDTH // LANES):
        ls = lses[blk]
        mx = jnp.maximum(jnp.maximum(ls[0], ls[1]), ls[2])
        ws = [jnp.exp(l - mx) for l in ls]
        tot = ws[0] + ws[1] + ws[2]
        o = (outs[blk][0] * ws[0] + outs[blk][1] * ws[1] + outs[blk][2] * ws[2]) / tot
        o_ref[:, blk * LANES:(blk + 1) * LANES] = o


def _attn_sample(q, k, v, caches, nbatch, dec_seq):
    m = q.shape[0]
    row_map = lambda b: (b, 0)
    in_specs = [pl.BlockSpec((dec_seq, N_DGROUPS * ATT_WIDTH), row_map),
                pl.BlockSpec((dec_seq, KV_HALF), row_map),
                pl.BlockSpec((dec_seq, KV_HALF), row_map)]
    for cch in caches:
        in_specs.append(pl.BlockSpec((None, 2 * GROUP_KV, cch.shape[2]), lambda b: (b, 0, 0)))
    return pl.pallas_call(
        functools.partial(_attn_sample_kernel, dec_seq=dec_seq),
        grid=(nbatch,),
        in_specs=in_specs,
        out_specs=pl.BlockSpec((dec_seq, ATT_WIDTH), row_map),
        out_shape=jax.ShapeDtypeStruct((m, ATT_WIDTH), F32),
        compiler_params=_cparams(("parallel",)),
        name="attn_sample",
    )(q, k, v, *caches)


def _token_order(ref, scr, slot, cb, dil, tm):
    cs = slice(cb * LANES, (cb + 1) * LANES)
    if dil == 1:
        return ref[0, :, cs].astype(F32)
    for r in range(dil):
        scr[slot, pl.ds(r, tm // dil, stride=dil), :] = ref[r, :, cs].astype(F32)
    return scr[slot]


def _out_ple_kernel(*refs, mode, tm):
    if mode == "mamba":
        (y_ref,) = refs[:1]
        pos = 1
        mix = y_ref[...].astype(BF16)
    elif mode == "attn_merge":
        o_refs = refs[0:3]
        l_refs = refs[3:6]
        gate_ref, x_ref = refs[6:8]
        pos = 8
        scr, mix_scr = refs[-2:]
        refs = refs[:-2]
        dils = [d for _, d in DILATION_GROUPS]
        ls = [_token_order(l_refs[g], scr, g, 0, dils[g], tm) for g in range(N_DGROUPS)]
        mx = jnp.maximum(jnp.maximum(ls[0], ls[1]), ls[2])
        es = [jnp.exp(l - mx) for l in ls]
        tot = es[0] + es[1] + es[2]
        sub = lax.broadcasted_iota(jnp.int32, (tm, LANES), 1) & 7
        packed = jnp.zeros((tm, LANES), BF16)
        for g in range(N_DGROUPS):
            for idx, part in enumerate(_split_bf16(es[g] / tot, 2)):
                packed = jnp.where(sub == 2 * g + idx, part, packed)
        ws_all = _dot(packed, x_ref[...])
        ws = [ws_all[:, g * ATT_WIDTH:(g + 1) * ATT_WIDTH] for g in range(N_DGROUPS)]
        for cb in range(ATT_WIDTH // LANES):
            cs = slice(cb * LANES, (cb + 1) * LANES)
            o = None
            for g in range(N_DGROUPS):
                og = _token_order(o_refs[g], scr, N_DGROUPS + (cb * N_DGROUPS + g) % RELAYOUT_SLOTS,
                                  cb, dils[g], tm)
                term = og * ws[g][:, cs]
                o = term if o is None else o + term
            gate = gate_ref[:, cs]
            mix_scr[:, cs] = (o * _silu(gate)).astype(BF16)
        mix = mix_scr[...]
    else:
        o_ref_in, gate_ref = refs[:2]
        pos = 2
        gate = gate_ref[...]
        mix = (o_ref_in[...] * _silu(gate)).astype(BF16)
    h_ref, p_ref, wout_ref, pnw_ref, gw_ref, pw_ref, out_ref = refs[pos:]
    out_ref[...] = _residual_ple(mix, h_ref[...], p_ref[...], wout_ref, pnw_ref, gw_ref, pw_ref)


def _residual_ple(mix, h, p, wout_ref, pnw_ref, gw_ref, pw_ref):
    h1 = h + _dot(mix, wout_ref[...])
    ms = jnp.mean(h1 * h1, axis=-1, keepdims=True)
    un = (h1 * lax.rsqrt(ms + EPS) * pnw_ref[...]).astype(BF16)
    gate_p = _sigmoid(_dot(un, gw_ref[...]))
    pe = _dot(p.astype(BF16), pw_ref[...])
    return h1 + gate_p * pe


def _out_ple(mix_inputs, mode, h, p_all, layer, wout, pnw, gw, pw):
    m = h.shape[0]
    tm = min(ROW_TILE, m)
    row_map = lambda i: (i, 0)
    in_specs = []
    for a in mix_inputs:
        if a.ndim == 4:
            _, dil, rows, width = a.shape
            tiles = rows * dil // tm
            in_specs.append(pl.BlockSpec((None, dil, tm // dil, width),
                                         lambda i, tiles=tiles: (i // tiles, 0, i % tiles, 0)))
        elif a.shape[0] == m:
            in_specs.append(pl.BlockSpec((tm, a.shape[1]), row_map))
        else:
            in_specs.append(_const_spec(a.shape))
    scratch = []
    if mode == "attn_merge":
        scratch = [pltpu.VMEM((N_DGROUPS + RELAYOUT_SLOTS, tm, LANES), F32),
                   pltpu.VMEM((tm, ATT_WIDTH), BF16)]
    in_specs += [pl.BlockSpec((tm, D_MODEL), row_map),
                 pl.BlockSpec((None, tm, PLE_DIM), lambda i: (layer, i, 0)),
                 _const_spec(wout.shape),
                 _const_spec((1, D_MODEL)),
                 _const_spec((D_MODEL, D_MODEL)),
                 _const_spec((PLE_DIM, D_MODEL))]
    return pl.pallas_call(
        functools.partial(_out_ple_kernel, mode=mode, tm=tm),
        grid=(m // tm,),
        in_specs=in_specs,
        out_specs=pl.BlockSpec((tm, D_MODEL), row_map),
        out_shape=jax.ShapeDtypeStruct((m, D_MODEL), F32),
        scratch_shapes=scratch,
        compiler_params=_cparams(("parallel",)),
        name=f"out_ple_{mode}",
    )(*mix_inputs, h, p_all, wout, pnw, gw, pw)


def _rope_tables(pos):
    half = ATT_HEAD_DIM // 2
    inv = 1.0 / (ROPE_THETA ** (jnp.arange(half, dtype=F32) / half))
    ang = pos.astype(F32)[:, None] * inv[None, :]
    cos, sin = jnp.cos(ang), jnp.sin(ang)
    cos128 = jnp.concatenate([cos, cos, cos, cos], axis=1)
    sin128 = jnp.concatenate([-sin, sin, -sin, sin], axis=1)
    return cos128, sin128


def _permute_heads(w, axis):
    shape = w.shape
    assert tuple(8 * c + 4 * hf + j for c in range(2) for j in range(4) for hf in range(2)) == HEAD_ORDER
    w = w.reshape(shape[:axis] + (2, 2, ATT_HEADS // 4, ATT_HEAD_DIM) + shape[axis + 1:])
    return jnp.swapaxes(w, axis + 1, axis + 2).reshape(shape)


def _prep_weights(norm_w, m_in_w, m_conv_w, m_conv_b, m_dt_bias, m_A_log, m_D, m_norm_w, m_out_w,
                  kv_norm_w, kv_w, k_norm_w, a_in_w, a_q_norm_w, a_out_w, ple_w, ple_gate_w,
                  ple_norm_w):
    row = lambda v: v.reshape(1, -1).astype(F32)
    reps = LANES // SSM_HEADS
    lane_pad = lambda v: jnp.tile(v.astype(F32), reps).reshape(1, LANES)
    mamba = []
    for i in range(N_A_LAYERS):
        mamba.append(dict(
            in_w=m_in_w[i].astype(BF16),
            dt_w=jnp.tile(m_in_w[i][:, D_INNER + CONV_DIM:], (1, reps)).astype(BF16),
            conv_w=m_conv_w[i].astype(F32),
            conv_b=row(m_conv_b[i]),
            dt_bias=lane_pad(m_dt_bias[i]),
            a_log=lane_pad(m_A_log[i]),
            d_exp=row(jnp.repeat(m_D[i], SSM_HEADDIM)),
            norm_w=row(m_norm_w[i]),
            out_w=m_out_w[i].astype(BF16),
        ))
    attn = []
    for j in range(DEPTH - N_A_LAYERS):
        w = a_in_w[j].reshape(D_MODEL, N_DGROUPS + 1, ATT_WIDTH)
        w = _permute_heads(w, 2).reshape(D_MODEL, (N_DGROUPS + 1) * ATT_WIDTH)
        attn.append(dict(
            in_w=w.astype(BF16),
            q_norm_w=row(jnp.tile(a_q_norm_w[j], LANES // ATT_HEAD_DIM)),
            out_w=_permute_heads(a_out_w[j], 0).astype(BF16),
        ))
    return dict(
        norm_w=[row(norm_w[i]) for i in range(DEPTH)],
        mamba=mamba,
        attn=attn,
        kv_norm_w=row(kv_norm_w),
        kv_w=kv_w.astype(BF16),
        k_norm_w=row(jnp.tile(k_norm_w, LANES // ATT_HEAD_DIM)),
        ple_w=[ple_w[i].astype(BF16) for i in range(DEPTH)],
        ple_gate_w=[ple_gate_w[i].astype(BF16) for i in range(DEPTH)],
        ple_norm_w=[row(ple_norm_w[i]) for i in range(DEPTH)],
    )


def _constants():
    t = SSD_CHUNK
    tril = jnp.tril(jnp.ones((t, t), F32)).astype(BF16)
    head = jnp.arange(LANES, dtype=jnp.int32)[:, None]
    col = jnp.arange(D_INNER, dtype=jnp.int32)[None, :]
    expand = (head % SSM_HEADS == col // SSM_HEADDIM).astype(BF16)
    li = jnp.arange(LANES, dtype=jnp.int32)
    l2 = jnp.arange(2 * LANES, dtype=jnp.int32)
    bd = ((l2[:, None] // ATT_HEAD_DIM) == (l2[None, :] // ATT_HEAD_DIM)).astype(F32) / ATT_HEAD_DIM
    ccol = jnp.arange(N_DGROUPS * ATT_WIDTH, dtype=jnp.int32)[None, :]
    cgrp, cc = ccol // ATT_WIDTH, ccol % ATT_WIDTH
    slot = cc // LANES + jnp.where(cc % LANES < ATT_HEAD_DIM, 0, 8)
    lrow = li[:, None]
    lse_spread = ((lrow >> 3 == slot) & ((lrow & 7) >> 1 == cgrp)).astype(BF16)
    return dict(tril=tril, expand=expand, bd=bd.astype(BF16), lse_spread=lse_spread)


_Q_SCALE = ATT_HEAD_DIM ** -0.5
_DILS = tuple(d for _, d in DILATION_GROUPS)
MAMBA_SEGS = ((0, D_INNER, False, 1.0, ((0, 0, 0),)),
              (D_INNER, CONV_DIM, False, 1.0, ((1, 0, 0),)))
MAMBA_OUTS = ((D_INNER, F32, 0), (CONV_DIM, F32, 0))
MAMBA_DT_SEGS = ((0, LANES, False, 1.0, ((0, 0, 0),)),)
MAMBA_DT_OUTS = ((LANES, F32, 0),)
KV_SEGS_TOK = ((0, KV_HALF, True, 1.0, ((0, 0, 0),)), (KV_HALF, KV_HALF, False, 1.0, ((1, 0, 0),)))
KV_OUTS_TOK = ((KV_HALF, F32, 0), (KV_HALF, F32, 0))
ATTN_SEGS_TOK = ((0, N_DGROUPS * ATT_WIDTH, True, _Q_SCALE, ((0, 0, 0),)),
                 (N_DGROUPS * ATT_WIDTH, ATT_WIDTH, False, 1.0, ((1, 0, 0),)))
ATTN_OUTS_TOK = ((N_DGROUPS * ATT_WIDTH, F32, 0), (ATT_WIDTH, F32, 0))
KV_SEGS_RES = tuple(
    (g * GROUP_KV, GROUP_KV, True, 1.0, ((0, g * GROUP_KV, 0), (2 + g, 0, _DILS[g])))
    for g in range(N_DGROUPS)) + tuple(
    (KV_HALF + g * GROUP_KV, GROUP_KV, False, 1.0, ((1, g * GROUP_KV, 0), (2 + N_DGROUPS + g, 0, _DILS[g])))
    for g in range(N_DGROUPS))
KV_OUTS_RES = KV_OUTS_TOK + tuple((GROUP_KV, BF16, d) for d in _DILS) * 2
ATTN_SEGS_RES = tuple((g * ATT_WIDTH, ATT_WIDTH, True, _Q_SCALE * LOG2E, ((g, 0, _DILS[g]),))
                      for g in range(N_DGROUPS)) + (
    (N_DGROUPS * ATT_WIDTH, ATT_WIDTH, False, 1.0, ((N_DGROUPS, 0, 0),)),)
ATTN_OUTS_RES = tuple((ATT_WIDTH, BF16, d) for d in _DILS) + ((ATT_WIDTH, F32, 0),)


def _trunk(x, p_all, wts, consts, rope_tabs, rope_rows, *, nbatch, seq, ssm0, conv0, caches):
    prompt = caches is None
    m = x.shape[0]
    h = x
    new_ssm, new_conv = [], []
    for i in range(N_A_LAYERS):
        lw = wts["mamba"][i]
        ple = (wts["ple_norm_w"][i], wts["ple_gate_w"][i], wts["ple_w"][i])
        if prompt:
            h, h_fin, xbc_tail = _mamba_layer(h, p_all, i, wts["norm_w"][i], lw, consts, *ple,
                                              nbatch=nbatch, seq=seq)
            new_conv.append(xbc_tail[:, SUBLANES - (CONV_K - 1):])
        else:
            z, xbc = _norm_proj(h, wts["norm_w"][i], lw["in_w"], MAMBA_SEGS, MAMBA_OUTS)
            (dtp,) = _norm_proj(h, wts["norm_w"][i], lw["dt_w"], MAMBA_DT_SEGS, MAMBA_DT_OUTS)
            y, h_fin = _ssd(z, xbc, dtp, lw, consts, ssm0, conv0, i, nbatch=nbatch,
                            rows_in=seq, nchunks=1, y_dtype=F32)
            new_conv.append(xbc.reshape(nbatch, seq, CONV_DIM)[:, seq - (CONV_K - 1):])
            h = _out_ple([y], "mamba", h, p_all, i, lw["out_w"], *ple)
        new_ssm.append(h_fin.reshape(nbatch, SSM_HEADS, SSM_HEADDIM, D_STATE))
    cos, sin = rope_tabs
    kv_outs = _norm_proj(h, wts["kv_norm_w"], wts["kv_w"],
                         KV_SEGS_RES if prompt else KV_SEGS_TOK,
                         KV_OUTS_RES if prompt else KV_OUTS_TOK,
                         rope_inputs=(wts["k_norm_w"], cos, sin, consts["bd"]), rope_rows=rope_rows,
                         nbatch=nbatch)
    k, v = kv_outs[:2]
    for j in range(DEPTH - N_A_LAYERS):
        i = N_A_LAYERS + j
        aw = wts["attn"][j]
        q_outs = _norm_proj(h, wts["norm_w"][i], aw["in_w"],
                            ATTN_SEGS_RES if prompt else ATTN_SEGS_TOK,
                            ATTN_OUTS_RES if prompt else ATTN_OUTS_TOK,
                            rope_inputs=(aw["q_norm_w"], cos, sin, consts["bd"]), rope_rows=rope_rows,
                            nbatch=nbatch)
        gate = q_outs[-1]
        if prompt:
            os_, ls_ = [], []
            for gi in range(N_DGROUPS):
                o_g, l_g = _attn_prompt_group(q_outs[gi], kv_outs[2 + gi], kv_outs[2 + N_DGROUPS + gi], gi)
                os_.append(o_g)
                ls_.append(l_g)
            mix_inputs, mode = os_ + ls_ + [gate, consts["lse_spread"]], "attn_merge"
        else:
            o = _attn_sample(q_outs[0], k, v, caches, nbatch, seq)
            mix_inputs, mode = [o, gate], "attn"
        h = _out_ple(mix_inputs, mode, h, p_all, i, aw["out_w"], wts["ple_norm_w"][i],
                     wts["ple_gate_w"][i], wts["ple_w"][i])
    return h, jnp.stack(new_ssm, axis=0), jnp.stack(new_conv, axis=0), k, v


def kernel(x_prompt, x_sample, state_ssm, state_conv, cache_kv_g1, cache_kv_g2, cache_kv_g3,
           p_prompt, p_sample, norm_w, m_in_w, m_conv_w, m_conv_b, m_dt_bias, m_A_log, m_D,
           m_norm_w, m_out_w, kv_norm_w, kv_w, k_norm_w, a_in_w, a_q_norm_w, a_out_w,
           ple_w, ple_gate_w, ple_norm_w):
    wts = _prep_weights(norm_w, m_in_w, m_conv_w, m_conv_b, m_dt_bias, m_A_log, m_D, m_norm_w,
                        m_out_w, kv_norm_w, kv_w, k_norm_w, a_in_w, a_q_norm_w, a_out_w, ple_w,
                        ple_gate_w, ple_norm_w)
    consts = _constants()

    b_p, seq = x_prompt.shape[0], x_prompt.shape[1]
    m_p = b_p * seq
    tabs_p = _rope_tables(jnp.arange(seq, dtype=jnp.int32))
    y_p, ssm_p, conv_p, k_p, v_p = _trunk(
        x_prompt.reshape(m_p, D_MODEL), p_prompt.reshape(DEPTH, m_p, PLE_DIM), wts, consts,
        tabs_p, seq, nbatch=b_p, seq=seq, ssm0=None, conv0=None, caches=None)

    b_s, dec = x_sample.shape[0], x_sample.shape[1]
    m_s = b_s * dec
    pos_s = PAST_LEN + jnp.arange(dec, dtype=jnp.int32)
    tabs_s = tuple(jnp.tile(t, (b_s, 1)) for t in _rope_tables(pos_s))
    ssm0_s = state_ssm.reshape(N_A_LAYERS, b_s, D_INNER, D_STATE)
    conv0_s = jnp.pad(state_conv, ((0, 0), (0, 0), (SUBLANES - (CONV_K - 1), 0), (0, 0)))
    caches = tuple(jnp.transpose(cch, (0, 2, 3, 4, 1)).reshape(b_s, 2 * GROUP_KV, cch.shape[1])
                   for cch in (cache_kv_g1, cache_kv_g2, cache_kv_g3))
    y_s, ssm_s, conv_s, k_s, v_s = _trunk(
        x_sample.reshape(m_s, D_MODEL), p_sample.reshape(DEPTH, m_s, PLE_DIM), wts, consts,
        tabs_s, m_s, nbatch=b_s, seq=dec, ssm0=ssm0_s, conv0=conv0_s, caches=caches)

    def kv_out(k, v, nbatch, length, gi, keep):
        def tail(a):
            a = a.reshape(nbatch, length, KV_HALF)[:, length - keep:, gi * GROUP_KV:(gi + 1) * GROUP_KV]
            return a.reshape(nbatch, keep, ATT_KV_HEADS, ATT_HEAD_DIM)
        return jnp.stack([tail(k), tail(v)], axis=2)

    kv_p = [kv_out(k_p, v_p, b_p, seq, gi, min(w, seq)) for gi, (w, _) in enumerate(DILATION_GROUPS)]
    kv_s = [kv_out(k_s, v_s, b_s, dec, gi, dec) for gi in range(N_DGROUPS)]
    return (y_p.reshape(b_p, seq, D_MODEL), y_s.reshape(b_s, dec, D_MODEL),
            ssm_p, conv_p, ssm_s, conv_s, kv_p[0], kv_p[1], kv_p[2], kv_s[0], kv_s[1], kv_s[2])
```

```python
import functools

import jax
import jax.numpy as jnp
from jax import lax
from jax.experimental import pallas as pl
from jax.experimental.pallas import tpu as pltpu

F32 = jnp.float32
BF16 = jnp.bfloat16

D_MODEL = 1024
SEQ = 8192
DEPTH = 4
PAST_LEN = 8192
N_A_LAYERS = DEPTH // 2
D_INNER = 2048
SSM_HEADDIM = 64
SSM_HEADS = 32
SSM_GROUPS = 4
D_STATE = 128
CONV_K = 4
BC_DIM = 2 * SSM_GROUPS * D_STATE
CONV_DIM = D_INNER + BC_DIM
SSD_CHUNK = 128
ATT_HEAD_DIM = 64
ATT_HEADS = 16
ATT_KV_HEADS = 4
DILATION_GROUPS = ((128, 1), (512, 4), (2048, 16))
N_DGROUPS = 3
ATT_WIDTH = 1024
KV_HALF = N_DGROUPS * ATT_KV_HEADS * ATT_HEAD_DIM
GROUP_KV = ATT_KV_HEADS * ATT_HEAD_DIM
N_KEYS = 129
ROPE_THETA = 10000.0
PLE_DIM = 256
EPS = 1e-6
LOG2E = 1.4426950408889634
LN2 = 0.6931471805599453

LANES = 128
SUBLANES = 8
VMEM_LIMIT_BYTES = 56 * 1024 * 1024
ROW_TILE = 512
NEG_BIG = -1e30
RELAYOUT_SLOTS = 4
ATTN_Q_BLOCKS = 4
SAMPLE_SEQS_PER_STEP = 2
MAMBA_ROWS = 256
MAMBA_PROJ_PIECE = 256

HEAD_ORDER = (0, 4, 1, 5, 2, 6, 3, 7, 8, 12, 9, 13, 10, 14, 11, 15)


def _cparams(semantics):
    return pltpu.CompilerParams(dimension_semantics=semantics,
                                vmem_limit_bytes=VMEM_LIMIT_BYTES)


def _const_spec(shape):
    nd = len(shape)
    return pl.BlockSpec(shape, lambda *_: (0,) * nd, pipeline_mode=pl.Buffered(1))


def _split_bf16(x, parts):
    out = []
    rem = x
    for _ in range(parts):
        hi = rem.astype(BF16)
        out.append(hi)
        rem = rem - hi.astype(F32)
    return out


def _dot(a, b):
    return jnp.dot(a, b, preferred_element_type=F32)


def _dot_nt(a, b):
    return lax.dot_general(a, b, (((1,), (1,)), ((), ())), preferred_element_type=F32)


def _sigmoid(x):
    return 0.5 + 0.5 * jnp.tanh(0.5 * x)


def _silu(x):
    hx = 0.5 * x
    return hx + hx * jnp.tanh(hx)


def _lo_mask(rows):
    return lax.broadcasted_iota(jnp.int32, (rows, LANES), 1) < ATT_HEAD_DIM


def _rope_partner(v):
    lane = lax.broadcasted_iota(jnp.int32, v.shape, 1)
    return jnp.where((lane & 32) == 0, pltpu.roll(v, 96, 1), pltpu.roll(v, 32, 1))


def _head_norm_rope(y, ms, hw, cos, sin, scale):
    yn = y * lax.rsqrt(ms + EPS) * (hw * scale)
    return yn * cos + _rope_partner(yn) * sin


def _norm_proj_kernel(*refs, segs, use_rope, n_out, tm):
    h_ref, nw_ref, w_ref = refs[:3]
    pos = 3
    if use_rope:
        hw_ref, cos_ref, sin_ref, bd_ref = refs[3:7]
        pos = 7
    out_refs = refs[pos:pos + n_out]
    scr = refs[pos + n_out] if len(refs) > pos + n_out else None
    x = h_ref[...]
    ms = jnp.mean(x * x, axis=-1, keepdims=True)
    u = (x * lax.rsqrt(ms + EPS) * nw_ref[...]).astype(BF16)
    slot = 0
    for start, width, rope, scale, sinks in segs:
        acc = _dot(u, w_ref[:, start:start + width])
        if not rope and all(dil == 0 for _, _, dil in sinks):
            for oi, col_off, _ in sinks:
                out_refs[oi][:, col_off:col_off + width] = acc.astype(out_refs[oi].dtype)
            continue
        for cb in range(width // LANES):
            val = acc[:, cb * LANES:(cb + 1) * LANES]
            if rope:
                if cb % 2 == 0:
                    y2 = acc[:, cb * LANES:(cb + 2) * LANES]
                    ms2 = _dot((y2 * y2).astype(BF16), bd_ref[...])
                val = _head_norm_rope(val, ms2[:, (cb % 2) * LANES:(cb % 2 + 1) * LANES],
                                      hw_ref[...], cos_ref[...], sin_ref[...], scale)
            for oi, col_off, dil in sinks:
                o_ref = out_refs[oi]
                cs = slice(col_off + cb * LANES, col_off + (cb + 1) * LANES)
                if dil == 0:
                    o_ref[:, cs] = val.astype(o_ref.dtype)
                elif dil == 1:
                    o_ref[0, :, cs] = val.astype(o_ref.dtype)
                else:
                    s = slot % RELAYOUT_SLOTS
                    slot += 1
                    scr[s] = val
                    for r in range(dil):
                        o_ref[r, :, cs] = scr[s, pl.ds(r, tm // dil, stride=dil), :].astype(o_ref.dtype)


def _norm_proj(h, nw, w, segs, out_defs, rope_inputs=None, rope_rows=None, nbatch=None):
    m = h.shape[0]
    tm = min(ROW_TILE, m)
    n = w.shape[1]
    use_rope = rope_inputs is not None
    in_specs = [pl.BlockSpec((tm, D_MODEL), lambda i: (i, 0)),
                _const_spec((1, D_MODEL)),
                _const_spec((D_MODEL, n))]
    args = [h, nw, w]
    if use_rope:
        hw, cos, sin, bd = rope_inputs
        nblk = rope_rows // tm
        in_specs += [_const_spec((1, LANES)),
                     pl.BlockSpec((tm, LANES), lambda i: (i % nblk, 0)),
                     pl.BlockSpec((tm, LANES), lambda i: (i % nblk, 0)),
                     _const_spec((2 * LANES, 2 * LANES))]
        args += [hw, cos, sin, bd]
    out_shape, out_specs = [], []
    for width, dt, dil in out_defs:
        if dil == 0:
            out_shape.append(jax.ShapeDtypeStruct((m, width), dt))
            out_specs.append(pl.BlockSpec((tm, width), lambda i: (i, 0)))
        else:
            tiles = m // nbatch // tm
            out_shape.append(jax.ShapeDtypeStruct((nbatch, dil, m // nbatch // dil, width), dt))
            out_specs.append(pl.BlockSpec((None, dil, tm // dil, width),
                                          lambda i, tiles=tiles: (i // tiles, 0, i % tiles, 0)))
    scratch = []
    if any(dil > 1 for _, _, dil in out_defs):
        scratch.append(pltpu.VMEM((RELAYOUT_SLOTS, tm, LANES), F32))
    return pl.pallas_call(
        functools.partial(_norm_proj_kernel, segs=segs, use_rope=use_rope, n_out=len(out_defs), tm=tm),
        grid=(m // tm,),
        in_specs=in_specs,
        out_specs=tuple(out_specs),
        out_shape=tuple(out_shape),
        scratch_shapes=scratch,
        compiler_params=_cparams(("parallel",)),
        name="norm_proj",
    )(*args)


def _ssd_kernel(z_ref, xbc_ref, dt_ref, cw_ref, cb_ref, dtb_ref, alog_ref, dexp_ref, nw_ref,
                tril_ref, e_ref, h0_ref, c0_ref, y_ref, hout_ref, ht_scr, xpad_scr,
                *, rows_in, nchunks):
    q = SSD_CHUNK
    c = pl.program_id(1)

    @pl.when(c == 0)
    def _():
        ht_scr[...] = jnp.transpose(h0_ref[...])
        xpad_scr[...] = c0_ref[...]

    def pad_rows(v):
        if rows_in == q:
            return v
        return jnp.concatenate([v, jnp.zeros((q - rows_in, v.shape[1]), v.dtype)], axis=0)

    y = _ssd_chunk(pad_rows(z_ref[...]), pad_rows(xbc_ref[...]), pad_rows(dt_ref[...]),
                   cw_ref, cb_ref, dtb_ref, alog_ref, dexp_ref, nw_ref, tril_ref, e_ref,
                   ht_scr, xpad_scr, rows_in)
    y_ref[...] = y[:rows_in].astype(y_ref.dtype)

    @pl.when(c == nchunks - 1)
    def _():
        hout_ref[...] = jnp.transpose(ht_scr[...])


def _ssd_chunk(zz, xbc, dt_raw, cw_ref, cb_ref, dtb_ref, alog_ref, dexp_ref, nw_ref, tril_ref, e_ref,
               ht_scr, xpad_scr, rows_valid, side=None):
    q = SSD_CHUNK
    gw = D_INNER // SSM_GROUPS

    def run_side(n=1):
        for _ in range(n):
            if side:
                side.pop(0)()

    tail = xpad_scr[...]
    xbc3 = xbc.reshape(q // SUBLANES, SUBLANES, CONV_DIM)
    row8 = lax.broadcasted_iota(jnp.int32, (1, SUBLANES, CONV_DIM), 1)
    conv = cb_ref[...] + cw_ref[CONV_K - 1:CONV_K, :] * xbc
    for k in range(CONV_K - 1):
        sh = CONV_K - 1 - k
        rot3 = pltpu.roll(xbc3, sh, 1)
        prev3 = jnp.concatenate([pltpu.roll(tail, sh, 0)[None], rot3[:-1]], axis=0)
        shifted = jnp.where(row8 < sh, prev3, rot3).reshape(q, CONV_DIM)
        conv = conv + cw_ref[k:k + 1, :] * shifted
    xpad_scr[...] = xbc[q - SUBLANES:q]
    run_side()
    act = _silu(conv)
    xs = act[:, :D_INNER]
    bm = act[:, D_INNER:D_INNER + SSM_GROUPS * D_STATE]
    cm = act[:, D_INNER + SSM_GROUPS * D_STATE:]

    dtr = dt_raw + dtb_ref[...]
    dt = jnp.maximum(dtr, 0.0) + jnp.log1p(jnp.exp(-jnp.abs(dtr)))
    if rows_valid < q:
        row = lax.broadcasted_iota(jnp.int32, dt.shape, 0)
        dt = jnp.where(row < rows_valid, dt, 0.0)
    a = dt * (-LOG2E * jnp.exp(alog_ref[...]))
    tril = tril_ref[...]
    acum = sum(_dot(tril, part) for part in _split_bf16(a, 3))
    acum_t = jnp.transpose(acum)

    e = e_ref[...]
    grp = lax.broadcasted_iota(jnp.int32, (q, LANES), 1) >> 5
    zero_l = jnp.zeros((q, LANES), BF16)

    def packed(parts):
        out = zero_l
        for idx, part in enumerate(parts):
            out = jnp.where(grp == idx, part, out)
        return out

    dt_e = _dot(packed(_split_bf16(dt, 2)), e)
    acum_e = _dot(packed(_split_bf16(acum, 3)), e)
    alast_e = acum_e[q - 1:q, :]
    exp_acum_e = jnp.exp2(acum_e)
    decay_end_e = jnp.exp2(alast_e - acum_e)
    chunk_decay_e = jnp.exp2(alast_e)

    xdt = xs * dt_e
    xdt_bf = xdt.astype(BF16)
    xdtw_bf = (xdt * decay_end_e).astype(BF16)

    li = lax.broadcasted_iota(jnp.int32, (q, q), 0)
    si = lax.broadcasted_iota(jnp.int32, (q, q), 1)
    causal = li >= si
    lo = _lo_mask(q)
    zero_bf = jnp.zeros((q, LANES), BF16)

    y_parts = []
    for g in range(SSM_GROUPS):
        gs = slice(g * gw, (g + 1) * gw)
        bg = bm[:, g * D_STATE:(g + 1) * D_STATE]
        cg_bf = cm[:, g * D_STATE:(g + 1) * D_STATE].astype(BF16)
        cb = jnp.where(causal, _dot_nt(cg_bf, bg.astype(BF16)), 0.0)
        htg = ht_scr[:, gs]
        y_off = _dot(cg_bf, htg.astype(BF16)) * exp_acum_e[:, gs]
        blocks = []
        for j in range(gw // LANES):
            hd = g * (SSM_HEADS // SSM_GROUPS) + 2 * j
            xpair = xdt_bf[:, hd * SSM_HEADDIM:hd * SSM_HEADDIM + LANES]
            mats = []
            for hh in (hd, hd + 1):
                seg = acum[:, hh:hh + 1] - acum_t[hh:hh + 1, :]
                mats.append((cb * jnp.exp2(jnp.minimum(seg, 0.0))).astype(BF16))
            xstack = jnp.concatenate([jnp.where(lo, xpair, zero_bf), jnp.where(lo, zero_bf, xpair)], axis=0)
            blocks.append(_dot(jnp.concatenate(mats, axis=1), xstack))
            run_side()
        y_diag = jnp.concatenate(blocks, axis=1)
        bgt_bf = jnp.transpose(bg).astype(BF16)
        st = _dot(bgt_bf, xdtw_bf[:, gs])
        ht_scr[:, gs] = htg * chunk_decay_e[:, gs] + st
        y_parts.append(y_diag + y_off)

    y = jnp.concatenate(y_parts, axis=1) + xs * dexp_ref[...]
    y = y * _silu(zz)
    run_side()
    normed = []
    for g in range(SSM_GROUPS):
        yg = y[:, g * gw:(g + 1) * gw]
        ms = jnp.mean(yg * yg, axis=-1, keepdims=True)
        normed.append(yg * lax.rsqrt(ms + EPS))
    return jnp.concatenate(normed, axis=1) * nw_ref[...]


def _ssd(z, xbc, dtp, lw, consts, h0, c0, layer, *, nbatch, rows_in, nchunks, y_dtype):
    m = z.shape[0]
    row_map = lambda b, c: (b * nchunks + c, 0)
    batch_map = lambda b, c: (b, 0, 0)
    state_map = lambda b, c: (layer, b, 0, 0)
    in_specs = [
        pl.BlockSpec((rows_in, D_INNER), row_map),
        pl.BlockSpec((rows_in, CONV_DIM), row_map),
        pl.BlockSpec((rows_in, LANES), row_map),
        _const_spec((CONV_K, CONV_DIM)),
        _const_spec((1, CONV_DIM)),
        _const_spec((1, LANES)),
        _const_spec((1, LANES)),
        _const_spec((1, D_INNER)),
        _const_spec((1, D_INNER)),
        _const_spec((SSD_CHUNK, SSD_CHUNK)),
        _const_spec((LANES, D_INNER)),
        pl.BlockSpec((None, None, D_INNER, D_STATE), state_map),
        pl.BlockSpec((None, None, SUBLANES, CONV_DIM), state_map),
    ]
    out_specs = (pl.BlockSpec((rows_in, D_INNER), row_map),
                 pl.BlockSpec((None, D_INNER, D_STATE), batch_map))
    out_shape = (jax.ShapeDtypeStruct((m, D_INNER), y_dtype),
                 jax.ShapeDtypeStruct((nbatch, D_INNER, D_STATE), F32))
    return pl.pallas_call(
        functools.partial(_ssd_kernel, rows_in=rows_in, nchunks=nchunks),
        grid=(nbatch, nchunks),
        in_specs=in_specs,
        out_specs=out_specs,
        out_shape=out_shape,
        scratch_shapes=[pltpu.VMEM((D_STATE, D_INNER), F32),
                        pltpu.VMEM((SUBLANES, CONV_DIM), F32)],
        compiler_params=_cparams(("parallel", "arbitrary")),
        name="ssd",
    )(z, xbc, dtp, lw["conv_w"], lw["conv_b"], lw["dt_bias"], lw["a_log"], lw["d_exp"],
      lw["norm_w"], consts["tril"], consts["expand"], h0, c0)


def _mamba_layer_kernel(ha_ref, hc_ref, p_ref, nw_ref, win_ref, wdt_ref, cw_ref, cb_ref, dtb_ref, alog_ref,
                        dexp_ref, mnw_ref, tril_ref, e_ref, wout_ref, pnw_ref, gw_ref, pw_ref,
                        out_ref, sout_ref, cout_ref,
                        proj0, proj1, y0, y1, ht_scr, xpad_scr, *, nsteps, rows):
    s = pl.program_id(1)
    q = SSD_CHUNK

    @pl.when(s == 0)
    def _():
        proj1[...] = jnp.zeros(proj1.shape, proj1.dtype)
        y0[...] = jnp.zeros(y0.shape, y0.dtype)
        y1[...] = jnp.zeros(y1.shape, y1.dtype)

    @pl.when(s <= 1)
    def _():
        ht_scr[...] = jnp.zeros(ht_scr.shape, ht_scr.dtype)
        xpad_scr[...] = jnp.zeros(xpad_scr.shape, xpad_scr.dtype)

    def stages(proj_w, proj_r, y_w, y_r):
        vals = {}

        def a_norm():
            x = ha_ref[...]
            ms = jnp.mean(x * x, axis=-1, keepdims=True)
            vals["u"] = (x * lax.rsqrt(ms + EPS) * nw_ref[...]).astype(BF16)

        def a_piece(c0, c1):
            def emit():
                proj_w[:, c0:c1] = _dot(vals["u"], win_ref[:, c0:c1])
            return emit

        def c_out(c0, c1):
            def emit():
                vals["h1", c0] = hc_ref[:, c0:c1] + _dot(y_r[...], wout_ref[:, c0:c1])
            return emit

        def c_norm():
            h1 = jnp.concatenate([vals["h1", c0] for c0 in c_cols], axis=1)
            vals["h1"] = h1
            ms = jnp.mean(h1 * h1, axis=-1, keepdims=True)
            vals["un"] = (h1 * lax.rsqrt(ms + EPS) * pnw_ref[...]).astype(BF16)

        def c_gate(c0, c1):
            def emit():
                gate = _sigmoid(_dot(vals["un"], gw_ref[:, c0:c1]))
                pe = _dot(p_ref[...].astype(BF16), pw_ref[:, c0:c1])
                out_ref[:, c0:c1] = vals["h1"][:, c0:c1] + gate * pe
            return emit

        def a_dt():
            proj_w[:, n_main:] = _dot(vals["u"], wdt_ref[...])

        n_main = D_INNER + CONV_DIM
        step = MAMBA_PROJ_PIECE
        a_work = [a_piece(c0, c0 + step) for c0 in range(0, n_main, step)] + [a_dt]
        c_cols = list(range(0, D_MODEL, step))
        c_work = ([c_out(c0, c0 + step) for c0 in c_cols] + [c_norm]
                  + [c_gate(c0, c0 + step) for c0 in c_cols])
        side = [a_norm]
        while a_work or c_work:
            if c_work:
                side.append(c_work.pop(0))
            if a_work:
                side.append(a_work.pop(0))
            if a_work:
                side.append(a_work.pop(0))
        for j in range(rows // q):
            rs = slice(j * q, (j + 1) * q)
            y = _ssd_chunk(proj_r[rs, 0:D_INNER], proj_r[rs, D_INNER:D_INNER + CONV_DIM],
                           proj_r[rs, D_INNER + CONV_DIM:], cw_ref, cb_ref, dtb_ref, alog_ref,
                           dexp_ref, mnw_ref, tril_ref, e_ref, ht_scr, xpad_scr, q, side=side)
            y_w[rs, :] = y.astype(y_w.dtype)
        while side:
            side.pop(0)()

    @pl.when(s % 2 == 0)
    def _():
        stages(proj0, proj1, y1, y0)

    @pl.when(s % 2 == 1)
    def _():
        stages(proj1, proj0, y0, y1)

    @pl.when(s == nsteps)
    def _():
        sout_ref[...] = jnp.transpose(ht_scr[...])
        cout_ref[...] = xpad_scr[...]


def _mamba_layer(h, p_all, layer, nw, lw, consts, pnw, gw, pw, *, nbatch, seq):
    m = h.shape[0]
    rows = MAMBA_ROWS
    nsteps = seq // rows
    n_proj = D_INNER + CONV_DIM + LANES
    a_map = lambda b, s: (b * nsteps + jnp.minimum(s, nsteps - 1), 0)
    c_map = lambda b, s: (b * nsteps + jnp.clip(s - 2, 0, nsteps - 1), 0)
    in_specs = [
        pl.BlockSpec((rows, D_MODEL), a_map),
        pl.BlockSpec((rows, D_MODEL), c_map),
        pl.BlockSpec((None, rows, PLE_DIM), lambda b, s: (layer, b * nsteps + jnp.clip(s - 2, 0, nsteps - 1), 0)),
        _const_spec((1, D_MODEL)),
        _const_spec(lw["in_w"].shape),
        _const_spec((D_MODEL, LANES)),
        _const_spec((CONV_K, CONV_DIM)),
        _const_spec((1, CONV_DIM)),
        _const_spec((1, LANES)),
        _const_spec((1, LANES)),
        _const_spec((1, D_INNER)),
        _const_spec((1, D_INNER)),
        _const_spec((SSD_CHUNK, SSD_CHUNK)),
        _const_spec((LANES, D_INNER)),
        _const_spec((D_INNER, D_MODEL)),
        _const_spec((1, D_MODEL)),
        _const_spec((D_MODEL, D_MODEL)),
        _const_spec((PLE_DIM, D_MODEL)),
    ]
    out_specs = (pl.BlockSpec((rows, D_MODEL), c_map),
                 pl.BlockSpec((None, D_INNER, D_STATE), lambda b, s: (b, 0, 0)),
                 pl.BlockSpec((None, SUBLANES, CONV_DIM), lambda b, s: (b, 0, 0)))
    out_shape = (jax.ShapeDtypeStruct((m, D_MODEL), F32),
                 jax.ShapeDtypeStruct((nbatch, D_INNER, D_STATE), F32),
                 jax.ShapeDtypeStruct((nbatch, SUBLANES, CONV_DIM), F32))
    return pl.pallas_call(
        functools.partial(_mamba_layer_kernel, nsteps=nsteps, rows=rows),
        grid=(nbatch, nsteps + 2),
        in_specs=in_specs,
        out_specs=out_specs,
        out_shape=out_shape,
        scratch_shapes=[pltpu.VMEM((rows, n_proj), F32), pltpu.VMEM((rows, n_proj), F32),
                        pltpu.VMEM((rows, D_INNER), BF16), pltpu.VMEM((rows, D_INNER), BF16),
                        pltpu.VMEM((D_STATE, D_INNER), F32),
                        pltpu.VMEM((SUBLANES, CONV_DIM), F32)],
        compiler_params=_cparams(("parallel", "arbitrary")),
        name="mamba_layer",
    )(h, h, p_all, nw, lw["in_w"], lw["dt_w"], lw["conv_w"], lw["conv_b"], lw["dt_bias"], lw["a_log"],
      lw["d_exp"], lw["norm_w"], consts["tril"], consts["expand"], lw["out_w"], pnw, gw, pw)


def _attn_prompt_kernel(q_ref, kcur_ref, kprev_ref, vcur_ref, vprev_ref, o_ref, l_ref, *, nq):
    i = pl.program_id(2)
    t = SSD_CHUNK
    npair = GROUP_KV // ATT_HEAD_DIM
    qi = lax.broadcasted_iota(jnp.int32, (t, 2 * t), 0)
    kk = lax.broadcasted_iota(jnp.int32, (t, 2 * t), 1)
    band = (kk >= qi) & (kk <= qi + (N_KEYS - 1))
    bias_inner = jnp.where(band, 0.0, NEG_BIG)
    bias_first = jnp.where(band & ((kk >= t) | (i > 0)), 0.0, NEG_BIG)
    lo = _lo_mask(t)
    lo2 = _lo_mask(2 * t)
    zero_v = jnp.zeros((2 * t, LANES), BF16)
    ones_stack = jnp.concatenate([jnp.where(lo2, 1.0, 0.0), jnp.where(lo2, 0.0, 1.0)],
                                 axis=0).astype(BF16)
    lane = lax.broadcasted_iota(jnp.int32, (t, LANES), 1)
    lane_slot = (lane & (ATT_HEAD_DIM - 1)) >> 3
    for s in range(nq):
        rows = slice(s * t, (s + 1) * t)
        bias = bias_first if s == 0 else bias_inner
        l_c = jnp.zeros((t, LANES), F32)
        for kc in range(GROUP_KV // LANES):
            ks = slice(kc * LANES, (kc + 1) * LANES)
            if s == 0:
                k2 = jnp.concatenate([kprev_ref[:, ks], kcur_ref[0:t, ks]], axis=0)
                v2 = jnp.concatenate([vprev_ref[:, ks], vcur_ref[0:t, ks]], axis=0)
            else:
                k2 = kcur_ref[(s - 1) * t:(s + 1) * t, ks]
                v2 = vcur_ref[(s - 1) * t:(s + 1) * t, ks]
            k_stack = jnp.concatenate([jnp.where(lo2, k2, zero_v), jnp.where(lo2, zero_v, k2)], axis=0)
            v_stack = jnp.concatenate([jnp.where(lo2, v2, zero_v), jnp.where(lo2, zero_v, v2)], axis=0)
            rhs = jnp.concatenate([v_stack, ones_stack], axis=1)
            lhs = jnp.concatenate(
                [q_ref[rows, (kc * npair + pb) * LANES:(kc * npair + pb + 1) * LANES] for pb in range(npair)],
                axis=0)
            sc = _dot_nt(lhs, k_stack).reshape(npair, t, 4 * t)
            sc_a = sc[:, :, :2 * t] + bias[None]
            sc_b = sc[:, :, 2 * t:] + bias[None]
            mx_a = jnp.max(sc_a, axis=-1, keepdims=True)
            mx_b = jnp.max(sc_b, axis=-1, keepdims=True)
            p_cat = jnp.concatenate([jnp.exp2(sc_a - mx_a), jnp.exp2(sc_b - mx_b)], axis=-1)
            res = _dot(p_cat.astype(BF16).reshape(npair * t, 4 * t), rhs)
            for pb in range(npair):
                blk = kc * npair + pb
                num = res[pb * t:(pb + 1) * t, :LANES]
                den = res[pb * t:(pb + 1) * t, LANES:]
                o_ref[rows, blk * LANES:(blk + 1) * LANES] = (num / den).astype(o_ref.dtype)
                lse = jnp.where(lo, mx_a[pb], mx_b[pb]) * LN2 + jnp.log(den)
                l_c = jnp.where(lane_slot == blk, lse, l_c)
        l_ref[rows, :] = l_c


def _attn_prompt_group(q, k, v, gi):
    nbatch, dil, rows, _ = q.shape
    t = SSD_CHUNK
    nq = ATTN_Q_BLOCKS
    cur = lambda b, r, i: (b, r, i, 0)
    prev = lambda b, r, i: (b, r, jnp.maximum(nq * i - 1, 0), 0)
    return pl.pallas_call(
        functools.partial(_attn_prompt_kernel, nq=nq),
        grid=(nbatch, dil, rows // (nq * t)),
        in_specs=[pl.BlockSpec((None, None, nq * t, ATT_WIDTH), cur),
                  pl.BlockSpec((None, None, nq * t, GROUP_KV), cur),
                  pl.BlockSpec((None, None, t, GROUP_KV), prev),
                  pl.BlockSpec((None, None, nq * t, GROUP_KV), cur),
                  pl.BlockSpec((None, None, t, GROUP_KV), prev)],
        out_specs=(pl.BlockSpec((None, None, nq * t, ATT_WIDTH), cur),
                   pl.BlockSpec((None, None, nq * t, LANES), cur)),
        out_shape=(jax.ShapeDtypeStruct((nbatch, dil, rows, ATT_WIDTH), BF16),
                   jax.ShapeDtypeStruct((nbatch, dil, rows, LANES), F32)),
        compiler_params=_cparams(("parallel", "parallel", "arbitrary")),
        name=f"attn_prompt_g{gi}",
    )(q, k, k, v, v)


def _attn_sample_kernel(q_ref, kn_ref, vn_ref, c1_ref, c2_ref, c3_ref, o_ref, *, dec_seq, nseq):
    for bb in range(nseq):
        rs = slice(bb * dec_seq, (bb + 1) * dec_seq)
        _attn_sample_one(q_ref.at[rs], kn_ref.at[rs], vn_ref.at[rs],
                         (c1_ref.at[bb], c2_ref.at[bb], c3_ref.at[bb]), o_ref.at[rs], dec_seq)


def _attn_sample_one(q_ref, kn_ref, vn_ref, caches, o_ref, dec_seq):
    nrow = 8 * dec_seq
    lo_row = _lo_mask(dec_seq)
    pad = LANES - dec_seq
    outs = [[None] * N_DGROUPS for _ in range(ATT_WIDTH // LANES)]
    lses = [[None] * N_DGROUPS for _ in range(ATT_WIDTH // LANES)]
    for gi, (_, dil) in enumerate(DILATION_GROUPS):
        cref = caches[gi]
        w = cref.shape[1]
        tq = lax.broadcasted_iota(jnp.int32, (nrow, w), 0) & (dec_seq - 1)
        rho = lax.broadcasted_iota(jnp.int32, (nrow, w), 1)
        delta = w + tq - rho
        valid_c = (rho >= tq) & ((delta & (dil - 1)) == 0) & (delta <= (N_KEYS - 1) * dil)
        bias_c = jnp.where(valid_c, 0.0, NEG_BIG)
        tq_n = lax.broadcasted_iota(jnp.int32, (nrow, LANES), 0) & (dec_seq - 1)
        tn = lax.broadcasted_iota(jnp.int32, (nrow, LANES), 1)
        valid_n = (tn <= tq_n) & (((tq_n - tn) & (dil - 1)) == 0)
        bias_n = jnp.where(valid_n, 0.0, NEG_BIG)
        gsl = slice(gi * GROUP_KV, (gi + 1) * GROUP_KV)
        k_new = jnp.concatenate([kn_ref[:, gsl], jnp.zeros((pad, GROUP_KV), F32)], axis=0)
        v_new = jnp.concatenate([vn_ref[:, gsl], jnp.zeros((pad, GROUP_KV), F32)], axis=0)
        for kc in range(GROUP_KV // LANES):
            ks = slice(kc * LANES, (kc + 1) * LANES)
            rows = []
            for pb in range(4):
                blk = kc * 4 + pb
                q2 = q_ref[:, gi * ATT_WIDTH + blk * LANES:gi * ATT_WIDTH + (blk + 1) * LANES]
                rows.append(jnp.where(lo_row, q2, 0.0))
                rows.append(jnp.where(lo_row, 0.0, q2))
            lhs = jnp.concatenate(rows, axis=0).astype(BF16)
            kcache_t = cref[ks, :].astype(BF16)
            vcache_t = cref[GROUP_KV + kc * LANES:GROUP_KV + (kc + 1) * LANES, :].astype(BF16)
            s_c = _dot(lhs, kcache_t) + bias_c
            s_n = _dot_nt(lhs, k_new[:, ks].astype(BF16)) + bias_n
            mx = jnp.maximum(jnp.max(s_c, axis=-1, keepdims=True),
                             jnp.max(s_n, axis=-1, keepdims=True))
            p_c = jnp.exp(s_c - mx)
            p_n = jnp.exp(s_n - mx)
            den = jnp.sum(p_c, axis=-1, keepdims=True) + jnp.sum(p_n, axis=-1, keepdims=True)
            num = _dot_nt(p_c.astype(BF16), vcache_t) + _dot(p_n.astype(BF16), v_new[:, ks].astype(BF16))
            on = num / den
            lse = mx + jnp.log(den)
            for pb in range(4):
                blk = kc * 4 + pb
                r0 = pb * 2 * dec_seq
                r1 = r0 + dec_seq
                outs[blk][gi] = jnp.where(lo_row, on[r0:r1], on[r1:r1 + dec_seq])
                lses[blk][gi] = jnp.where(lo_row, lse[r0:r1], lse[r1:r1 + dec_seq])
    for blk in range(ATT_WIDTH // LANES):
        ls = lses[blk]
        mx = jnp.maximum(jnp.maximum(ls[0], ls[1]), ls[2])
        ws = [jnp.exp(l - mx) for l in ls]
        tot = ws[0] + ws[1] + ws[2]
        o = (outs[blk][0] * ws[0] + outs[blk][1] * ws[1] + outs[blk][2] * ws[2]) / tot
        o_ref[:, blk * LANES:(blk + 1) * LANES] = o


def _attn_sample(q, k, v, caches, nbatch, dec_seq):
    m = q.shape[0]
    nseq = SAMPLE_SEQS_PER_STEP
    rows = nseq * dec_seq
    row_map = lambda b: (b, 0)
    in_specs = [pl.BlockSpec((rows, N_DGROUPS * ATT_WIDTH), row_map),
                pl.BlockSpec((rows, KV_HALF), row_map),
                pl.BlockSpec((rows, KV_HALF), row_map)]
    for cch in caches:
        in_specs.append(pl.BlockSpec((nseq, 2 * GROUP_KV, cch.shape[2]), lambda b: (b, 0, 0)))
    return pl.pallas_call(
        functools.partial(_attn_sample_kernel, dec_seq=dec_seq, nseq=nseq),
        grid=(nbatch // nseq,),
        in_specs=in_specs,
        out_specs=pl.BlockSpec((rows, ATT_WIDTH), row_map),
        out_shape=jax.ShapeDtypeStruct((m, ATT_WIDTH), F32),
        compiler_params=_cparams(("parallel",)),
        name="attn_sample",
    )(q, k, v, *caches)


def _token_order(ref, scr, slot, cb, dil, tm):
    cs = slice(cb * LANES, (cb + 1) * LANES)
    if dil == 1:
        return ref[0, :, cs].astype(F32)
    for r in range(dil):
        scr[slot, pl.ds(r, tm // dil, stride=dil), :] = ref[r, :, cs].astype(F32)
    return scr[slot]


def _out_ple_kernel(*refs, mode, tm):
    if mode == "mamba":
        (y_ref,) = refs[:1]
        pos = 1
        mix = y_ref[...].astype(BF16)
    elif mode == "attn_merge":
        o_refs = refs[0:3]
        l_refs = refs[3:6]
        gate_ref, x_ref = refs[6:8]
        pos = 8
        scr, mix_scr = refs[-2:]
        refs = refs[:-2]
        dils = [d for _, d in DILATION_GROUPS]
        ls = [_token_order(l_refs[g], scr, g, 0, dils[g], tm) for g in range(N_DGROUPS)]
        mx = jnp.maximum(jnp.maximum(ls[0], ls[1]), ls[2])
        es = [jnp.exp(l - mx) for l in ls]
        tot = es[0] + es[1] + es[2]
        sub = lax.broadcasted_iota(jnp.int32, (tm, LANES), 1) & 7
        packed = jnp.zeros((tm, LANES), BF16)
        for g in range(N_DGROUPS):
            for idx, part in enumerate(_split_bf16(es[g] / tot, 2)):
                packed = jnp.where(sub == 2 * g + idx, part, packed)
        ws_all = _dot(packed, x_ref[...])
        ws = [ws_all[:, g * ATT_WIDTH:(g + 1) * ATT_WIDTH] for g in range(N_DGROUPS)]
        for cb in range(ATT_WIDTH // LANES):
            cs = slice(cb * LANES, (cb + 1) * LANES)
            o = None
            for g in range(N_DGROUPS):
                og = _token_order(o_refs[g], scr, N_DGROUPS + (cb * N_DGROUPS + g) % RELAYOUT_SLOTS,
                                  cb, dils[g], tm)
                term = og * ws[g][:, cs]
                o = term if o is None else o + term
            gate = gate_ref[:, cs]
            mix_scr[:, cs] = (o * _silu(gate)).astype(BF16)
        mix = mix_scr[...]
    else:
        o_ref_in, gate_ref = refs[:2]
        pos = 2
        gate = gate_ref[...]
        mix = (o_ref_in[...] * _silu(gate)).astype(BF16)
    h_ref, p_ref, wout_ref, pnw_ref, gw_ref, pw_ref, out_ref = refs[pos:]
    out_ref[...] = _residual_ple(mix, h_ref[...], p_ref[...], wout_ref, pnw_ref, gw_ref, pw_ref)


def _residual_ple(mix, h, p, wout_ref, pnw_ref, gw_ref, pw_ref):
    h1 = h + _dot(mix, wout_ref[...])
    ms = jnp.mean(h1 * h1, axis=-1, keepdims=True)
    un = (h1 * lax.rsqrt(ms + EPS) * pnw_ref[...]).astype(BF16)
    gate_p = _sigmoid(_dot(un, gw_ref[...]))
    pe = _dot(p.astype(BF16), pw_ref[...])
    return h1 + gate_p * pe


def _out_ple(mix_inputs, mode, h, p_all, layer, wout, pnw, gw, pw):
    m = h.shape[0]
    tm = min(ROW_TILE, m)
    row_map = lambda i: (i, 0)
    in_specs = []
    for a in mix_inputs:
        if a.ndim == 4:
            _, dil, rows, width = a.shape
            tiles = rows * dil // tm
            in_specs.append(pl.BlockSpec((None, dil, tm // dil, width),
                                         lambda i, tiles=tiles: (i // tiles, 0, i % tiles, 0)))
        elif a.shape[0] == m:
            in_specs.append(pl.BlockSpec((tm, a.shape[1]), row_map))
        else:
            in_specs.append(_const_spec(a.shape))
    scratch = []
    if mode == "attn_merge":
        scratch = [pltpu.VMEM((N_DGROUPS + RELAYOUT_SLOTS, tm, LANES), F32),
                   pltpu.VMEM((tm, ATT_WIDTH), BF16)]
    in_specs += [pl.BlockSpec((tm, D_MODEL), row_map),
                 pl.BlockSpec((None, tm, PLE_DIM), lambda i: (layer, i, 0)),
                 _const_spec(wout.shape),
                 _const_spec((1, D_MODEL)),
                 _const_spec((D_MODEL, D_MODEL)),
                 _const_spec((PLE_DIM, D_MODEL))]
    return pl.pallas_call(
        functools.partial(_out_ple_kernel, mode=mode, tm=tm),
        grid=(m // tm,),
        in_specs=in_specs,
        out_specs=pl.BlockSpec((tm, D_MODEL), row_map),
        out_shape=jax.ShapeDtypeStruct((m, D_MODEL), F32),
        scratch_shapes=scratch,
        compiler_params=_cparams(("parallel",)),
        name=f"out_ple_{mode}",
    )(*mix_inputs, h, p_all, wout, pnw, gw, pw)


def _rope_tables(pos):
    half = ATT_HEAD_DIM // 2
    inv = 1.0 / (ROPE_THETA ** (jnp.arange(half, dtype=F32) / half))
    ang = pos.astype(F32)[:, None] * inv[None, :]
    cos, sin = jnp.cos(ang), jnp.sin(ang)
    cos128 = jnp.concatenate([cos, cos, cos, cos], axis=1)
    sin128 = jnp.concatenate([-sin, sin, -sin, sin], axis=1)
    return cos128, sin128


def _permute_heads(w, axis):
    shape = w.shape
    assert tuple(8 * c + 4 * hf + j for c in range(2) for j in range(4) for hf in range(2)) == HEAD_ORDER
    w = w.reshape(shape[:axis] + (2, 2, ATT_HEADS // 4, ATT_HEAD_DIM) + shape[axis + 1:])
    return jnp.swapaxes(w, axis + 1, axis + 2).reshape(shape)


def _prep_weights(norm_w, m_in_w, m_conv_w, m_conv_b, m_dt_bias, m_A_log, m_D, m_norm_w, m_out_w,
                  kv_norm_w, kv_w, k_norm_w, a_in_w, a_q_norm_w, a_out_w, ple_w, ple_gate_w,
                  ple_norm_w):
    row = lambda v: v.reshape(1, -1).astype(F32)
    reps = LANES // SSM_HEADS
    lane_pad = lambda v: jnp.tile(v.astype(F32), reps).reshape(1, LANES)
    mamba = []
    for i in range(N_A_LAYERS):
        mamba.append(dict(
            in_w=m_in_w[i].astype(BF16),
            dt_w=jnp.tile(m_in_w[i][:, D_INNER + CONV_DIM:], (1, reps)).astype(BF16),
            conv_w=m_conv_w[i].astype(F32),
            conv_b=row(m_conv_b[i]),
            dt_bias=lane_pad(m_dt_bias[i]),
            a_log=lane_pad(m_A_log[i]),
            d_exp=row(jnp.repeat(m_D[i], SSM_HEADDIM)),
            norm_w=row(m_norm_w[i]),
            out_w=m_out_w[i].astype(BF16),
        ))
    attn = []
    for j in range(DEPTH - N_A_LAYERS):
        w = a_in_w[j].reshape(D_MODEL, N_DGROUPS + 1, ATT_WIDTH)
        w = _permute_heads(w, 2).reshape(D_MODEL, (N_DGROUPS + 1) * ATT_WIDTH)
        attn.append(dict(
            in_w=w.astype(BF16),
            q_norm_w=row(jnp.tile(a_q_norm_w[j], LANES // ATT_HEAD_DIM)),
            out_w=_permute_heads(a_out_w[j], 0).astype(BF16),
        ))
    return dict(
        norm_w=[row(norm_w[i]) for i in range(DEPTH)],
        mamba=mamba,
        attn=attn,
        kv_norm_w=row(kv_norm_w),
        kv_w=kv_w.astype(BF16),
        k_norm_w=row(jnp.tile(k_norm_w, LANES // ATT_HEAD_DIM)),
        ple_w=[ple_w[i].astype(BF16) for i in range(DEPTH)],
        ple_gate_w=[ple_gate_w[i].astype(BF16) for i in range(DEPTH)],
        ple_norm_w=[row(ple_norm_w[i]) for i in range(DEPTH)],
    )


def _constants():
    t = SSD_CHUNK
    tril = jnp.tril(jnp.ones((t, t), F32)).astype(BF16)
    head = jnp.arange(LANES, dtype=jnp.int32)[:, None]
    col = jnp.arange(D_INNER, dtype=jnp.int32)[None, :]
    expand = (head % SSM_HEADS == col // SSM_HEADDIM).astype(BF16)
    li = jnp.arange(LANES, dtype=jnp.int32)
    l2 = jnp.arange(2 * LANES, dtype=jnp.int32)
    bd = ((l2[:, None] // ATT_HEAD_DIM) == (l2[None, :] // ATT_HEAD_DIM)).astype(F32) / ATT_HEAD_DIM
    ccol = jnp.arange(N_DGROUPS * ATT_WIDTH, dtype=jnp.int32)[None, :]
    cgrp, cc = ccol // ATT_WIDTH, ccol % ATT_WIDTH
    slot = cc // LANES + jnp.where(cc % LANES < ATT_HEAD_DIM, 0, 8)
    lrow = li[:, None]
    lse_spread = ((lrow >> 3 == slot) & ((lrow & 7) >> 1 == cgrp)).astype(BF16)
    return dict(tril=tril, expand=expand, bd=bd.astype(BF16), lse_spread=lse_spread)


_Q_SCALE = ATT_HEAD_DIM ** -0.5
_DILS = tuple(d for _, d in DILATION_GROUPS)
MAMBA_SEGS = ((0, D_INNER, False, 1.0, ((0, 0, 0),)),
              (D_INNER, CONV_DIM, False, 1.0, ((1, 0, 0),)))
MAMBA_OUTS = ((D_INNER, F32, 0), (CONV_DIM, F32, 0))
MAMBA_DT_SEGS = ((0, LANES, False, 1.0, ((0, 0, 0),)),)
MAMBA_DT_OUTS = ((LANES, F32, 0),)
KV_SEGS_TOK = ((0, KV_HALF, True, 1.0, ((0, 0, 0),)), (KV_HALF, KV_HALF, False, 1.0, ((1, 0, 0),)))
KV_OUTS_TOK = ((KV_HALF, F32, 0), (KV_HALF, F32, 0))
ATTN_SEGS_TOK = ((0, N_DGROUPS * ATT_WIDTH, True, _Q_SCALE, ((0, 0, 0),)),
                 (N_DGROUPS * ATT_WIDTH, ATT_WIDTH, False, 1.0, ((1, 0, 0),)))
ATTN_OUTS_TOK = ((N_DGROUPS * ATT_WIDTH, F32, 0), (ATT_WIDTH, F32, 0))
KV_SEGS_RES = tuple(
    (g * GROUP_KV, GROUP_KV, True, 1.0, ((0, g * GROUP_KV, 0), (2 + g, 0, _DILS[g])))
    for g in range(N_DGROUPS)) + tuple(
    (KV_HALF + g * GROUP_KV, GROUP_KV, False, 1.0, ((1, g * GROUP_KV, 0), (2 + N_DGROUPS + g, 0, _DILS[g])))
    for g in range(N_DGROUPS))
KV_OUTS_RES = KV_OUTS_TOK + tuple((GROUP_KV, BF16, d) for d in _DILS) * 2
ATTN_SEGS_RES = tuple((g * ATT_WIDTH, ATT_WIDTH, True, _Q_SCALE * LOG2E, ((g, 0, _DILS[g]),))
                      for g in range(N_DGROUPS)) + (
    (N_DGROUPS * ATT_WIDTH, ATT_WIDTH, False, 1.0, ((N_DGROUPS, 0, 0),)),)
ATTN_OUTS_RES = tuple((ATT_WIDTH, BF16, d) for d in _DILS) + ((ATT_WIDTH, F32, 0),)


def _trunk(x, p_all, wts, consts, rope_tabs, rope_rows, *, nbatch, seq, ssm0, conv0, caches):
    prompt = caches is None
    m = x.shape[0]
    h = x
    new_ssm, new_conv = [], []
    for i in range(N_A_LAYERS):
        lw = wts["mamba"][i]
        ple = (wts["ple_norm_w"][i], wts["ple_gate_w"][i], wts["ple_w"][i])
        if prompt:
            h, h_fin, xbc_tail = _mamba_layer(h, p_all, i, wts["norm_w"][i], lw, consts, *ple,
                                              nbatch=nbatch, seq=seq)
            new_conv.append(xbc_tail[:, SUBLANES - (CONV_K - 1):])
        else:
            z, xbc = _norm_proj(h, wts["norm_w"][i], lw["in_w"], MAMBA_SEGS, MAMBA_OUTS)
            (dtp,) = _norm_proj(h, wts["norm_w"][i], lw["dt_w"], MAMBA_DT_SEGS, MAMBA_DT_OUTS)
            y, h_fin = _ssd(z, xbc, dtp, lw, consts, ssm0, conv0, i, nbatch=nbatch,
                            rows_in=seq, nchunks=1, y_dtype=F32)
            new_conv.append(xbc.reshape(nbatch, seq, CONV_DIM)[:, seq - (CONV_K - 1):])
            h = _out_ple([y], "mamba", h, p_all, i, lw["out_w"], *ple)
        new_ssm.append(h_fin.reshape(nbatch, SSM_HEADS, SSM_HEADDIM, D_STATE))
    cos, sin = rope_tabs
    kv_outs = _norm_proj(h, wts["kv_norm_w"], wts["kv_w"],
                         KV_SEGS_RES if prompt else KV_SEGS_TOK,
                         KV_OUTS_RES if prompt else KV_OUTS_TOK,
                         rope_inputs=(wts["k_norm_w"], cos, sin, consts["bd"]), rope_rows=rope_rows,
                         nbatch=nbatch)
    k, v = kv_outs[:2]
    for j in range(DEPTH - N_A_LAYERS):
        i = N_A_LAYERS + j
        aw = wts["attn"][j]
        q_outs = _norm_proj(h, wts["norm_w"][i], aw["in_w"],
                            ATTN_SEGS_RES if prompt else ATTN_SEGS_TOK,
                            ATTN_OUTS_RES if prompt else ATTN_OUTS_TOK,
                            rope_inputs=(aw["q_norm_w"], cos, sin, consts["bd"]), rope_rows=rope_rows,
                            nbatch=nbatch)
        gate = q_outs[-1]
        if prompt:
            os_, ls_ = [], []
            for gi in range(N_DGROUPS):
                o_g, l_g = _attn_prompt_group(q_outs[gi], kv_outs[2 + gi], kv_outs[2 + N_DGROUPS + gi], gi)
                os_.append(o_g)
                ls_.append(l_g)
            mix_inputs, mode = os_ + ls_ + [gate, consts["lse_spread"]], "attn_merge"
        else:
            o = _attn_sample(q_outs[0], k, v, caches, nbatch, seq)
            mix_inputs, mode = [o, gate], "attn"
        h = _out_ple(mix_inputs, mode, h, p_all, i, aw["out_w"], wts["ple_norm_w"][i],
                     wts["ple_gate_w"][i], wts["ple_w"][i])
    return h, jnp.stack(new_ssm, axis=0), jnp.stack(new_conv, axis=0), k, v


def kernel(x_prompt, x_sample, state_ssm, state_conv, cache_kv_g1, cache_kv_g2, cache_kv_g3,
           p_prompt, p_sample, norm_w, m_in_w, m_conv_w, m_conv_b, m_dt_bias, m_A_log, m_D,
           m_norm_w, m_out_w, kv_norm_w, kv_w, k_norm_w, a_in_w, a_q_norm_w, a_out_w,
           ple_w, ple_gate_w, ple_norm_w):
    wts = _prep_weights(norm_w, m_in_w, m_conv_w, m_conv_b, m_dt_bias, m_A_log, m_D, m_norm_w,
                        m_out_w, kv_norm_w, kv_w, k_norm_w, a_in_w, a_q_norm_w, a_out_w, ple_w,
                        ple_gate_w, ple_norm_w)
    consts = _constants()

    b_p, seq = x_prompt.shape[0], x_prompt.shape[1]
    m_p = b_p * seq
    tabs_p = _rope_tables(jnp.arange(seq, dtype=jnp.int32))
    y_p, ssm_p, conv_p, k_p, v_p = _trunk(
        x_prompt.reshape(m_p, D_MODEL), p_prompt.reshape(DEPTH, m_p, PLE_DIM), wts, consts,
        tabs_p, seq, nbatch=b_p, seq=seq, ssm0=None, conv0=None, caches=None)

    b_s, dec = x_sample.shape[0], x_sample.shape[1]
    m_s = b_s * dec
    pos_s = PAST_LEN + jnp.arange(dec, dtype=jnp.int32)
    tabs_s = tuple(jnp.tile(t, (b_s, 1)) for t in _rope_tables(pos_s))
    ssm0_s = state_ssm.reshape(N_A_LAYERS, b_s, D_INNER, D_STATE)
    conv0_s = jnp.pad(state_conv, ((0, 0), (0, 0), (SUBLANES - (CONV_K - 1), 0), (0, 0)))
    caches = tuple(jnp.transpose(cch, (0, 2, 3, 4, 1)).reshape(b_s, 2 * GROUP_KV, cch.shape[1])
                   for cch in (cache_kv_g1, cache_kv_g2, cache_kv_g3))
    y_s, ssm_s, conv_s, k_s, v_s = _trunk(
        x_sample.reshape(m_s, D_MODEL), p_sample.reshape(DEPTH, m_s, PLE_DIM), wts, consts,
        tabs_s, m_s, nbatch=b_s, seq=dec, ssm0=ssm0_s, conv0=conv0_s, caches=caches)

    def kv_out(k, v, nbatch, length, gi, keep):
        def tail(a):
            a = a.reshape(nbatch, length, KV_HALF)[:, length - keep:, gi * GROUP_KV:(gi + 1) * GROUP_KV]
            return a.reshape(nbatch, keep, ATT_KV_HEADS, ATT_HEAD_DIM)
        return jnp.stack([tail(k), tail(v)], axis=2)

    kv_p = [kv_out(k_p, v_p, b_p, seq, gi, min(w, seq)) for gi, (w, _) in enumerate(DILATION_GROUPS)]
    kv_s = [kv_out(k_s, v_s, b_s, dec, gi, dec) for gi in range(N_DGROUPS)]
    return (y_p.reshape(b_p, seq, D_MODEL), y_s.reshape(b_s, dec, D_MODEL),
            ssm_p, conv_p, ssm_s, conv_s, kv_p[0], kv_p[1], kv_p[2], kv_s[0], kv_s[1], kv_s[2])
```

```python
import functools

import jax
import jax.numpy as jnp
from jax import lax
from jax.experimental import pallas as pl
from jax.experimental.pallas import tpu as pltpu

F32 = jnp.float32
BF16 = jnp.bfloat16

D_MODEL = 1024
SEQ = 8192
DEPTH = 4
PAST_LEN = 8192
N_A_LAYERS = DEPTH // 2
D_INNER = 2048
SSM_HEADDIM = 64
SSM_HEADS = 32
SSM_GROUPS = 4
D_STATE = 128
CONV_K = 4
BC_DIM = 2 * SSM_GROUPS * D_STATE
CONV_DIM = D_INNER + BC_DIM
SSD_CHUNK = 128
ATT_HEAD_DIM = 64
ATT_HEADS = 16
ATT_KV_HEADS = 4
DILATION_GROUPS = ((128, 1), (512, 4), (2048, 16))
N_DGROUPS = 3
ATT_WIDTH = 1024
KV_HALF = N_DGROUPS * ATT_KV_HEADS * ATT_HEAD_DIM
GROUP_KV = ATT_KV_HEADS * ATT_HEAD_DIM
N_KEYS = 129
ROPE_THETA = 10000.0
PLE_DIM = 256
EPS = 1e-6
LOG2E = 1.4426950408889634
LN2 = 0.6931471805599453

LANES = 128
SUBLANES = 8
VMEM_LIMIT_BYTES = 56 * 1024 * 1024
ROW_TILE = 512
NEG_BIG = -1e30
RELAYOUT_SLOTS = 4
ATTN_Q_BLOCKS = 4
SAMPLE_SEQS_PER_STEP = 2
MAMBA_ROWS = 256
MAMBA_PROJ_PIECE = 256

HEAD_ORDER = (0, 4, 1, 5, 2, 6, 3, 7, 8, 12, 9, 13, 10, 14, 11, 15)


def _cparams(semantics):
    return pltpu.CompilerParams(dimension_semantics=semantics,
                                vmem_limit_bytes=VMEM_LIMIT_BYTES)


def _const_spec(shape):
    nd = len(shape)
    return pl.BlockSpec(shape, lambda *_: (0,) * nd, pipeline_mode=pl.Buffered(1))


def _split_bf16(x, parts):
    out = []
    rem = x
    for _ in range(parts):
        hi = rem.astype(BF16)
        out.append(hi)
        rem = rem - hi.astype(F32)
    return out


def _dot(a, b):
    return jnp.dot(a, b, preferred_element_type=F32)


def _dot_nt(a, b):
    return lax.dot_general(a, b, (((1,), (1,)), ((), ())), preferred_element_type=F32)


def _sigmoid(x):
    return 0.5 + 0.5 * jnp.tanh(0.5 * x)


def _silu(x):
    hx = 0.5 * x
    return hx + hx * jnp.tanh(hx)


def _lo_mask(rows):
    return lax.broadcasted_iota(jnp.int32, (rows, LANES), 1) < ATT_HEAD_DIM


def _rope_partner(v):
    lane = lax.broadcasted_iota(jnp.int32, v.shape, 1)
    return jnp.where((lane & 32) == 0, pltpu.roll(v, 96, 1), pltpu.roll(v, 32, 1))


def _head_norm_rope(y, ms, hw, cos, sin, scale):
    yn = y * lax.rsqrt(ms + EPS) * (hw * scale)
    return yn * cos + _rope_partner(yn) * sin


def _norm_proj_kernel(*refs, segs, use_rope, n_out, tm):
    h_ref, nw_ref, w_ref = refs[:3]
    pos = 3
    if use_rope:
        hw_ref, cos_ref, sin_ref, bd_ref = refs[3:7]
        pos = 7
    out_refs = refs[pos:pos + n_out]
    scr = refs[pos + n_out] if len(refs) > pos + n_out else None
    x = h_ref[...]
    ms = jnp.mean(x * x, axis=-1, keepdims=True)
    u = (x * lax.rsqrt(ms + EPS) * nw_ref[...]).astype(BF16)
    slot = 0
    for start, width, rope, scale, sinks in segs:
        acc = _dot(u, w_ref[:, start:start + width])
        if not rope and all(dil == 0 for _, _, dil in sinks):
            for oi, col_off, _ in sinks:
                out_refs[oi][:, col_off:col_off + width] = acc.astype(out_refs[oi].dtype)
            continue
        for cb in range(width // LANES):
            val = acc[:, cb * LANES:(cb + 1) * LANES]
            if rope:
                if cb % 2 == 0:
                    y2 = acc[:, cb * LANES:(cb + 2) * LANES]
                    ms2 = _dot((y2 * y2).astype(BF16), bd_ref[...])
                val = _head_norm_rope(val, ms2[:, (cb % 2) * LANES:(cb % 2 + 1) * LANES],
                                      hw_ref[...], cos_ref[...], sin_ref[...], scale)
            for oi, col_off, dil in sinks:
                o_ref = out_refs[oi]
                cs = slice(col_off + cb * LANES, col_off + (cb + 1) * LANES)
                if dil == 0:
                    o_ref[:, cs] = val.astype(o_ref.dtype)
                elif dil == 1:
                    o_ref[0, :, cs] = val.astype(o_ref.dtype)
                else:
                    s = slot % RELAYOUT_SLOTS
                    slot += 1
                    scr[s] = val
                    for r in range(dil):
                        o_ref[r, :, cs] = scr[s, pl.ds(r, tm // dil, stride=dil), :].astype(o_ref.dtype)


def _norm_proj(h, nw, w, segs, out_defs, rope_inputs=None, rope_rows=None, nbatch=None):
    m = h.shape[0]
    tm = min(ROW_TILE, m)
    n = w.shape[1]
    use_rope = rope_inputs is not None
    in_specs = [pl.BlockSpec((tm, D_MODEL), lambda i: (i, 0)),
                _const_spec((1, D_MODEL)),
                _const_spec((D_MODEL, n))]
    args = [h, nw, w]
    if use_rope:
        hw, cos, sin, bd = rope_inputs
        nblk = rope_rows // tm
        in_specs += [_const_spec((1, LANES)),
                     pl.BlockSpec((tm, LANES), lambda i: (i % nblk, 0)),
                     pl.BlockSpec((tm, LANES), lambda i: (i % nblk, 0)),
                     _const_spec((2 * LANES, 2 * LANES))]
        args += [hw, cos, sin, bd]
    out_shape, out_specs = [], []
    for width, dt, dil in out_defs:
        if dil == 0:
            out_shape.append(jax.ShapeDtypeStruct((m, width), dt))
            out_specs.append(pl.BlockSpec((tm, width), lambda i: (i, 0)))
        else:
            tiles = m // nbatch // tm
            out_shape.append(jax.ShapeDtypeStruct((nbatch, dil, m // nbatch // dil, width), dt))
            out_specs.append(pl.BlockSpec((None, dil, tm // dil, width),
                                          lambda i, tiles=tiles: (i // tiles, 0, i % tiles, 0)))
    scratch = []
    if any(dil > 1 for _, _, dil in out_defs):
        scratch.append(pltpu.VMEM((RELAYOUT_SLOTS, tm, LANES), F32))
    return pl.pallas_call(
        functools.partial(_norm_proj_kernel, segs=segs, use_rope=use_rope, n_out=len(out_defs), tm=tm),
        grid=(m // tm,),
        in_specs=in_specs,
        out_specs=tuple(out_specs),
        out_shape=tuple(out_shape),
        scratch_shapes=scratch,
        compiler_params=_cparams(("parallel",)),
        name="norm_proj",
    )(*args)


def _ssd_kernel(z_ref, xbc_ref, dt_ref, cw_ref, cb_ref, dtb_ref, alog_ref, dexp_ref, nw_ref,
                tril_ref, e_ref, h0_ref, c0_ref, y_ref, hout_ref, ht_scr, xpad_scr,
                *, rows_in, nchunks):
    q = SSD_CHUNK
    c = pl.program_id(1)

    @pl.when(c == 0)
    def _():
        ht_scr[...] = jnp.transpose(h0_ref[...])
        xpad_scr[...] = c0_ref[...]

    def pad_rows(v):
        if rows_in == q:
            return v
        return jnp.concatenate([v, jnp.zeros((q - rows_in, v.shape[1]), v.dtype)], axis=0)

    y = _ssd_chunk(pad_rows(z_ref[...]), pad_rows(xbc_ref[...]), pad_rows(dt_ref[...]),
                   cw_ref, cb_ref, dtb_ref, alog_ref, dexp_ref, nw_ref, tril_ref, e_ref,
                   ht_scr, xpad_scr, rows_in)
    y_ref[...] = y[:rows_in].astype(y_ref.dtype)

    @pl.when(c == nchunks - 1)
    def _():
        hout_ref[...] = jnp.transpose(ht_scr[...])


def _ssd_chunk(zz, xbc, dt_raw, cw_ref, cb_ref, dtb_ref, alog_ref, dexp_ref, nw_ref, tril_ref, e_ref,
               ht_scr, xpad_scr, rows_valid, side=None):
    q = SSD_CHUNK
    gw = D_INNER // SSM_GROUPS

    def run_side(n=1):
        for _ in range(n):
            if side:
                side.pop(0)()

    tail = xpad_scr[...]
    xbc3 = xbc.reshape(q // SUBLANES, SUBLANES, CONV_DIM)
    row8 = lax.broadcasted_iota(jnp.int32, (1, SUBLANES, CONV_DIM), 1)
    conv = cb_ref[...] + cw_ref[CONV_K - 1:CONV_K, :] * xbc
    for k in range(CONV_K - 1):
        sh = CONV_K - 1 - k
        rot3 = pltpu.roll(xbc3, sh, 1)
        prev3 = jnp.concatenate([pltpu.roll(tail, sh, 0)[None], rot3[:-1]], axis=0)
        shifted = jnp.where(row8 < sh, prev3, rot3).reshape(q, CONV_DIM)
        conv = conv + cw_ref[k:k + 1, :] * shifted
    xpad_scr[...] = xbc[q - SUBLANES:q]
    run_side()
    act = _silu(conv)
    xs = act[:, :D_INNER]
    bm = act[:, D_INNER:D_INNER + SSM_GROUPS * D_STATE]
    cm = act[:, D_INNER + SSM_GROUPS * D_STATE:]

    dtr = dt_raw + dtb_ref[...]
    dt = jnp.maximum(dtr, 0.0) + jnp.log1p(jnp.exp(-jnp.abs(dtr)))
    if rows_valid < q:
        row = lax.broadcasted_iota(jnp.int32, dt.shape, 0)
        dt = jnp.where(row < rows_valid, dt, 0.0)
    a = dt * (-LOG2E * jnp.exp(alog_ref[...]))
    tril = tril_ref[...]
    acum = sum(_dot(tril, part) for part in _split_bf16(a, 3))
    acum_t = jnp.transpose(acum)

    e = e_ref[...]
    grp = lax.broadcasted_iota(jnp.int32, (q, LANES), 1) >> 5
    zero_l = jnp.zeros((q, LANES), BF16)

    def packed(parts):
        out = zero_l
        for idx, part in enumerate(parts):
            out = jnp.where(grp == idx, part, out)
        return out

    dt_e = _dot(packed(_split_bf16(dt, 2)), e)
    acum_e = _dot(packed(_split_bf16(acum, 3)), e)
    alast_e = acum_e[q - 1:q, :]
    exp_acum_e = jnp.exp2(acum_e)
    decay_end_e = jnp.exp2(alast_e - acum_e)
    chunk_decay_e = jnp.exp2(alast_e)

    xdt = xs * dt_e
    xdt_bf = xdt.astype(BF16)
    xdtw_bf = (xdt * decay_end_e).astype(BF16)

    li = lax.broadcasted_iota(jnp.int32, (q, q), 0)
    si = lax.broadcasted_iota(jnp.int32, (q, q), 1)
    causal = li >= si
    lo = _lo_mask(q)
    zero_bf = jnp.zeros((q, LANES), BF16)

    y_parts = []
    for g in range(SSM_GROUPS):
        gs = slice(g * gw, (g + 1) * gw)
        bg = bm[:, g * D_STATE:(g + 1) * D_STATE]
        cg_bf = cm[:, g * D_STATE:(g + 1) * D_STATE].astype(BF16)
        cb = jnp.where(causal, _dot_nt(cg_bf, bg.astype(BF16)), 0.0)
        htg = ht_scr[:, gs]
        y_off = _dot(cg_bf, htg.astype(BF16)) * exp_acum_e[:, gs]
        blocks = []
        for j in range(gw // LANES):
            hd = g * (SSM_HEADS // SSM_GROUPS) + 2 * j
            xpair = xdt_bf[:, hd * SSM_HEADDIM:hd * SSM_HEADDIM + LANES]
            mats = []
            for hh in (hd, hd + 1):
                seg = acum[:, hh:hh + 1] - acum_t[hh:hh + 1, :]
                mats.append((cb * jnp.exp2(jnp.minimum(seg, 0.0))).astype(BF16))
            xstack = jnp.concatenate([jnp.where(lo, xpair, zero_bf), jnp.where(lo, zero_bf, xpair)], axis=0)
            blocks.append(_dot(jnp.concatenate(mats, axis=1), xstack))
            run_side()
        y_diag = jnp.concatenate(blocks, axis=1)
        bgt_bf = jnp.transpose(bg).astype(BF16)
        st = _dot(bgt_bf, xdtw_bf[:, gs])
        ht_scr[:, gs] = htg * chunk_decay_e[:, gs] + st
        y_parts.append(y_diag + y_off)

    y = jnp.concatenate(y_parts, axis=1) + xs * dexp_ref[...]
    y = y * _silu(zz)
    run_side()
    normed = []
    for g in range(SSM_GROUPS):
        yg = y[:, g * gw:(g + 1) * gw]
        ms = jnp.mean(yg * yg, axis=-1, keepdims=True)
        normed.append(yg * lax.rsqrt(ms + EPS))
    return jnp.concatenate(normed, axis=1) * nw_ref[...]


def _ssd(z, xbc, dtp, lw, consts, h0, c0, layer, *, nbatch, rows_in, nchunks, y_dtype):
    m = z.shape[0]
    row_map = lambda b, c: (b * nchunks + c, 0)
    batch_map = lambda b, c: (b, 0, 0)
    state_map = lambda b, c: (layer, b, 0, 0)
    in_specs = [
        pl.BlockSpec((rows_in, D_INNER), row_map),
        pl.BlockSpec((rows_in, CONV_DIM), row_map),
        pl.BlockSpec((rows_in, LANES), row_map),
        _const_spec((CONV_K, CONV_DIM)),
        _const_spec((1, CONV_DIM)),
        _const_spec((1, LANES)),
        _const_spec((1, LANES)),
        _const_spec((1, D_INNER)),
        _const_spec((1, D_INNER)),
        _const_spec((SSD_CHUNK, SSD_CHUNK)),
        _const_spec((LANES, D_INNER)),
        pl.BlockSpec((None, None, D_INNER, D_STATE), state_map),
        pl.BlockSpec((None, None, SUBLANES, CONV_DIM), state_map),
    ]
    out_specs = (pl.BlockSpec((rows_in, D_INNER), row_map),
                 pl.BlockSpec((None, D_INNER, D_STATE), batch_map))
    out_shape = (jax.ShapeDtypeStruct((m, D_INNER), y_dtype),
                 jax.ShapeDtypeStruct((nbatch, D_INNER, D_STATE), F32))
    return pl.pallas_call(
        functools.partial(_ssd_kernel, rows_in=rows_in, nchunks=nchunks),
        grid=(nbatch, nchunks),
        in_specs=in_specs,
        out_specs=out_specs,
        out_shape=out_shape,
        scratch_shapes=[pltpu.VMEM((D_STATE, D_INNER), F32),
                        pltpu.VMEM((SUBLANES, CONV_DIM), F32)],
        compiler_params=_cparams(("parallel", "arbitrary")),
        name="ssd",
    )(z, xbc, dtp, lw["conv_w"], lw["conv_b"], lw["dt_bias"], lw["a_log"], lw["d_exp"],
      lw["norm_w"], consts["tril"], consts["expand"], h0, c0)


def _mamba_layer_kernel(ha_ref, hc_ref, p_ref, nw_ref, win_ref, wdt_ref, cw_ref, cb_ref, dtb_ref, alog_ref,
                        dexp_ref, mnw_ref, tril_ref, e_ref, wout_ref, pnw_ref, gw_ref, pw_ref,
                        out_ref, sout_ref, cout_ref,
                        proj0, proj1, y0, y1, ht_scr, xpad_scr, *, nsteps, rows):
    s = pl.program_id(0)
    q = SSD_CHUNK

    @pl.when(s == 0)
    def _():
        proj1[...] = jnp.zeros(proj1.shape, proj1.dtype)
        y0[...] = jnp.zeros(y0.shape, y0.dtype)
        y1[...] = jnp.zeros(y1.shape, y1.dtype)

    @pl.when((s == 0) | (s % nsteps == 1))
    def _():
        ht_scr[...] = jnp.zeros(ht_scr.shape, ht_scr.dtype)
        xpad_scr[...] = jnp.zeros(xpad_scr.shape, xpad_scr.dtype)

    def stages(proj_w, proj_r, y_w, y_r):
        vals = {}

        def a_norm():
            x = ha_ref[...]
            ms = jnp.mean(x * x, axis=-1, keepdims=True)
            vals["u"] = (x * lax.rsqrt(ms + EPS) * nw_ref[...]).astype(BF16)

        def a_piece(c0, c1):
            def emit():
                proj_w[:, c0:c1] = _dot(vals["u"], win_ref[:, c0:c1])
            return emit

        def c_out(c0, c1):
            def emit():
                vals["h1", c0] = hc_ref[:, c0:c1] + _dot(y_r[...], wout_ref[:, c0:c1])
            return emit

        def c_norm():
            h1 = jnp.concatenate([vals["h1", c0] for c0 in c_cols], axis=1)
            vals["h1"] = h1
            ms = jnp.mean(h1 * h1, axis=-1, keepdims=True)
            vals["un"] = (h1 * lax.rsqrt(ms + EPS) * pnw_ref[...]).astype(BF16)

        def c_gate(c0, c1):
            def emit():
                gate = _sigmoid(_dot(vals["un"], gw_ref[:, c0:c1]))
                pe = _dot(p_ref[...].astype(BF16), pw_ref[:, c0:c1])
                out_ref[:, c0:c1] = vals["h1"][:, c0:c1] + gate * pe
            return emit

        def a_dt():
            proj_w[:, n_main:] = _dot(vals["u"], wdt_ref[...])

        n_main = D_INNER + CONV_DIM
        step = MAMBA_PROJ_PIECE
        a_work = [a_piece(c0, c0 + step) for c0 in range(0, n_main, step)] + [a_dt]
        c_cols = list(range(0, D_MODEL, step))
        c_work = ([c_out(c0, c0 + step) for c0 in c_cols] + [c_norm]
                  + [c_gate(c0, c0 + step) for c0 in c_cols])
        side = [a_norm]
        while a_work or c_work:
            if c_work:
                side.append(c_work.pop(0))
            if a_work:
                side.append(a_work.pop(0))
            if a_work:
                side.append(a_work.pop(0))
        for j in range(rows // q):
            rs = slice(j * q, (j + 1) * q)
            y = _ssd_chunk(proj_r[rs, 0:D_INNER], proj_r[rs, D_INNER:D_INNER + CONV_DIM],
                           proj_r[rs, D_INNER + CONV_DIM:], cw_ref, cb_ref, dtb_ref, alog_ref,
                           dexp_ref, mnw_ref, tril_ref, e_ref, ht_scr, xpad_scr, q, side=side)
            y_w[rs, :] = y.astype(y_w.dtype)
        while side:
            side.pop(0)()

    @pl.when(s % 2 == 0)
    def _():
        stages(proj0, proj1, y1, y0)

    @pl.when(s % 2 == 1)
    def _():
        stages(proj1, proj0, y0, y1)

    @pl.when((s > 0) & (s % nsteps == 0))
    def _():
        sout_ref[...] = jnp.transpose(ht_scr[...])
        cout_ref[...] = xpad_scr[...]


def _mamba_layer(h, p_all, layer, nw, lw, consts, pnw, gw, pw, *, nbatch, seq):
    m = h.shape[0]
    rows = MAMBA_ROWS
    nsteps = seq // rows
    n_proj = D_INNER + CONV_DIM + LANES
    total = nbatch * nsteps
    a_map = lambda s: (jnp.minimum(s, total - 1), 0)
    c_map = lambda s: (jnp.clip(s - 2, 0, total - 1), 0)
    seq_map = lambda s: (jnp.clip((s - 1) // nsteps, 0, nbatch - 1), 0, 0)
    in_specs = [
        pl.BlockSpec((rows, D_MODEL), a_map),
        pl.BlockSpec((rows, D_MODEL), c_map),
        pl.BlockSpec((None, rows, PLE_DIM), lambda s: (layer, jnp.clip(s - 2, 0, total - 1), 0)),
        _const_spec((1, D_MODEL)),
        _const_spec(lw["in_w"].shape),
        _const_spec((D_MODEL, LANES)),
        _const_spec((CONV_K, CONV_DIM)),
        _const_spec((1, CONV_DIM)),
        _const_spec((1, LANES)),
        _const_spec((1, LANES)),
        _const_spec((1, D_INNER)),
        _const_spec((1, D_INNER)),
        _const_spec((SSD_CHUNK, SSD_CHUNK)),
        _const_spec((LANES, D_INNER)),
        _const_spec((D_INNER, D_MODEL)),
        _const_spec((1, D_MODEL)),
        _const_spec((D_MODEL, D_MODEL)),
        _const_spec((PLE_DIM, D_MODEL)),
    ]
    out_specs = (pl.BlockSpec((rows, D_MODEL), c_map),
                 pl.BlockSpec((None, D_INNER, D_STATE), seq_map),
                 pl.BlockSpec((None, SUBLANES, CONV_DIM), seq_map))
    out_shape = (jax.ShapeDtypeStruct((m, D_MODEL), F32),
                 jax.ShapeDtypeStruct((nbatch, D_INNER, D_STATE), F32),
                 jax.ShapeDtypeStruct((nbatch, SUBLANES, CONV_DIM), F32))
    return pl.pallas_call(
        functools.partial(_mamba_layer_kernel, nsteps=nsteps, rows=rows),
        grid=(total + 2,),
        in_specs=in_specs,
        out_specs=out_specs,
        out_shape=out_shape,
        scratch_shapes=[pltpu.VMEM((rows, n_proj), F32), pltpu.VMEM((rows, n_proj), F32),
                        pltpu.VMEM((rows, D_INNER), BF16), pltpu.VMEM((rows, D_INNER), BF16),
                        pltpu.VMEM((D_STATE, D_INNER), F32),
                        pltpu.VMEM((SUBLANES, CONV_DIM), F32)],
        compiler_params=_cparams(("arbitrary",)),
        name="mamba_layer",
    )(h, h, p_all, nw, lw["in_w"], lw["dt_w"], lw["conv_w"], lw["conv_b"], lw["dt_bias"], lw["a_log"],
      lw["d_exp"], lw["norm_w"], consts["tril"], consts["expand"], lw["out_w"], pnw, gw, pw)


def _attn_prompt_kernel(q_ref, kcur_ref, kprev_ref, vcur_ref, vprev_ref, o_ref, l_ref, *, nq):
    i = pl.program_id(2)
    t = SSD_CHUNK
    npair = GROUP_KV // ATT_HEAD_DIM
    qi = lax.broadcasted_iota(jnp.int32, (t, 2 * t), 0)
    kk = lax.broadcasted_iota(jnp.int32, (t, 2 * t), 1)
    band = (kk >= qi) & (kk <= qi + (N_KEYS - 1))
    bias_inner = jnp.where(band, 0.0, NEG_BIG)
    bias_first = jnp.where(band & ((kk >= t) | (i > 0)), 0.0, NEG_BIG)
    lo = _lo_mask(t)
    lo2 = _lo_mask(2 * t)
    zero_v = jnp.zeros((2 * t, LANES), BF16)
    ones_stack = jnp.concatenate([jnp.where(lo2, 1.0, 0.0), jnp.where(lo2, 0.0, 1.0)],
                                 axis=0).astype(BF16)
    lane = lax.broadcasted_iota(jnp.int32, (t, LANES), 1)
    lane_slot = (lane & (ATT_HEAD_DIM - 1)) >> 3
    for s in range(nq):
        rows = slice(s * t, (s + 1) * t)
        bias = bias_first if s == 0 else bias_inner
        l_c = jnp.zeros((t, LANES), F32)
        for kc in range(GROUP_KV // LANES):
            ks = slice(kc * LANES, (kc + 1) * LANES)
            if s == 0:
                k2 = jnp.concatenate([kprev_ref[:, ks], kcur_ref[0:t, ks]], axis=0)
                v2 = jnp.concatenate([vprev_ref[:, ks], vcur_ref[0:t, ks]], axis=0)
            else:
                k2 = kcur_ref[(s - 1) * t:(s + 1) * t, ks]
                v2 = vcur_ref[(s - 1) * t:(s + 1) * t, ks]
            k_stack = jnp.concatenate([jnp.where(lo2, k2, zero_v), jnp.where(lo2, zero_v, k2)], axis=0)
            v_stack = jnp.concatenate([jnp.where(lo2, v2, zero_v), jnp.where(lo2, zero_v, v2)], axis=0)
            rhs = jnp.concatenate([v_stack, ones_stack], axis=1)
            lhs = jnp.concatenate(
                [q_ref[rows, (kc * npair + pb) * LANES:(kc * npair + pb + 1) * LANES] for pb in range(npair)],
                axis=0)
            sc = _dot_nt(lhs, k_stack).reshape(npair, t, 4 * t)
            sc_a = sc[:, :, :2 * t] + bias[None]
            sc_b = sc[:, :, 2 * t:] + bias[None]
            mx_a = jnp.max(sc_a, axis=-1, keepdims=True)
            mx_b = jnp.max(sc_b, axis=-1, keepdims=True)
            p_cat = jnp.concatenate([jnp.exp2(sc_a - mx_a), jnp.exp2(sc_b - mx_b)], axis=-1)
            res = _dot(p_cat.astype(BF16).reshape(npair * t, 4 * t), rhs)
            for pb in range(npair):
                blk = kc * npair + pb
                num = res[pb * t:(pb + 1) * t, :LANES]
                den = res[pb * t:(pb + 1) * t, LANES:]
                o_ref[rows, blk * LANES:(blk + 1) * LANES] = (num / den).astype(o_ref.dtype)
                lse = jnp.where(lo, mx_a[pb], mx_b[pb]) * LN2 + jnp.log(den)
                l_c = jnp.where(lane_slot == blk, lse, l_c)
        l_ref[rows, :] = l_c


def _attn_prompt_group(q, k, v, gi):
    nbatch, dil, rows, _ = q.shape
    t = SSD_CHUNK
    nq = ATTN_Q_BLOCKS
    cur = lambda b, r, i: (b, r, i, 0)
    prev = lambda b, r, i: (b, r, jnp.maximum(nq * i - 1, 0), 0)
    return pl.pallas_call(
        functools.partial(_attn_prompt_kernel, nq=nq),
        grid=(nbatch, dil, rows // (nq * t)),
        in_specs=[pl.BlockSpec((None, None, nq * t, ATT_WIDTH), cur),
                  pl.BlockSpec((None, None, nq * t, GROUP_KV), cur),
                  pl.BlockSpec((None, None, t, GROUP_KV), prev),
                  pl.BlockSpec((None, None, nq * t, GROUP_KV), cur),
                  pl.BlockSpec((None, None, t, GROUP_KV), prev)],
        out_specs=(pl.BlockSpec((None, None, nq * t, ATT_WIDTH), cur),
                   pl.BlockSpec((None, None, nq * t, LANES), cur)),
        out_shape=(jax.ShapeDtypeStruct((nbatch, dil, rows, ATT_WIDTH), BF16),
                   jax.ShapeDtypeStruct((nbatch, dil, rows, LANES), F32)),
        compiler_params=_cparams(("parallel", "parallel", "arbitrary")),
        name=f"attn_prompt_g{gi}",
    )(q, k, k, v, v)


def _attn_sample_kernel(q_ref, kn_ref, vn_ref, c1_ref, c2_ref, c3_ref, o_ref, *, dec_seq, nseq):
    for bb in range(nseq):
        rs = slice(bb * dec_seq, (bb + 1) * dec_seq)
        _attn_sample_one(q_ref.at[rs], kn_ref.at[rs], vn_ref.at[rs],
                         (c1_ref.at[bb], c2_ref.at[bb], c3_ref.at[bb]), o_ref.at[rs], dec_seq)


def _attn_sample_one(q_ref, kn_ref, vn_ref, caches, o_ref, dec_seq):
    nrow = 8 * dec_seq
    lo_row = _lo_mask(dec_seq)
    pad = LANES - dec_seq
    outs = [[None] * N_DGROUPS for _ in range(ATT_WIDTH // LANES)]
    lses = [[None] * N_DGROUPS for _ in range(ATT_WIDTH // LANES)]
    for gi, (_, dil) in enumerate(DILATION_GROUPS):
        cref = caches[gi]
        w = cref.shape[1]
        tq = lax.broadcasted_iota(jnp.int32, (nrow, w), 0) & (dec_seq - 1)
        rho = lax.broadcasted_iota(jnp.int32, (nrow, w), 1)
        delta = w + tq - rho
        valid_c = (rho >= tq) & ((delta & (dil - 1)) == 0) & (delta <= (N_KEYS - 1) * dil)
        bias_c = jnp.where(valid_c, 0.0, NEG_BIG)
        tq_n = lax.broadcasted_iota(jnp.int32, (nrow, LANES), 0) & (dec_seq - 1)
        tn = lax.broadcasted_iota(jnp.int32, (nrow, LANES), 1)
        valid_n = (tn <= tq_n) & (((tq_n - tn) & (dil - 1)) == 0)
        bias_n = jnp.where(valid_n, 0.0, NEG_BIG)
        gsl = slice(gi * GROUP_KV, (gi + 1) * GROUP_KV)
        k_new = jnp.concatenate([kn_ref[:, gsl], jnp.zeros((pad, GROUP_KV), F32)], axis=0)
        v_new = jnp.concatenate([vn_ref[:, gsl], jnp.zeros((pad, GROUP_KV), F32)], axis=0)
        for kc in range(GROUP_KV // LANES):
            ks = slice(kc * LANES, (kc + 1) * LANES)
            rows = []
            for pb in range(4):
                blk = kc * 4 + pb
                q2 = q_ref[:, gi * ATT_WIDTH + blk * LANES:gi * ATT_WIDTH + (blk + 1) * LANES]
                rows.append(jnp.where(lo_row, q2, 0.0))
                rows.append(jnp.where(lo_row, 0.0, q2))
            lhs = jnp.concatenate(rows, axis=0).astype(BF16)
            kcache_t = cref[ks, :].astype(BF16)
            vcache_t = cref[GROUP_KV + kc * LANES:GROUP_KV + (kc + 1) * LANES, :].astype(BF16)
            s_c = _dot(lhs, kcache_t) + bias_c
            s_n = _dot_nt(lhs, k_new[:, ks].astype(BF16)) + bias_n
            mx = jnp.maximum(jnp.max(s_c, axis=-1, keepdims=True),
                             jnp.max(s_n, axis=-1, keepdims=True))
            p_c = jnp.exp(s_c - mx)
            p_n = jnp.exp(s_n - mx)
            den = jnp.sum(p_c, axis=-1, keepdims=True) + jnp.sum(p_n, axis=-1, keepdims=True)
            num = _dot_nt(p_c.astype(BF16), vcache_t) + _dot(p_n.astype(BF16), v_new[:, ks].astype(BF16))
            on = num / den
            lse = mx + jnp.log(den)
            for pb in range(4):
                blk = kc * 4 + pb
                r0 = pb * 2 * dec_seq
                r1 = r0 + dec_seq
                outs[blk][gi] = jnp.where(lo_row, on[r0:r1], on[r1:r1 + dec_seq])
                lses[blk][gi] = jnp.where(lo_row, lse[r0:r1], lse[r1:r1 + dec_seq])
    for blk in range(ATT_WIDTH // LANES):
        ls = lses[blk]
        mx = jnp.maximum(jnp.maximum(ls[0], ls[1]), ls[2])
        ws = [jnp.exp(l - mx) for l in ls]
        tot = ws[0] + ws[1] + ws[2]
        o = (outs[blk][0] * ws[0] + outs[blk][1] * ws[1] + outs[blk][2] * ws[2]) / tot
        o_ref[:, blk * LANES:(blk + 1) * LANES] = o


def _attn_sample(q, k, v, caches, nbatch, dec_seq):
    m = q.shape[0]
    nseq = SAMPLE_SEQS_PER_STEP
    rows = nseq * dec_seq
    row_map = lambda b: (b, 0)
    in_specs = [pl.BlockSpec((rows, N_DGROUPS * ATT_WIDTH), row_map),
                pl.BlockSpec((rows, KV_HALF), row_map),
                pl.BlockSpec((rows, KV_HALF), row_map)]
    for cch in caches:
        in_specs.append(pl.BlockSpec((nseq, 2 * GROUP_KV, cch.shape[2]), lambda b: (b, 0, 0)))
    return pl.pallas_call(
        functools.partial(_attn_sample_kernel, dec_seq=dec_seq, nseq=nseq),
        grid=(nbatch // nseq,),
        in_specs=in_specs,
        out_specs=pl.BlockSpec((rows, ATT_WIDTH), row_map),
        out_shape=jax.ShapeDtypeStruct((m, ATT_WIDTH), F32),
        compiler_params=_cparams(("parallel",)),
        name="attn_sample",
    )(q, k, v, *caches)


def _token_order(ref, scr, slot, cb, dil, tm):
    cs = slice(cb * LANES, (cb + 1) * LANES)
    if dil == 1:
        return ref[0, :, cs].astype(F32)
    for r in range(dil):
        scr[slot, pl.ds(r, tm // dil, stride=dil), :] = ref[r, :, cs].astype(F32)
    return scr[slot]


def _out_ple_kernel(*refs, mode, tm):
    if mode == "mamba":
        (y_ref,) = refs[:1]
        pos = 1
        mix = y_ref[...].astype(BF16)
    elif mode == "attn_merge":
        o_refs = refs[0:3]
        l_refs = refs[3:6]
        gate_ref, x_ref = refs[6:8]
        pos = 8
        scr, mix_scr = refs[-2:]
        refs = refs[:-2]
        dils = [d for _, d in DILATION_GROUPS]
        ls = [_token_order(l_refs[g], scr, g, 0, dils[g], tm) for g in range(N_DGROUPS)]
        mx = jnp.maximum(jnp.maximum(ls[0], ls[1]), ls[2])
        es = [jnp.exp(l - mx) for l in ls]
        tot = es[0] + es[1] + es[2]
        sub = lax.broadcasted_iota(jnp.int32, (tm, LANES), 1) & 7
        packed = jnp.zeros((tm, LANES), BF16)
        for g in range(N_DGROUPS):
            for idx, part in enumerate(_split_bf16(es[g] / tot, 2)):
                packed = jnp.where(sub == 2 * g + idx, part, packed)
        ws_all = _dot(packed, x_ref[...])
        ws = [ws_all[:, g * ATT_WIDTH:(g + 1) * ATT_WIDTH] for g in range(N_DGROUPS)]
        for cb in range(ATT_WIDTH // LANES):
            cs = slice(cb * LANES, (cb + 1) * LANES)
            o = None
            for g in range(N_DGROUPS):
                og = _token_order(o_refs[g], scr, N_DGROUPS + (cb * N_DGROUPS + g) % RELAYOUT_SLOTS,
                                  cb, dils[g], tm)
                term = og * ws[g][:, cs]
                o = term if o is None else o + term
            gate = gate_ref[:, cs]
            mix_scr[:, cs] = (o * _silu(gate)).astype(BF16)
        mix = mix_scr[...]
    else:
        o_ref_in, gate_ref = refs[:2]
        pos = 2
        gate = gate_ref[...]
        mix = (o_ref_in[...] * _silu(gate)).astype(BF16)
    h_ref, p_ref, wout_ref, pnw_ref, gw_ref, pw_ref, out_ref = refs[pos:]
    out_ref[...] = _residual_ple(mix, h_ref[...], p_ref[...], wout_ref, pnw_ref, gw_ref, pw_ref)


def _residual_ple(mix, h, p, wout_ref, pnw_ref, gw_ref, pw_ref):
    h1 = h + _dot(mix, wout_ref[...])
    ms = jnp.mean(h1 * h1, axis=-1, keepdims=True)
    un = (h1 * lax.rsqrt(ms + EPS) * pnw_ref[...]).astype(BF16)
    gate_p = _sigmoid(_dot(un, gw_ref[...]))
    pe = _dot(p.astype(BF16), pw_ref[...])
    return h1 + gate_p * pe


def _out_ple(mix_inputs, mode, h, p_all, layer, wout, pnw, gw, pw):
    m = h.shape[0]
    tm = min(ROW_TILE, m)
    row_map = lambda i: (i, 0)
    in_specs = []
    for a in mix_inputs:
        if a.ndim == 4:
            _, dil, rows, width = a.shape
            tiles = rows * dil // tm
            in_specs.append(pl.BlockSpec((None, dil, tm // dil, width),
                                         lambda i, tiles=tiles: (i // tiles, 0, i % tiles, 0)))
        elif a.shape[0] == m:
            in_specs.append(pl.BlockSpec((tm, a.shape[1]), row_map))
        else:
            in_specs.append(_const_spec(a.shape))
    scratch = []
    if mode == "attn_merge":
        scratch = [pltpu.VMEM((N_DGROUPS + RELAYOUT_SLOTS, tm, LANES), F32),
                   pltpu.VMEM((tm, ATT_WIDTH), BF16)]
    in_specs += [pl.BlockSpec((tm, D_MODEL), row_map),
                 pl.BlockSpec((None, tm, PLE_DIM), lambda i: (layer, i, 0)),
                 _const_spec(wout.shape),
                 _const_spec((1, D_MODEL)),
                 _const_spec((D_MODEL, D_MODEL)),
                 _const_spec((PLE_DIM, D_MODEL))]
    return pl.pallas_call(
        functools.partial(_out_ple_kernel, mode=mode, tm=tm),
        grid=(m // tm,),
        in_specs=in_specs,
        out_specs=pl.BlockSpec((tm, D_MODEL), row_map),
        out_shape=jax.ShapeDtypeStruct((m, D_MODEL), F32),
        scratch_shapes=scratch,
        compiler_params=_cparams(("parallel",)),
        name=f"out_ple_{mode}",
    )(*mix_inputs, h, p_all, wout, pnw, gw, pw)


def _rope_tables(pos):
    half = ATT_HEAD_DIM // 2
    inv = 1.0 / (ROPE_THETA ** (jnp.arange(half, dtype=F32) / half))
    ang = pos.astype(F32)[:, None] * inv[None, :]
    cos, sin = jnp.cos(ang), jnp.sin(ang)
    cos128 = jnp.concatenate([cos, cos, cos, cos], axis=1)
    sin128 = jnp.concatenate([-sin, sin, -sin, sin], axis=1)
    return cos128, sin128


def _permute_cols_kernel(w_ref, o_ref):
    lo = _lo_mask(w_ref.shape[0])
    for pos in range(ATT_HEADS // 2):
        ha, hb = HEAD_ORDER[2 * pos], HEAD_ORDER[2 * pos + 1]
        xa = w_ref[:, (ha // 2) * LANES:(ha // 2 + 1) * LANES]
        xb = w_ref[:, (hb // 2) * LANES:(hb // 2 + 1) * LANES]
        if ha % 2 == 1:
            xa = pltpu.roll(xa, ATT_HEAD_DIM, 1)
        if hb % 2 == 0:
            xb = pltpu.roll(xb, ATT_HEAD_DIM, 1)
        o_ref[:, pos * LANES:(pos + 1) * LANES] = jnp.where(lo, xa, xb).astype(o_ref.dtype)


def _permute_rows_kernel(w_ref, o_ref):
    for pos, head in enumerate(HEAD_ORDER):
        o_ref[pos * ATT_HEAD_DIM:(pos + 1) * ATT_HEAD_DIM, :] = (
            w_ref[head * ATT_HEAD_DIM:(head + 1) * ATT_HEAD_DIM, :].astype(o_ref.dtype))


def _permute_attn_weights(a_in_w, a_out_w):
    nl = a_in_w.shape[0]
    ngrp = a_in_w.shape[2] // ATT_WIDTH
    blk = (None, D_MODEL, ATT_WIDTH)
    in_w = pl.pallas_call(
        _permute_cols_kernel,
        grid=(nl, ngrp),
        in_specs=[pl.BlockSpec(blk, lambda j, g: (j, 0, g))],
        out_specs=pl.BlockSpec(blk, lambda j, g: (j, 0, g)),
        out_shape=jax.ShapeDtypeStruct(a_in_w.shape, BF16),
        compiler_params=_cparams(("parallel", "parallel")),
        name="permute_cols",
    )(a_in_w)
    out_w = pl.pallas_call(
        _permute_rows_kernel,
        grid=(nl,),
        in_specs=[pl.BlockSpec(blk, lambda j: (j, 0, 0))],
        out_specs=pl.BlockSpec(blk, lambda j: (j, 0, 0)),
        out_shape=jax.ShapeDtypeStruct(a_out_w.shape, BF16),
        compiler_params=_cparams(("parallel",)),
        name="permute_rows",
    )(a_out_w)
    return in_w, out_w


def _prep_weights(norm_w, m_in_w, m_conv_w, m_conv_b, m_dt_bias, m_A_log, m_D, m_norm_w, m_out_w,
                  kv_norm_w, kv_w, k_norm_w, a_in_w, a_q_norm_w, a_out_w, ple_w, ple_gate_w,
                  ple_norm_w):
    row = lambda v: v.reshape(1, -1).astype(F32)
    reps = LANES // SSM_HEADS
    lane_pad = lambda v: jnp.tile(v.astype(F32), reps).reshape(1, LANES)
    mamba = []
    for i in range(N_A_LAYERS):
        mamba.append(dict(
            in_w=m_in_w[i].astype(BF16),
            dt_w=jnp.tile(m_in_w[i][:, D_INNER + CONV_DIM:], (1, reps)).astype(BF16),
            conv_w=m_conv_w[i].astype(F32),
            conv_b=row(m_conv_b[i]),
            dt_bias=lane_pad(m_dt_bias[i]),
            a_log=lane_pad(m_A_log[i]),
            d_exp=row(jnp.repeat(m_D[i], SSM_HEADDIM)),
            norm_w=row(m_norm_w[i]),
            out_w=m_out_w[i].astype(BF16),
        ))
    attn = []
    a_in_bf, a_out_bf = _permute_attn_weights(a_in_w, a_out_w)
    for j in range(DEPTH - N_A_LAYERS):
        attn.append(dict(
            in_w=a_in_bf[j],
            q_norm_w=row(jnp.tile(a_q_norm_w[j], LANES // ATT_HEAD_DIM)),
            out_w=a_out_bf[j],
        ))
    return dict(
        norm_w=[row(norm_w[i]) for i in range(DEPTH)],
        mamba=mamba,
        attn=attn,
        kv_norm_w=row(kv_norm_w),
        kv_w=kv_w.astype(BF16),
        k_norm_w=row(jnp.tile(k_norm_w, LANES // ATT_HEAD_DIM)),
        ple_w=[ple_w[i].astype(BF16) for i in range(DEPTH)],
        ple_gate_w=[ple_gate_w[i].astype(BF16) for i in range(DEPTH)],
        ple_norm_w=[row(ple_norm_w[i]) for i in range(DEPTH)],
    )


def _constants():
    t = SSD_CHUNK
    tril = jnp.tril(jnp.ones((t, t), F32)).astype(BF16)
    head = jnp.arange(LANES, dtype=jnp.int32)[:, None]
    col = jnp.arange(D_INNER, dtype=jnp.int32)[None, :]
    expand = (head % SSM_HEADS == col // SSM_HEADDIM).astype(BF16)
    li = jnp.arange(LANES, dtype=jnp.int32)
    l2 = jnp.arange(2 * LANES, dtype=jnp.int32)
    bd = ((l2[:, None] // ATT_HEAD_DIM) == (l2[None, :] // ATT_HEAD_DIM)).astype(F32) / ATT_HEAD_DIM
    ccol = jnp.arange(N_DGROUPS * ATT_WIDTH, dtype=jnp.int32)[None, :]
    cgrp, cc = ccol // ATT_WIDTH, ccol % ATT_WIDTH
    slot = cc // LANES + jnp.where(cc % LANES < ATT_HEAD_DIM, 0, 8)
    lrow = li[:, None]
    lse_spread = ((lrow >> 3 == slot) & ((lrow & 7) >> 1 == cgrp)).astype(BF16)
    return dict(tril=tril, expand=expand, bd=bd.astype(BF16), lse_spread=lse_spread)


_Q_SCALE = ATT_HEAD_DIM ** -0.5
_DILS = tuple(d for _, d in DILATION_GROUPS)
MAMBA_SEGS = ((0, D_INNER, False, 1.0, ((0, 0, 0),)),
              (D_INNER, CONV_DIM, False, 1.0, ((1, 0, 0),)))
MAMBA_OUTS = ((D_INNER, F32, 0), (CONV_DIM, F32, 0))
MAMBA_DT_SEGS = ((0, LANES, False, 1.0, ((0, 0, 0),)),)
MAMBA_DT_OUTS = ((LANES, F32, 0),)
KV_SEGS_TOK = ((0, KV_HALF, True, 1.0, ((0, 0, 0),)), (KV_HALF, KV_HALF, False, 1.0, ((1, 0, 0),)))
KV_OUTS_TOK = ((KV_HALF, F32, 0), (KV_HALF, F32, 0))
ATTN_SEGS_TOK = ((0, N_DGROUPS * ATT_WIDTH, True, _Q_SCALE, ((0, 0, 0),)),
                 (N_DGROUPS * ATT_WIDTH, ATT_WIDTH, False, 1.0, ((1, 0, 0),)))
ATTN_OUTS_TOK = ((N_DGROUPS * ATT_WIDTH, F32, 0), (ATT_WIDTH, F32, 0))
KV_SEGS_RES = tuple(
    (g * GROUP_KV, GROUP_KV, True, 1.0, ((0, g * GROUP_KV, 0), (2 + g, 0, _DILS[g])))
    for g in range(N_DGROUPS)) + tuple(
    (KV_HALF + g * GROUP_KV, GROUP_KV, False, 1.0, ((1, g * GROUP_KV, 0), (2 + N_DGROUPS + g, 0, _DILS[g])))
    for g in range(N_DGROUPS))
KV_OUTS_RES = KV_OUTS_TOK + tuple((GROUP_KV, BF16, d) for d in _DILS) * 2
ATTN_SEGS_RES = tuple((g * ATT_WIDTH, ATT_WIDTH, True, _Q_SCALE * LOG2E, ((g, 0, _DILS[g]),))
                      for g in range(N_DGROUPS)) + (
    (N_DGROUPS * ATT_WIDTH, ATT_WIDTH, False, 1.0, ((N_DGROUPS, 0, 0),)),)
ATTN_OUTS_RES = tuple((ATT_WIDTH, BF16, d) for d in _DILS) + ((ATT_WIDTH, F32, 0),)


def _trunk(x, p_all, wts, consts, rope_tabs, rope_rows, *, nbatch, seq, ssm0, conv0, caches):
    prompt = caches is None
    m = x.shape[0]
    h = x
    new_ssm, new_conv = [], []
    for i in range(N_A_LAYERS):
        lw = wts["mamba"][i]
        ple = (wts["ple_norm_w"][i], wts["ple_gate_w"][i], wts["ple_w"][i])
        if prompt:
            h, h_fin, xbc_tail = _mamba_layer(h, p_all, i, wts["norm_w"][i], lw, consts, *ple,
                                              nbatch=nbatch, seq=seq)
            new_conv.append(xbc_tail[:, SUBLANES - (CONV_K - 1):])
        else:
            z, xbc = _norm_proj(h, wts["norm_w"][i], lw["in_w"], MAMBA_SEGS, MAMBA_OUTS)
            (dtp,) = _norm_proj(h, wts["norm_w"][i], lw["dt_w"], MAMBA_DT_SEGS, MAMBA_DT_OUTS)
            y, h_fin = _ssd(z, xbc, dtp, lw, consts, ssm0, conv0, i, nbatch=nbatch,
                            rows_in=seq, nchunks=1, y_dtype=F32)
            new_conv.append(xbc.reshape(nbatch, seq, CONV_DIM)[:, seq - (CONV_K - 1):])
            h = _out_ple([y], "mamba", h, p_all, i, lw["out_w"], *ple)
        new_ssm.append(h_fin.reshape(nbatch, SSM_HEADS, SSM_HEADDIM, D_STATE))
    cos, sin = rope_tabs
    kv_outs = _norm_proj(h, wts["kv_norm_w"], wts["kv_w"],
                         KV_SEGS_RES if prompt else KV_SEGS_TOK,
                         KV_OUTS_RES if prompt else KV_OUTS_TOK,
                         rope_inputs=(wts["k_norm_w"], cos, sin, consts["bd"]), rope_rows=rope_rows,
                         nbatch=nbatch)
    k, v = kv_outs[:2]
    for j in range(DEPTH - N_A_LAYERS):
        i = N_A_LAYERS + j
        aw = wts["attn"][j]
        q_outs = _norm_proj(h, wts["norm_w"][i], aw["in_w"],
                            ATTN_SEGS_RES if prompt else ATTN_SEGS_TOK,
                            ATTN_OUTS_RES if prompt else ATTN_OUTS_TOK,
                            rope_inputs=(aw["q_norm_w"], cos, sin, consts["bd"]), rope_rows=rope_rows,
                            nbatch=nbatch)
        gate = q_outs[-1]
        if prompt:
            os_, ls_ = [], []
            for gi in range(N_DGROUPS):
                o_g, l_g = _attn_prompt_group(q_outs[gi], kv_outs[2 + gi], kv_outs[2 + N_DGROUPS + gi], gi)
                os_.append(o_g)
                ls_.append(l_g)
            mix_inputs, mode = os_ + ls_ + [gate, consts["lse_spread"]], "attn_merge"
        else:
            o = _attn_sample(q_outs[0], k, v, caches, nbatch, seq)
            mix_inputs, mode = [o, gate], "attn"
        h = _out_ple(mix_inputs, mode, h, p_all, i, aw["out_w"], wts["ple_norm_w"][i],
                     wts["ple_gate_w"][i], wts["ple_w"][i])
    return h, jnp.stack(new_ssm, axis=0), jnp.stack(new_conv, axis=0), k, v


def kernel(x_prompt, x_sample, state_ssm, state_conv, cache_kv_g1, cache_kv_g2, cache_kv_g3,
           p_prompt, p_sample, norm_w, m_in_w, m_conv_w, m_conv_b, m_dt_bias, m_A_log, m_D,
           m_norm_w, m_out_w, kv_norm_w, kv_w, k_norm_w, a_in_w, a_q_norm_w, a_out_w,
           ple_w, ple_gate_w, ple_norm_w):
    wts = _prep_weights(norm_w, m_in_w, m_conv_w, m_conv_b, m_dt_bias, m_A_log, m_D, m_norm_w,
                        m_out_w, kv_norm_w, kv_w, k_norm_w, a_in_w, a_q_norm_w, a_out_w, ple_w,
                        ple_gate_w, ple_norm_w)
    consts = _constants()

    b_p, seq = x_prompt.shape[0], x_prompt.shape[1]
    m_p = b_p * seq
    tabs_p = _rope_tables(jnp.arange(seq, dtype=jnp.int32))
    y_p, ssm_p, conv_p, k_p, v_p = _trunk(
        x_prompt.reshape(m_p, D_MODEL), p_prompt.reshape(DEPTH, m_p, PLE_DIM), wts, consts,
        tabs_p, seq, nbatch=b_p, seq=seq, ssm0=None, conv0=None, caches=None)

    b_s, dec = x_sample.shape[0], x_sample.shape[1]
    m_s = b_s * dec
    pos_s = PAST_LEN + jnp.arange(dec, dtype=jnp.int32)
    tabs_s = tuple(jnp.tile(t, (b_s, 1)) for t in _rope_tables(pos_s))
    ssm0_s = state_ssm.reshape(N_A_LAYERS, b_s, D_INNER, D_STATE)
    conv0_s = jnp.pad(state_conv, ((0, 0), (0, 0), (SUBLANES - (CONV_K - 1), 0), (0, 0)))
    caches = tuple(jnp.transpose(cch, (0, 2, 3, 4, 1)).reshape(b_s, 2 * GROUP_KV, cch.shape[1])
                   for cch in (cache_kv_g1, cache_kv_g2, cache_kv_g3))
    y_s, ssm_s, conv_s, k_s, v_s = _trunk(
        x_sample.reshape(m_s, D_MODEL), p_sample.reshape(DEPTH, m_s, PLE_DIM), wts, consts,
        tabs_s, m_s, nbatch=b_s, seq=dec, ssm0=ssm0_s, conv0=conv0_s, caches=caches)

    def kv_out(k, v, nbatch, length, gi, keep):
        def tail(a):
            a = a.reshape(nbatch, length, KV_HALF)[:, length - keep:, gi * GROUP_KV:(gi + 1) * GROUP_KV]
            return a.reshape(nbatch, keep, ATT_KV_HEADS, ATT_HEAD_DIM)
        return jnp.stack([tail(k), tail(v)], axis=2)

    kv_p = [kv_out(k_p, v_p, b_p, seq, gi, min(w, seq)) for gi, (w, _) in enumerate(DILATION_GROUPS)]
    kv_s = [kv_out(k_s, v_s, b_s, dec, gi, dec) for gi in range(N_DGROUPS)]
    return (y_p.reshape(b_p, seq, D_MODEL), y_s.reshape(b_s, dec, D_MODEL),
            ssm_p, conv_p, ssm_s, conv_s, kv_p[0], kv_p[1], kv_p[2], kv_s[0], kv_s[1], kv_s[2])
```

```python
import functools

import jax
import jax.numpy as jnp
from jax import lax
from jax.experimental import pallas as pl
from jax.experimental.pallas import tpu as pltpu

F32 = jnp.float32
BF16 = jnp.bfloat16

D_MODEL = 1024
SEQ = 8192
DEPTH = 4
PAST_LEN = 8192
N_A_LAYERS = DEPTH // 2
D_INNER = 2048
SSM_HEADDIM = 64
SSM_HEADS = 32
SSM_GROUPS = 4
D_STATE = 128
CONV_K = 4
BC_DIM = 2 * SSM_GROUPS * D_STATE
CONV_DIM = D_INNER + BC_DIM
SSD_CHUNK = 128
ATT_HEAD_DIM = 64
ATT_HEADS = 16
ATT_KV_HEADS = 4
DILATION_GROUPS = ((128, 1), (512, 4), (2048, 16))
N_DGROUPS = 3
ATT_WIDTH = 1024
KV_HALF = N_DGROUPS * ATT_KV_HEADS * ATT_HEAD_DIM
GROUP_KV = ATT_KV_HEADS * ATT_HEAD_DIM
N_KEYS = 129
ROPE_THETA = 10000.0
PLE_DIM = 256
EPS = 1e-6
LOG2E = 1.4426950408889634
LN2 = 0.6931471805599453

LANES = 128
SUBLANES = 8
VMEM_LIMIT_BYTES = 56 * 1024 * 1024
ROW_TILE = 512
NEG_BIG = -1e30
RELAYOUT_SLOTS = 4
ATTN_Q_BLOCKS = 4
SAMPLE_SEQS_PER_STEP = 2
MAMBA_ROWS = 256
MAMBA_PROJ_PIECE = 256

HEAD_ORDER = (0, 4, 1, 5, 2, 6, 3, 7, 8, 12, 9, 13, 10, 14, 11, 15)


def _cparams(semantics):
    return pltpu.CompilerParams(dimension_semantics=semantics,
                                vmem_limit_bytes=VMEM_LIMIT_BYTES)


def _const_spec(shape):
    nd = len(shape)
    return pl.BlockSpec(shape, lambda *_: (0,) * nd, pipeline_mode=pl.Buffered(1))


def _split_bf16(x, parts):
    out = []
    rem = x
    for _ in range(parts):
        hi = rem.astype(BF16)
        out.append(hi)
        rem = rem - hi.astype(F32)
    return out


def _dot(a, b):
    return jnp.dot(a, b, preferred_element_type=F32)


def _dot_nt(a, b):
    return lax.dot_general(a, b, (((1,), (1,)), ((), ())), preferred_element_type=F32)


def _sigmoid(x):
    return 0.5 + 0.5 * jnp.tanh(0.5 * x)


def _silu(x):
    hx = 0.5 * x
    return hx + hx * jnp.tanh(hx)


def _lo_mask(rows):
    return lax.broadcasted_iota(jnp.int32, (rows, LANES), 1) < ATT_HEAD_DIM


def _rope_partner(v):
    lane = lax.broadcasted_iota(jnp.int32, v.shape, 1)
    return jnp.where((lane & 32) == 0, pltpu.roll(v, 96, 1), pltpu.roll(v, 32, 1))


def _head_norm_rope(y, ms, hw, cos, sin, scale):
    yn = y * lax.rsqrt(ms + EPS) * (hw * scale)
    return yn * cos + _rope_partner(yn) * sin


def _norm_proj_kernel(*refs, segs, use_rope, n_out, tm, has_extra):
    h_ref, nw_ref, w_ref = refs[:3]
    pos = 3
    if has_extra:
        wx_ref = refs[pos]
        pos += 1
    if use_rope:
        hw_ref, cos_ref, sin_ref, bd_ref = refs[pos:pos + 4]
        pos += 4
    n_main = w_ref.shape[1]
    out_refs = refs[pos:pos + n_out]
    scr = refs[pos + n_out] if len(refs) > pos + n_out else None
    x = h_ref[...]
    ms = jnp.mean(x * x, axis=-1, keepdims=True)
    u = (x * lax.rsqrt(ms + EPS) * nw_ref[...]).astype(BF16)
    slot = 0
    for start, width, rope, scale, sinks in segs:
        if start >= n_main:
            acc = _dot(u, wx_ref[:, start - n_main:start - n_main + width])
        else:
            acc = _dot(u, w_ref[:, start:start + width])
        if not rope and all(dil == 0 for _, _, dil in sinks):
            for oi, col_off, _ in sinks:
                out_refs[oi][:, col_off:col_off + width] = acc.astype(out_refs[oi].dtype)
            continue
        for cb in range(width // LANES):
            val = acc[:, cb * LANES:(cb + 1) * LANES]
            if rope:
                if cb % 2 == 0:
                    y2 = acc[:, cb * LANES:(cb + 2) * LANES]
                    ms2 = _dot((y2 * y2).astype(BF16), bd_ref[...])
                val = _head_norm_rope(val, ms2[:, (cb % 2) * LANES:(cb % 2 + 1) * LANES],
                                      hw_ref[...], cos_ref[...], sin_ref[...], scale)
            for oi, col_off, dil in sinks:
                o_ref = out_refs[oi]
                cs = slice(col_off + cb * LANES, col_off + (cb + 1) * LANES)
                if dil == 0:
                    o_ref[:, cs] = val.astype(o_ref.dtype)
                elif dil == 1:
                    o_ref[0, :, cs] = val.astype(o_ref.dtype)
                else:
                    s = slot % RELAYOUT_SLOTS
                    slot += 1
                    scr[s] = val
                    for r in range(dil):
                        o_ref[r, :, cs] = scr[s, pl.ds(r, tm // dil, stride=dil), :].astype(o_ref.dtype)


def _norm_proj(h, nw, w, segs, out_defs, rope_inputs=None, rope_rows=None, nbatch=None,
               w_layer=None, w_extra=None):
    m = h.shape[0]
    tm = min(ROW_TILE, m)
    use_rope = rope_inputs is not None
    if w.ndim == 3:
        w_spec = pl.BlockSpec((None,) + w.shape[1:], lambda i: (w_layer, 0, 0), pipeline_mode=pl.Buffered(1))
    else:
        w_spec = _const_spec(w.shape)
    in_specs = [pl.BlockSpec((tm, D_MODEL), lambda i: (i, 0)),
                _const_spec((1, D_MODEL)),
                w_spec]
    args = [h, nw, w]
    if w_extra is not None:
        in_specs.append(_const_spec(w_extra.shape))
        args.append(w_extra)
    if use_rope:
        hw, cos, sin, bd = rope_inputs
        nblk = rope_rows // tm
        in_specs += [_const_spec((1, LANES)),
                     pl.BlockSpec((tm, LANES), lambda i: (i % nblk, 0)),
                     pl.BlockSpec((tm, LANES), lambda i: (i % nblk, 0)),
                     _const_spec((2 * LANES, 2 * LANES))]
        args += [hw, cos, sin, bd]
    out_shape, out_specs = [], []
    for width, dt, dil in out_defs:
        if dil == 0:
            out_shape.append(jax.ShapeDtypeStruct((m, width), dt))
            out_specs.append(pl.BlockSpec((tm, width), lambda i: (i, 0)))
        else:
            tiles = m // nbatch // tm
            out_shape.append(jax.ShapeDtypeStruct((nbatch, dil, m // nbatch // dil, width), dt))
            out_specs.append(pl.BlockSpec((None, dil, tm // dil, width),
                                          lambda i, tiles=tiles: (i // tiles, 0, i % tiles, 0)))
    scratch = []
    if any(dil > 1 for _, _, dil in out_defs):
        scratch.append(pltpu.VMEM((RELAYOUT_SLOTS, tm, LANES), F32))
    return pl.pallas_call(
        functools.partial(_norm_proj_kernel, segs=segs, use_rope=use_rope, n_out=len(out_defs), tm=tm,
                          has_extra=w_extra is not None),
        grid=(m // tm,),
        in_specs=in_specs,
        out_specs=tuple(out_specs),
        out_shape=tuple(out_shape),
        scratch_shapes=scratch,
        compiler_params=_cparams(("parallel",)),
        name="norm_proj",
    )(*args)


def _ssd_kernel(z_ref, xbc_ref, dt_ref, cw_ref, cb_ref, dtb_ref, alog_ref, dexp_ref, nw_ref,
                tril_ref, e_ref, h0_ref, c0_ref, y_ref, hout_ref, ht_scr, xpad_scr,
                *, rows_in, nchunks):
    q = SSD_CHUNK
    c = pl.program_id(1)

    @pl.when(c == 0)
    def _():
        ht_scr[...] = jnp.transpose(h0_ref[...])
        xpad_scr[...] = c0_ref[...]

    def pad_rows(v):
        if rows_in == q:
            return v
        return jnp.concatenate([v, jnp.zeros((q - rows_in, v.shape[1]), v.dtype)], axis=0)

    y = _ssd_chunk(pad_rows(z_ref[...]), pad_rows(xbc_ref[...]), pad_rows(dt_ref[...]),
                   cw_ref, cb_ref, dtb_ref, alog_ref, dexp_ref, nw_ref, tril_ref, e_ref,
                   ht_scr, xpad_scr, rows_in)
    y_ref[...] = y[:rows_in].astype(y_ref.dtype)

    @pl.when(c == nchunks - 1)
    def _():
        hout_ref[...] = jnp.transpose(ht_scr[...])


def _ssd_chunk(zz, xbc, dt_raw, cw_ref, cb_ref, dtb_ref, alog_ref, dexp_ref, nw_ref, tril_ref, e_ref,
               ht_scr, xpad_scr, rows_valid, side=None):
    q = SSD_CHUNK
    gw = D_INNER // SSM_GROUPS

    def run_side(n=1):
        for _ in range(n):
            if side:
                side.pop(0)()

    tail = xpad_scr[...]
    xbc3 = xbc.reshape(q // SUBLANES, SUBLANES, CONV_DIM)
    row8 = lax.broadcasted_iota(jnp.int32, (1, SUBLANES, CONV_DIM), 1)
    conv = cb_ref[...] + cw_ref[CONV_K - 1:CONV_K, :] * xbc
    for k in range(CONV_K - 1):
        sh = CONV_K - 1 - k
        rot3 = pltpu.roll(xbc3, sh, 1)
        prev3 = jnp.concatenate([pltpu.roll(tail, sh, 0)[None], rot3[:-1]], axis=0)
        shifted = jnp.where(row8 < sh, prev3, rot3).reshape(q, CONV_DIM)
        conv = conv + cw_ref[k:k + 1, :] * shifted
    xpad_scr[...] = xbc[q - SUBLANES:q]
    run_side()
    act = _silu(conv)
    xs = act[:, :D_INNER]
    bm = act[:, D_INNER:D_INNER + SSM_GROUPS * D_STATE]
    cm = act[:, D_INNER + SSM_GROUPS * D_STATE:]

    dtr = dt_raw + dtb_ref[...]
    dt = jnp.maximum(dtr, 0.0) + jnp.log1p(jnp.exp(-jnp.abs(dtr)))
    if rows_valid < q:
        row = lax.broadcasted_iota(jnp.int32, dt.shape, 0)
        dt = jnp.where(row < rows_valid, dt, 0.0)
    a = dt * (-LOG2E * jnp.exp(alog_ref[...]))
    tril = tril_ref[...]
    acum = sum(_dot(tril, part) for part in _split_bf16(a, 3))
    acum_t = jnp.transpose(acum)

    e = e_ref[...]
    grp = lax.broadcasted_iota(jnp.int32, (q, LANES), 1) >> 5
    zero_l = jnp.zeros((q, LANES), BF16)

    def packed(parts):
        out = zero_l
        for idx, part in enumerate(parts):
            out = jnp.where(grp == idx, part, out)
        return out

    dt_e = _dot(packed(_split_bf16(dt, 2)), e)
    acum_e = _dot(packed(_split_bf16(acum, 3)), e)
    alast_e = acum_e[q - 1:q, :]
    exp_acum_e = jnp.exp2(acum_e)
    decay_end_e = jnp.exp2(alast_e - acum_e)
    chunk_decay_e = jnp.exp2(alast_e)

    xdt = xs * dt_e
    xdt_bf = xdt.astype(BF16)
    xdtw_bf = (xdt * decay_end_e).astype(BF16)

    li = lax.broadcasted_iota(jnp.int32, (q, q), 0)
    si = lax.broadcasted_iota(jnp.int32, (q, q), 1)
    causal = li >= si
    lo = _lo_mask(q)
    zero_bf = jnp.zeros((q, LANES), BF16)

    y_parts = []
    for g in range(SSM_GROUPS):
        gs = slice(g * gw, (g + 1) * gw)
        bg = bm[:, g * D_STATE:(g + 1) * D_STATE]
        cg_bf = cm[:, g * D_STATE:(g + 1) * D_STATE].astype(BF16)
        cb = jnp.where(causal, _dot_nt(cg_bf, bg.astype(BF16)), 0.0)
        htg = ht_scr[:, gs]
        y_off = _dot(cg_bf, htg.astype(BF16)) * exp_acum_e[:, gs]
        blocks = []
        for j in range(gw // LANES):
            hd = g * (SSM_HEADS // SSM_GROUPS) + 2 * j
            xpair = xdt_bf[:, hd * SSM_HEADDIM:hd * SSM_HEADDIM + LANES]
            mats = []
            for hh in (hd, hd + 1):
                seg = acum[:, hh:hh + 1] - acum_t[hh:hh + 1, :]
                mats.append((cb * jnp.exp2(jnp.minimum(seg, 0.0))).astype(BF16))
            xstack = jnp.concatenate([jnp.where(lo, xpair, zero_bf), jnp.where(lo, zero_bf, xpair)], axis=0)
            blocks.append(_dot(jnp.concatenate(mats, axis=1), xstack))
            run_side()
        y_diag = jnp.concatenate(blocks, axis=1)
        bgt_bf = jnp.transpose(bg).astype(BF16)
        st = _dot(bgt_bf, xdtw_bf[:, gs])
        ht_scr[:, gs] = htg * chunk_decay_e[:, gs] + st
        y_parts.append(y_diag + y_off)

    y = jnp.concatenate(y_parts, axis=1) + xs * dexp_ref[...]
    y = y * _silu(zz)
    run_side()
    normed = []
    for g in range(SSM_GROUPS):
        yg = y[:, g * gw:(g + 1) * gw]
        ms = jnp.mean(yg * yg, axis=-1, keepdims=True)
        normed.append(yg * lax.rsqrt(ms + EPS))
    return jnp.concatenate(normed, axis=1) * nw_ref[...]


def _ssd(z, xbc, dtp, lw, consts, h0, c0, layer, *, nbatch, rows_in, nchunks, y_dtype):
    m = z.shape[0]
    row_map = lambda b, c: (b * nchunks + c, 0)
    batch_map = lambda b, c: (b, 0, 0)
    state_map = lambda b, c: (layer, b, 0, 0)
    in_specs = [
        pl.BlockSpec((rows_in, D_INNER), row_map),
        pl.BlockSpec((rows_in, CONV_DIM), row_map),
        pl.BlockSpec((rows_in, LANES), row_map),
        _const_spec((CONV_K, CONV_DIM)),
        _const_spec((1, CONV_DIM)),
        _const_spec((1, LANES)),
        _const_spec((1, LANES)),
        _const_spec((1, D_INNER)),
        _const_spec((1, D_INNER)),
        _const_spec((SSD_CHUNK, SSD_CHUNK)),
        _const_spec((LANES, D_INNER)),
        pl.BlockSpec((None, None, D_INNER, D_STATE), state_map),
        pl.BlockSpec((None, None, SUBLANES, CONV_DIM), state_map),
    ]
    out_specs = (pl.BlockSpec((rows_in, D_INNER), row_map),
                 pl.BlockSpec((None, D_INNER, D_STATE), batch_map))
    out_shape = (jax.ShapeDtypeStruct((m, D_INNER), y_dtype),
                 jax.ShapeDtypeStruct((nbatch, D_INNER, D_STATE), F32))
    return pl.pallas_call(
        functools.partial(_ssd_kernel, rows_in=rows_in, nchunks=nchunks),
        grid=(nbatch, nchunks),
        in_specs=in_specs,
        out_specs=out_specs,
        out_shape=out_shape,
        scratch_shapes=[pltpu.VMEM((D_STATE, D_INNER), F32),
                        pltpu.VMEM((SUBLANES, CONV_DIM), F32)],
        compiler_params=_cparams(("parallel", "arbitrary")),
        name="ssd",
    )(z, xbc, dtp, lw["conv_w"], lw["conv_b"], lw["dt_bias"], lw["a_log"], lw["d_exp"],
      lw["norm_w"], consts["tril"], consts["expand"], h0, c0)


def _mamba_layer_kernel(ha_ref, hc_ref, p_ref, nw_ref, win_ref, wdt_ref, cw_ref, cb_ref, dtb_ref, alog_ref,
                        dexp_ref, mnw_ref, tril_ref, e_ref, wout_ref, pnw_ref, gw_ref, pw_ref,
                        out_ref, sout_ref, cout_ref,
                        proj0, proj1, y0, y1, ht_scr, xpad_scr, *, nsteps, rows):
    s = pl.program_id(0)
    q = SSD_CHUNK

    @pl.when(s == 0)
    def _():
        proj1[...] = jnp.zeros(proj1.shape, proj1.dtype)
        y0[...] = jnp.zeros(y0.shape, y0.dtype)
        y1[...] = jnp.zeros(y1.shape, y1.dtype)

    @pl.when((s == 0) | (s % nsteps == 1))
    def _():
        ht_scr[...] = jnp.zeros(ht_scr.shape, ht_scr.dtype)
        xpad_scr[...] = jnp.zeros(xpad_scr.shape, xpad_scr.dtype)

    def stages(proj_w, proj_r, y_w, y_r):
        vals = {}

        def a_norm():
            x = ha_ref[...]
            ms = jnp.mean(x * x, axis=-1, keepdims=True)
            vals["u"] = (x * lax.rsqrt(ms + EPS) * nw_ref[...]).astype(BF16)

        def a_piece(c0, c1):
            def emit():
                proj_w[:, c0:c1] = _dot(vals["u"], win_ref[:, c0:c1])
            return emit

        def c_out(c0, c1):
            def emit():
                vals["h1", c0] = hc_ref[:, c0:c1] + _dot(y_r[...], wout_ref[:, c0:c1])
            return emit

        def c_norm():
            h1 = jnp.concatenate([vals["h1", c0] for c0 in c_cols], axis=1)
            vals["h1"] = h1
            ms = jnp.mean(h1 * h1, axis=-1, keepdims=True)
            vals["un"] = (h1 * lax.rsqrt(ms + EPS) * pnw_ref[...]).astype(BF16)

        def c_gate(c0, c1):
            def emit():
                gate = _sigmoid(_dot(vals["un"], gw_ref[:, c0:c1]))
                pe = _dot(p_ref[...].astype(BF16), pw_ref[:, c0:c1])
                out_ref[:, c0:c1] = vals["h1"][:, c0:c1] + gate * pe
            return emit

        def a_dt():
            proj_w[:, n_main:] = _dot(vals["u"], wdt_ref[...])

        n_main = D_INNER + CONV_DIM
        step = MAMBA_PROJ_PIECE
        a_work = [a_piece(c0, c0 + step) for c0 in range(0, n_main, step)] + [a_dt]
        c_cols = list(range(0, D_MODEL, step))
        c_work = ([c_out(c0, c0 + step) for c0 in c_cols] + [c_norm]
                  + [c_gate(c0, c0 + step) for c0 in c_cols])
        side = [a_norm]
        while a_work or c_work:
            if c_work:
                side.append(c_work.pop(0))
            if a_work:
                side.append(a_work.pop(0))
            if a_work:
                side.append(a_work.pop(0))
        for j in range(rows // q):
            rs = slice(j * q, (j + 1) * q)
            y = _ssd_chunk(proj_r[rs, 0:D_INNER], proj_r[rs, D_INNER:D_INNER + CONV_DIM],
                           proj_r[rs, D_INNER + CONV_DIM:], cw_ref, cb_ref, dtb_ref, alog_ref,
                           dexp_ref, mnw_ref, tril_ref, e_ref, ht_scr, xpad_scr, q, side=side)
            y_w[rs, :] = y.astype(y_w.dtype)
        while side:
            side.pop(0)()

    @pl.when(s % 2 == 0)
    def _():
        stages(proj0, proj1, y1, y0)

    @pl.when(s % 2 == 1)
    def _():
        stages(proj1, proj0, y0, y1)

    @pl.when((s > 0) & (s % nsteps == 0))
    def _():
        sout_ref[...] = jnp.transpose(ht_scr[...])
        cout_ref[...] = xpad_scr[...]


def _mamba_layer(h, p_all, layer, nw, lw, consts, pnw, gw, pw, *, nbatch, seq):
    m = h.shape[0]
    rows = MAMBA_ROWS
    nsteps = seq // rows
    n_proj = D_INNER + CONV_DIM + LANES
    total = nbatch * nsteps
    a_map = lambda s: (jnp.minimum(s, total - 1), 0)
    c_map = lambda s: (jnp.clip(s - 2, 0, total - 1), 0)
    seq_map = lambda s: (jnp.clip((s - 1) // nsteps, 0, nbatch - 1), 0, 0)
    in_specs = [
        pl.BlockSpec((rows, D_MODEL), a_map),
        pl.BlockSpec((rows, D_MODEL), c_map),
        pl.BlockSpec((None, rows, PLE_DIM), lambda s: (layer, jnp.clip(s - 2, 0, total - 1), 0)),
        _const_spec((1, D_MODEL)),
        _const_spec(lw["in_w"].shape),
        _const_spec((D_MODEL, LANES)),
        _const_spec((CONV_K, CONV_DIM)),
        _const_spec((1, CONV_DIM)),
        _const_spec((1, LANES)),
        _const_spec((1, LANES)),
        _const_spec((1, D_INNER)),
        _const_spec((1, D_INNER)),
        _const_spec((SSD_CHUNK, SSD_CHUNK)),
        _const_spec((LANES, D_INNER)),
        _const_spec((D_INNER, D_MODEL)),
        _const_spec((1, D_MODEL)),
        _const_spec((D_MODEL, D_MODEL)),
        _const_spec((PLE_DIM, D_MODEL)),
    ]
    out_specs = (pl.BlockSpec((rows, D_MODEL), c_map),
                 pl.BlockSpec((None, D_INNER, D_STATE), seq_map),
                 pl.BlockSpec((None, SUBLANES, CONV_DIM), seq_map))
    out_shape = (jax.ShapeDtypeStruct((m, D_MODEL), F32),
                 jax.ShapeDtypeStruct((nbatch, D_INNER, D_STATE), F32),
                 jax.ShapeDtypeStruct((nbatch, SUBLANES, CONV_DIM), F32))
    return pl.pallas_call(
        functools.partial(_mamba_layer_kernel, nsteps=nsteps, rows=rows),
        grid=(total + 2,),
        in_specs=in_specs,
        out_specs=out_specs,
        out_shape=out_shape,
        scratch_shapes=[pltpu.VMEM((rows, n_proj), F32), pltpu.VMEM((rows, n_proj), F32),
                        pltpu.VMEM((rows, D_INNER), BF16), pltpu.VMEM((rows, D_INNER), BF16),
                        pltpu.VMEM((D_STATE, D_INNER), F32),
                        pltpu.VMEM((SUBLANES, CONV_DIM), F32)],
        compiler_params=_cparams(("arbitrary",)),
        name="mamba_layer",
    )(h, h, p_all, nw, lw["in_w"], lw["dt_w"], lw["conv_w"], lw["conv_b"], lw["dt_bias"], lw["a_log"],
      lw["d_exp"], lw["norm_w"], consts["tril"], consts["expand"], lw["out_w"], pnw, gw, pw)


def _attn_prompt_kernel(q_ref, kcur_ref, kprev_ref, vcur_ref, vprev_ref, o_ref, l_ref, *, nq):
    i = pl.program_id(2)
    t = SSD_CHUNK
    npair = GROUP_KV // ATT_HEAD_DIM
    qi = lax.broadcasted_iota(jnp.int32, (t, 2 * t), 0)
    kk = lax.broadcasted_iota(jnp.int32, (t, 2 * t), 1)
    band = (kk >= qi) & (kk <= qi + (N_KEYS - 1))
    bias_inner = jnp.where(band, 0.0, NEG_BIG)
    bias_first = jnp.where(band & ((kk >= t) | (i > 0)), 0.0, NEG_BIG)
    lo = _lo_mask(t)
    lo2 = _lo_mask(2 * t)
    zero_v = jnp.zeros((2 * t, LANES), BF16)
    ones_stack = jnp.concatenate([jnp.where(lo2, 1.0, 0.0), jnp.where(lo2, 0.0, 1.0)],
                                 axis=0).astype(BF16)
    lane = lax.broadcasted_iota(jnp.int32, (t, LANES), 1)
    lane_slot = (lane & (ATT_HEAD_DIM - 1)) >> 3
    for s in range(nq):
        rows = slice(s * t, (s + 1) * t)
        bias = bias_first if s == 0 else bias_inner
        l_c = jnp.zeros((t, LANES), F32)
        for kc in range(GROUP_KV // LANES):
            ks = slice(kc * LANES, (kc + 1) * LANES)
            if s == 0:
                k2 = jnp.concatenate([kprev_ref[:, ks], kcur_ref[0:t, ks]], axis=0)
                v2 = jnp.concatenate([vprev_ref[:, ks], vcur_ref[0:t, ks]], axis=0)
            else:
                k2 = kcur_ref[(s - 1) * t:(s + 1) * t, ks]
                v2 = vcur_ref[(s - 1) * t:(s + 1) * t, ks]
            k_stack = jnp.concatenate([jnp.where(lo2, k2, zero_v), jnp.where(lo2, zero_v, k2)], axis=0)
            v_stack = jnp.concatenate([jnp.where(lo2, v2, zero_v), jnp.where(lo2, zero_v, v2)], axis=0)
            rhs = jnp.concatenate([v_stack, ones_stack], axis=1)
            lhs = jnp.concatenate(
                [q_ref[rows, (kc * npair + pb) * LANES:(kc * npair + pb + 1) * LANES] for pb in range(npair)],
                axis=0)
            sc = _dot_nt(lhs, k_stack).reshape(npair, t, 4 * t)
            sc_a = sc[:, :, :2 * t] + bias[None]
            sc_b = sc[:, :, 2 * t:] + bias[None]
            mx_a = jnp.max(sc_a, axis=-1, keepdims=True)
            mx_b = jnp.max(sc_b, axis=-1, keepdims=True)
            p_cat = jnp.concatenate([jnp.exp2(sc_a - mx_a), jnp.exp2(sc_b - mx_b)], axis=-1)
            res = _dot(p_cat.astype(BF16).reshape(npair * t, 4 * t), rhs)
            for pb in range(npair):
                blk = kc * npair + pb
                num = res[pb * t:(pb + 1) * t, :LANES]
                den = res[pb * t:(pb + 1) * t, LANES:]
                o_ref[rows, blk * LANES:(blk + 1) * LANES] = (num / den).astype(o_ref.dtype)
                lse = jnp.where(lo, mx_a[pb], mx_b[pb]) * LN2 + jnp.log(den)
                l_c = jnp.where(lane_slot == blk, lse, l_c)
        l_ref[rows, :] = l_c


def _attn_prompt_group(q, k, v, gi):
    nbatch, dil, rows, _ = q.shape
    t = SSD_CHUNK
    nq = ATTN_Q_BLOCKS
    cur = lambda b, r, i: (b, r, i, 0)
    prev = lambda b, r, i: (b, r, jnp.maximum(nq * i - 1, 0), 0)
    return pl.pallas_call(
        functools.partial(_attn_prompt_kernel, nq=nq),
        grid=(nbatch, dil, rows // (nq * t)),
        in_specs=[pl.BlockSpec((None, None, nq * t, ATT_WIDTH), cur),
                  pl.BlockSpec((None, None, nq * t, GROUP_KV), cur),
                  pl.BlockSpec((None, None, t, GROUP_KV), prev),
                  pl.BlockSpec((None, None, nq * t, GROUP_KV), cur),
                  pl.BlockSpec((None, None, t, GROUP_KV), prev)],
        out_specs=(pl.BlockSpec((None, None, nq * t, ATT_WIDTH), cur),
                   pl.BlockSpec((None, None, nq * t, LANES), cur)),
        out_shape=(jax.ShapeDtypeStruct((nbatch, dil, rows, ATT_WIDTH), BF16),
                   jax.ShapeDtypeStruct((nbatch, dil, rows, LANES), F32)),
        compiler_params=_cparams(("parallel", "parallel", "arbitrary")),
        name=f"attn_prompt_g{gi}",
    )(q, k, k, v, v)


def _attn_sample_kernel(q_ref, kn_ref, vn_ref, c1_ref, c2_ref, c3_ref, o_ref, *, dec_seq, nseq):
    for bb in range(nseq):
        rs = slice(bb * dec_seq, (bb + 1) * dec_seq)
        _attn_sample_one(q_ref.at[rs], kn_ref.at[rs], vn_ref.at[rs],
                         (c1_ref.at[bb], c2_ref.at[bb], c3_ref.at[bb]), o_ref.at[rs], dec_seq)


def _attn_sample_one(q_ref, kn_ref, vn_ref, caches, o_ref, dec_seq):
    nrow = 8 * dec_seq
    lo_row = _lo_mask(dec_seq)
    pad = LANES - dec_seq
    outs = [[None] * N_DGROUPS for _ in range(ATT_WIDTH // LANES)]
    lses = [[None] * N_DGROUPS for _ in range(ATT_WIDTH // LANES)]
    for gi, (_, dil) in enumerate(DILATION_GROUPS):
        cref = caches[gi]
        w = cref.shape[1]
        tq = lax.broadcasted_iota(jnp.int32, (nrow, w), 0) & (dec_seq - 1)
        rho = lax.broadcasted_iota(jnp.int32, (nrow, w), 1)
        delta = w + tq - rho
        valid_c = (rho >= tq) & ((delta & (dil - 1)) == 0) & (delta <= (N_KEYS - 1) * dil)
        bias_c = jnp.where(valid_c, 0.0, NEG_BIG)
        tq_n = lax.broadcasted_iota(jnp.int32, (nrow, LANES), 0) & (dec_seq - 1)
        tn = lax.broadcasted_iota(jnp.int32, (nrow, LANES), 1)
        valid_n = (tn <= tq_n) & (((tq_n - tn) & (dil - 1)) == 0)
        bias_n = jnp.where(valid_n, 0.0, NEG_BIG)
        gsl = slice(gi * GROUP_KV, (gi + 1) * GROUP_KV)
        k_new = jnp.concatenate([kn_ref[:, gsl], jnp.zeros((pad, GROUP_KV), F32)], axis=0)
        v_new = jnp.concatenate([vn_ref[:, gsl], jnp.zeros((pad, GROUP_KV), F32)], axis=0)
        for kc in range(GROUP_KV // LANES):
            ks = slice(kc * LANES, (kc + 1) * LANES)
            rows = []
            for pb in range(4):
                blk = kc * 4 + pb
                q2 = q_ref[:, gi * ATT_WIDTH + blk * LANES:gi * ATT_WIDTH + (blk + 1) * LANES]
                rows.append(jnp.where(lo_row, q2, 0.0))
                rows.append(jnp.where(lo_row, 0.0, q2))
            lhs = jnp.concatenate(rows, axis=0).astype(BF16)
            kcache_t = cref[ks, :].astype(BF16)
            vcache_t = cref[GROUP_KV + kc * LANES:GROUP_KV + (kc + 1) * LANES, :].astype(BF16)
            s_c = _dot(lhs, kcache_t) + bias_c
            s_n = _dot_nt(lhs, k_new[:, ks].astype(BF16)) + bias_n
            mx = jnp.maximum(jnp.max(s_c, axis=-1, keepdims=True),
                             jnp.max(s_n, axis=-1, keepdims=True))
            p_c = jnp.exp(s_c - mx)
            p_n = jnp.exp(s_n - mx)
            den = jnp.sum(p_c, axis=-1, keepdims=True) + jnp.sum(p_n, axis=-1, keepdims=True)
            num = _dot_nt(p_c.astype(BF16), vcache_t) + _dot(p_n.astype(BF16), v_new[:, ks].astype(BF16))
            on = num / den
            lse = mx + jnp.log(den)
            for pb in range(4):
                blk = kc * 4 + pb
                r0 = pb * 2 * dec_seq
                r1 = r0 + dec_seq
                outs[blk][gi] = jnp.where(lo_row, on[r0:r1], on[r1:r1 + dec_seq])
                lses[blk][gi] = jnp.where(lo_row, lse[r0:r1], lse[r1:r1 + dec_seq])
    for blk in range(ATT_WIDTH // LANES):
        ls = lses[blk]
        mx = jnp.maximum(jnp.maximum(ls[0], ls[1]), ls[2])
        ws = [jnp.exp(l - mx) for l in ls]
        tot = ws[0] + ws[1] + ws[2]
        o = (outs[blk][0] * ws[0] + outs[blk][1] * ws[1] + outs[blk][2] * ws[2]) / tot
        o_ref[:, blk * LANES:(blk + 1) * LANES] = o


def _attn_sample(q, k, v, caches, nbatch, dec_seq):
    m = q.shape[0]
    nseq = SAMPLE_SEQS_PER_STEP
    rows = nseq * dec_seq
    row_map = lambda b: (b, 0)
    in_specs = [pl.BlockSpec((rows, N_DGROUPS * ATT_WIDTH), row_map),
                pl.BlockSpec((rows, KV_HALF), row_map),
                pl.BlockSpec((rows, KV_HALF), row_map)]
    for cch in caches:
        in_specs.append(pl.BlockSpec((nseq, 2 * GROUP_KV, cch.shape[2]), lambda b: (b, 0, 0)))
    return pl.pallas_call(
        functools.partial(_attn_sample_kernel, dec_seq=dec_seq, nseq=nseq),
        grid=(nbatch // nseq,),
        in_specs=in_specs,
        out_specs=pl.BlockSpec((rows, ATT_WIDTH), row_map),
        out_shape=jax.ShapeDtypeStruct((m, ATT_WIDTH), F32),
        compiler_params=_cparams(("parallel",)),
        name="attn_sample",
    )(q, k, v, *caches)


def _token_order(ref, scr, slot, cb, dil, tm):
    cs = slice(cb * LANES, (cb + 1) * LANES)
    if dil == 1:
        return ref[0, :, cs].astype(F32)
    for r in range(dil):
        scr[slot, pl.ds(r, tm // dil, stride=dil), :] = ref[r, :, cs].astype(F32)
    return scr[slot]


def _out_ple_kernel(*refs, mode, tm):
    if mode == "mamba":
        (y_ref,) = refs[:1]
        pos = 1
        mix = y_ref[...].astype(BF16)
    elif mode == "attn_merge":
        o_refs = refs[0:3]
        l_refs = refs[3:6]
        gate_ref, x_ref = refs[6:8]
        pos = 8
        scr, mix_scr = refs[-2:]
        refs = refs[:-2]
        dils = [d for _, d in DILATION_GROUPS]
        ls = [_token_order(l_refs[g], scr, g, 0, dils[g], tm) for g in range(N_DGROUPS)]
        mx = jnp.maximum(jnp.maximum(ls[0], ls[1]), ls[2])
        es = [jnp.exp(l - mx) for l in ls]
        tot = es[0] + es[1] + es[2]
        sub = lax.broadcasted_iota(jnp.int32, (tm, LANES), 1) & 7
        packed = jnp.zeros((tm, LANES), BF16)
        for g in range(N_DGROUPS):
            for idx, part in enumerate(_split_bf16(es[g] / tot, 2)):
                packed = jnp.where(sub == 2 * g + idx, part, packed)
        ws_all = _dot(packed, x_ref[...])
        ws = [ws_all[:, g * ATT_WIDTH:(g + 1) * ATT_WIDTH] for g in range(N_DGROUPS)]
        for cb in range(ATT_WIDTH // LANES):
            cs = slice(cb * LANES, (cb + 1) * LANES)
            o = None
            for g in range(N_DGROUPS):
                og = _token_order(o_refs[g], scr, N_DGROUPS + (cb * N_DGROUPS + g) % RELAYOUT_SLOTS,
                                  cb, dils[g], tm)
                term = og * ws[g][:, cs]
                o = term if o is None else o + term
            gate = gate_ref[:, cs]
            mix_scr[:, cs] = (o * _silu(gate)).astype(BF16)
        mix = mix_scr[...]
    else:
        o_ref_in, gate_ref = refs[:2]
        pos = 2
        gate = gate_ref[...]
        mix = (o_ref_in[...] * _silu(gate)).astype(BF16)
    h_ref, p_ref, wout_ref, pnw_ref, gw_ref, pw_ref, out_ref = refs[pos:]
    out_ref[...] = _residual_ple(mix, h_ref[...], p_ref[...], wout_ref, pnw_ref, gw_ref, pw_ref)


def _residual_ple(mix, h, p, wout_ref, pnw_ref, gw_ref, pw_ref):
    h1 = h + _dot(mix, wout_ref[...])
    ms = jnp.mean(h1 * h1, axis=-1, keepdims=True)
    un = (h1 * lax.rsqrt(ms + EPS) * pnw_ref[...]).astype(BF16)
    gate_p = _sigmoid(_dot(un, gw_ref[...]))
    pe = _dot(p.astype(BF16), pw_ref[...])
    return h1 + gate_p * pe


def _out_ple(mix_inputs, mode, h, p_all, layer, wout, pnw, gw, pw):
    m = h.shape[0]
    tm = min(ROW_TILE, m)
    row_map = lambda i: (i, 0)
    in_specs = []
    for a in mix_inputs:
        if a.ndim == 4:
            _, dil, rows, width = a.shape
            tiles = rows * dil // tm
            in_specs.append(pl.BlockSpec((None, dil, tm // dil, width),
                                         lambda i, tiles=tiles: (i // tiles, 0, i % tiles, 0)))
        elif a.shape[0] == m:
            in_specs.append(pl.BlockSpec((tm, a.shape[1]), row_map))
        else:
            in_specs.append(_const_spec(a.shape))
    scratch = []
    if mode == "attn_merge":
        scratch = [pltpu.VMEM((N_DGROUPS + RELAYOUT_SLOTS, tm, LANES), F32),
                   pltpu.VMEM((tm, ATT_WIDTH), BF16)]
    in_specs += [pl.BlockSpec((tm, D_MODEL), row_map),
                 pl.BlockSpec((None, tm, PLE_DIM), lambda i: (layer, i, 0)),
                 _const_spec(wout.shape),
                 _const_spec((1, D_MODEL)),
                 _const_spec((D_MODEL, D_MODEL)),
                 _const_spec((PLE_DIM, D_MODEL))]
    return pl.pallas_call(
        functools.partial(_out_ple_kernel, mode=mode, tm=tm),
        grid=(m // tm,),
        in_specs=in_specs,
        out_specs=pl.BlockSpec((tm, D_MODEL), row_map),
        out_shape=jax.ShapeDtypeStruct((m, D_MODEL), F32),
        scratch_shapes=scratch,
        compiler_params=_cparams(("parallel",)),
        name=f"out_ple_{mode}",
    )(*mix_inputs, h, p_all, wout, pnw, gw, pw)


def _rope_tables(pos):
    half = ATT_HEAD_DIM // 2
    inv = 1.0 / (ROPE_THETA ** (jnp.arange(half, dtype=F32) / half))
    ang = pos.astype(F32)[:, None] * inv[None, :]
    cos, sin = jnp.cos(ang), jnp.sin(ang)
    cos128 = jnp.concatenate([cos, cos, cos, cos], axis=1)
    sin128 = jnp.concatenate([-sin, sin, -sin, sin], axis=1)
    return cos128, sin128


def _permute_cols_kernel(w_ref, o_ref):
    lo = _lo_mask(w_ref.shape[0])
    for pos in range(ATT_HEADS // 2):
        ha, hb = HEAD_ORDER[2 * pos], HEAD_ORDER[2 * pos + 1]
        xa = w_ref[:, (ha // 2) * LANES:(ha // 2 + 1) * LANES]
        xb = w_ref[:, (hb // 2) * LANES:(hb // 2 + 1) * LANES]
        if ha % 2 == 1:
            xa = pltpu.roll(xa, ATT_HEAD_DIM, 1)
        if hb % 2 == 0:
            xb = pltpu.roll(xb, ATT_HEAD_DIM, 1)
        o_ref[:, pos * LANES:(pos + 1) * LANES] = jnp.where(lo, xa, xb).astype(o_ref.dtype)


def _permute_rows_kernel(w_ref, o_ref):
    for pos, head in enumerate(HEAD_ORDER):
        o_ref[pos * ATT_HEAD_DIM:(pos + 1) * ATT_HEAD_DIM, :] = (
            w_ref[head * ATT_HEAD_DIM:(head + 1) * ATT_HEAD_DIM, :].astype(o_ref.dtype))


def _permute_attn_weights(a_in_w, a_out_w):
    nl = a_in_w.shape[0]
    ngrp = a_in_w.shape[2] // ATT_WIDTH
    blk = (None, D_MODEL, ATT_WIDTH)
    in_w = pl.pallas_call(
        _permute_cols_kernel,
        grid=(nl, ngrp),
        in_specs=[pl.BlockSpec(blk, lambda j, g: (j, 0, g))],
        out_specs=pl.BlockSpec(blk, lambda j, g: (j, 0, g)),
        out_shape=jax.ShapeDtypeStruct(a_in_w.shape, BF16),
        compiler_params=_cparams(("parallel", "parallel")),
        name="permute_cols",
    )(a_in_w)
    out_w = pl.pallas_call(
        _permute_rows_kernel,
        grid=(nl,),
        in_specs=[pl.BlockSpec(blk, lambda j: (j, 0, 0))],
        out_specs=pl.BlockSpec(blk, lambda j: (j, 0, 0)),
        out_shape=jax.ShapeDtypeStruct(a_out_w.shape, BF16),
        compiler_params=_cparams(("parallel",)),
        name="permute_rows",
    )(a_out_w)
    return in_w, out_w


def _prep_weights(norm_w, m_in_w, m_conv_w, m_conv_b, m_dt_bias, m_A_log, m_D, m_norm_w, m_out_w,
                  kv_norm_w, kv_w, k_norm_w, a_in_w, a_q_norm_w, a_out_w, ple_w, ple_gate_w,
                  ple_norm_w):
    row = lambda v: v.reshape(1, -1).astype(F32)
    reps = LANES // SSM_HEADS
    lane_pad = lambda v: jnp.tile(v.astype(F32), reps).reshape(1, LANES)
    mamba = []
    for i in range(N_A_LAYERS):
        mamba.append(dict(
            in_w=m_in_w[i].astype(BF16),
            dt_w=jnp.tile(m_in_w[i][:, D_INNER + CONV_DIM:], (1, reps)).astype(BF16),
            conv_w=m_conv_w[i].astype(F32),
            conv_b=row(m_conv_b[i]),
            dt_bias=lane_pad(m_dt_bias[i]),
            a_log=lane_pad(m_A_log[i]),
            d_exp=row(jnp.repeat(m_D[i], SSM_HEADDIM)),
            norm_w=row(m_norm_w[i]),
            out_w=m_out_w[i].astype(BF16),
        ))
    attn = []
    a_in_bf, a_out_bf = _permute_attn_weights(a_in_w, a_out_w)
    for j in range(DEPTH - N_A_LAYERS):
        attn.append(dict(
            q_norm_w=row(jnp.tile(a_q_norm_w[j], LANES // ATT_HEAD_DIM)),
            out_w=a_out_bf[j],
        ))
    return dict(
        norm_w=[row(norm_w[i]) for i in range(DEPTH)],
        mamba=mamba,
        attn=attn,
        attn_in_w=a_in_bf,
        kv_norm_w=row(kv_norm_w),
        kv_w=kv_w.astype(BF16),
        k_norm_w=row(jnp.tile(k_norm_w, LANES // ATT_HEAD_DIM)),
        ple_w=[ple_w[i].astype(BF16) for i in range(DEPTH)],
        ple_gate_w=[ple_gate_w[i].astype(BF16) for i in range(DEPTH)],
        ple_norm_w=[row(ple_norm_w[i]) for i in range(DEPTH)],
    )


def _constants():
    t = SSD_CHUNK
    tril = jnp.tril(jnp.ones((t, t), F32)).astype(BF16)
    head = jnp.arange(LANES, dtype=jnp.int32)[:, None]
    col = jnp.arange(D_INNER, dtype=jnp.int32)[None, :]
    expand = (head % SSM_HEADS == col // SSM_HEADDIM).astype(BF16)
    li = jnp.arange(LANES, dtype=jnp.int32)
    l2 = jnp.arange(2 * LANES, dtype=jnp.int32)
    bd = ((l2[:, None] // ATT_HEAD_DIM) == (l2[None, :] // ATT_HEAD_DIM)).astype(F32) / ATT_HEAD_DIM
    ccol = jnp.arange(N_DGROUPS * ATT_WIDTH, dtype=jnp.int32)[None, :]
    cgrp, cc = ccol // ATT_WIDTH, ccol % ATT_WIDTH
    slot = cc // LANES + jnp.where(cc % LANES < ATT_HEAD_DIM, 0, 8)
    lrow = li[:, None]
    lse_spread = ((lrow >> 3 == slot) & ((lrow & 7) >> 1 == cgrp)).astype(BF16)
    return dict(tril=tril, expand=expand, bd=bd.astype(BF16), lse_spread=lse_spread)


_Q_SCALE = ATT_HEAD_DIM ** -0.5
_DILS = tuple(d for _, d in DILATION_GROUPS)
M_IN_COLS = D_INNER + CONV_DIM + SSM_HEADS
MAMBA_SEGS = ((0, D_INNER, False, 1.0, ((0, 0, 0),)),
              (D_INNER, CONV_DIM, False, 1.0, ((1, 0, 0),)),
              (M_IN_COLS, LANES, False, 1.0, ((2, 0, 0),)))
MAMBA_OUTS = ((D_INNER, F32, 0), (CONV_DIM, F32, 0), (LANES, F32, 0))
KV_SEGS_TOK = ((0, KV_HALF, True, 1.0, ((0, 0, 0),)), (KV_HALF, KV_HALF, False, 1.0, ((1, 0, 0),)))
KV_OUTS_TOK = ((KV_HALF, F32, 0), (KV_HALF, F32, 0))
ATTN_SEGS_TOK = ((0, N_DGROUPS * ATT_WIDTH, True, _Q_SCALE, ((0, 0, 0),)),
                 (N_DGROUPS * ATT_WIDTH, ATT_WIDTH, False, 1.0, ((1, 0, 0),)))
ATTN_OUTS_TOK = ((N_DGROUPS * ATT_WIDTH, F32, 0), (ATT_WIDTH, F32, 0))
KV_SEGS_RES = tuple(
    (g * GROUP_KV, GROUP_KV, True, 1.0, ((0, g * GROUP_KV, 0), (2 + g, 0, _DILS[g])))
    for g in range(N_DGROUPS)) + tuple(
    (KV_HALF + g * GROUP_KV, GROUP_KV, False, 1.0, ((1, g * GROUP_KV, 0), (2 + N_DGROUPS + g, 0, _DILS[g])))
    for g in range(N_DGROUPS))
KV_OUTS_RES = KV_OUTS_TOK + tuple((GROUP_KV, BF16, d) for d in _DILS) * 2
ATTN_SEGS_RES = tuple((g * ATT_WIDTH, ATT_WIDTH, True, _Q_SCALE * LOG2E, ((g, 0, _DILS[g]),))
                      for g in range(N_DGROUPS)) + (
    (N_DGROUPS * ATT_WIDTH, ATT_WIDTH, False, 1.0, ((N_DGROUPS, 0, 0),)),)
ATTN_OUTS_RES = tuple((ATT_WIDTH, BF16, d) for d in _DILS) + ((ATT_WIDTH, F32, 0),)


def _trunk(x, p_all, wts, consts, rope_tabs, rope_rows, *, nbatch, seq, ssm0, conv0, caches):
    prompt = caches is None
    m = x.shape[0]
    h = x
    new_ssm, new_conv = [], []
    for i in range(N_A_LAYERS):
        lw = wts["mamba"][i]
        ple = (wts["ple_norm_w"][i], wts["ple_gate_w"][i], wts["ple_w"][i])
        if prompt:
            h, h_fin, xbc_tail = _mamba_layer(h, p_all, i, wts["norm_w"][i], lw, consts, *ple,
                                              nbatch=nbatch, seq=seq)
            new_conv.append(xbc_tail[:, SUBLANES - (CONV_K - 1):])
        else:
            z, xbc, dtp = _norm_proj(h, wts["norm_w"][i], lw["in_w"], MAMBA_SEGS, MAMBA_OUTS,
                                     w_extra=lw["dt_w"])
            y, h_fin = _ssd(z, xbc, dtp, lw, consts, ssm0, conv0, i, nbatch=nbatch,
                            rows_in=seq, nchunks=1, y_dtype=F32)
            new_conv.append(xbc.reshape(nbatch, seq, CONV_DIM)[:, seq - (CONV_K - 1):])
            h = _out_ple([y], "mamba", h, p_all, i, lw["out_w"], *ple)
        new_ssm.append(h_fin.reshape(nbatch, SSM_HEADS, SSM_HEADDIM, D_STATE))
    cos, sin = rope_tabs
    kv_outs = _norm_proj(h, wts["kv_norm_w"], wts["kv_w"],
                         KV_SEGS_RES if prompt else KV_SEGS_TOK,
                         KV_OUTS_RES if prompt else KV_OUTS_TOK,
                         rope_inputs=(wts["k_norm_w"], cos, sin, consts["bd"]), rope_rows=rope_rows,
                         nbatch=nbatch)
    k, v = kv_outs[:2]
    for j in range(DEPTH - N_A_LAYERS):
        i = N_A_LAYERS + j
        aw = wts["attn"][j]
        q_outs = _norm_proj(h, wts["norm_w"][i], wts["attn_in_w"],
                            ATTN_SEGS_RES if prompt else ATTN_SEGS_TOK,
                            ATTN_OUTS_RES if prompt else ATTN_OUTS_TOK,
                            rope_inputs=(aw["q_norm_w"], cos, sin, consts["bd"]), rope_rows=rope_rows,
                            nbatch=nbatch, w_layer=j)
        gate = q_outs[-1]
        if prompt:
            os_, ls_ = [], []
            for gi in range(N_DGROUPS):
                o_g, l_g = _attn_prompt_group(q_outs[gi], kv_outs[2 + gi], kv_outs[2 + N_DGROUPS + gi], gi)
                os_.append(o_g)
                ls_.append(l_g)
            mix_inputs, mode = os_ + ls_ + [gate, consts["lse_spread"]], "attn_merge"
        else:
            o = _attn_sample(q_outs[0], k, v, caches, nbatch, seq)
            mix_inputs, mode = [o, gate], "attn"
        h = _out_ple(mix_inputs, mode, h, p_all, i, aw["out_w"], wts["ple_norm_w"][i],
                     wts["ple_gate_w"][i], wts["ple_w"][i])
    return h, jnp.stack(new_ssm, axis=0), jnp.stack(new_conv, axis=0), k, v


def kernel(x_prompt, x_sample, state_ssm, state_conv, cache_kv_g1, cache_kv_g2, cache_kv_g3,
           p_prompt, p_sample, norm_w, m_in_w, m_conv_w, m_conv_b, m_dt_bias, m_A_log, m_D,
           m_norm_w, m_out_w, kv_norm_w, kv_w, k_norm_w, a_in_w, a_q_norm_w, a_out_w,
           ple_w, ple_gate_w, ple_norm_w):
    wts = _prep_weights(norm_w, m_in_w, m_conv_w, m_conv_b, m_dt_bias, m_A_log, m_D, m_norm_w,
                        m_out_w, kv_norm_w, kv_w, k_norm_w, a_in_w, a_q_norm_w, a_out_w, ple_w,
                        ple_gate_w, ple_norm_w)
    consts = _constants()

    b_p, seq = x_prompt.shape[0], x_prompt.shape[1]
    m_p = b_p * seq
    tabs_p = _rope_tables(jnp.arange(seq, dtype=jnp.int32))
    y_p, ssm_p, conv_p, k_p, v_p = _trunk(
        x_prompt.reshape(m_p, D_MODEL), p_prompt.reshape(DEPTH, m_p, PLE_DIM), wts, consts,
        tabs_p, seq, nbatch=b_p, seq=seq, ssm0=None, conv0=None, caches=None)

    b_s, dec = x_sample.shape[0], x_sample.shape[1]
    m_s = b_s * dec
    pos_s = PAST_LEN + jnp.arange(dec, dtype=jnp.int32)
    tabs_s = tuple(jnp.tile(t, (b_s, 1)) for t in _rope_tables(pos_s))
    ssm0_s = state_ssm.reshape(N_A_LAYERS, b_s, D_INNER, D_STATE)
    conv0_s = jnp.pad(state_conv, ((0, 0), (0, 0), (SUBLANES - (CONV_K - 1), 0), (0, 0)))
    caches = tuple(jnp.transpose(cch, (0, 2, 3, 4, 1)).reshape(b_s, 2 * GROUP_KV, cch.shape[1])
                   for cch in (cache_kv_g1, cache_kv_g2, cache_kv_g3))
    y_s, ssm_s, conv_s, k_s, v_s = _trunk(
        x_sample.reshape(m_s, D_MODEL), p_sample.reshape(DEPTH, m_s, PLE_DIM), wts, consts,
        tabs_s, m_s, nbatch=b_s, seq=dec, ssm0=ssm0_s, conv0=conv0_s, caches=caches)

    def kv_out(k, v, nbatch, length, gi, keep):
        def tail(a):
            a = a.reshape(nbatch, length, KV_HALF)[:, length - keep:, gi * GROUP_KV:(gi + 1) * GROUP_KV]
            return a.reshape(nbatch, keep, ATT_KV_HEADS, ATT_HEAD_DIM)
        return jnp.stack([tail(k), tail(v)], axis=2)

    kv_p = [kv_out(k_p, v_p, b_p, seq, gi, min(w, seq)) for gi, (w, _) in enumerate(DILATION_GROUPS)]
    kv_s = [kv_out(k_s, v_s, b_s, dec, gi, dec) for gi in range(N_DGROUPS)]
    return (y_p.reshape(b_p, seq, D_MODEL), y_s.reshape(b_s, dec, D_MODEL),
            ssm_p, conv_p, ssm_s, conv_s, kv_p[0], kv_p[1], kv_p[2], kv_s[0], kv_s[1], kv_s[2])
```

```python
import functools

import jax
import jax.numpy as jnp
from jax import lax
from jax.experimental import pallas as pl
from jax.experimental.pallas import tpu as pltpu

F32 = jnp.float32
BF16 = jnp.bfloat16

D_MODEL = 1024
SEQ = 8192
DEPTH = 4
PAST_LEN = 8192
N_A_LAYERS = DEPTH // 2
D_INNER = 2048
SSM_HEADDIM = 64
SSM_HEADS = 32
SSM_GROUPS = 4
D_STATE = 128
CONV_K = 4
BC_DIM = 2 * SSM_GROUPS * D_STATE
CONV_DIM = D_INNER + BC_DIM
SSD_CHUNK = 128
ATT_HEAD_DIM = 64
ATT_HEADS = 16
ATT_KV_HEADS = 4
DILATION_GROUPS = ((128, 1), (512, 4), (2048, 16))
N_DGROUPS = 3
ATT_WIDTH = 1024
KV_HALF = N_DGROUPS * ATT_KV_HEADS * ATT_HEAD_DIM
GROUP_KV = ATT_KV_HEADS * ATT_HEAD_DIM
N_KEYS = 129
ROPE_THETA = 10000.0
PLE_DIM = 256
EPS = 1e-6
SSM_HEADS_LOG2 = 5
ROPE_HALF = ATT_HEAD_DIM // 2
LSE_SLOT_LANES = 8
LSE_SLOT_LOG2 = 3
LOG2E = 1.4426950408889634
LN2 = 0.6931471805599453

LANES = 128
SUBLANES = 8
VMEM_LIMIT_BYTES = 56 * 1024 * 1024
ROW_TILE = 512
NEG_BIG = -1e30
RELAYOUT_SLOTS = 4
ATTN_Q_BLOCKS = 4
SAMPLE_SEQS_PER_STEP = 2
MAMBA_ROWS = 256
MAMBA_PROJ_PIECE = 256

HEAD_ORDER = (0, 4, 1, 5, 2, 6, 3, 7, 8, 12, 9, 13, 10, 14, 11, 15)


def _cparams(semantics):
    return pltpu.CompilerParams(dimension_semantics=semantics,
                                vmem_limit_bytes=VMEM_LIMIT_BYTES)


def _const_spec(shape):
    nd = len(shape)
    return pl.BlockSpec(shape, lambda *_: (0,) * nd, pipeline_mode=pl.Buffered(1))


def _split_bf16(x, parts):
    out = []
    rem = x
    for _ in range(parts):
        hi = rem.astype(BF16)
        out.append(hi)
        rem = rem - hi.astype(F32)
    return out


def _dot(a, b):
    return jnp.dot(a, b, preferred_element_type=F32)


def _dot_nt(a, b):
    return lax.dot_general(a, b, (((1,), (1,)), ((), ())), preferred_element_type=F32)


def _sigmoid(x):
    return 0.5 + 0.5 * jnp.tanh(0.5 * x)


def _silu(x):
    hx = 0.5 * x
    return hx + hx * jnp.tanh(hx)


def _lo_mask(rows):
    return lax.broadcasted_iota(jnp.int32, (rows, LANES), 1) < ATT_HEAD_DIM


def _rope_partner(v):
    lane = lax.broadcasted_iota(jnp.int32, v.shape, 1)
    return jnp.where((lane & ROPE_HALF) == 0, pltpu.roll(v, LANES - ROPE_HALF, 1), pltpu.roll(v, ROPE_HALF, 1))


def _head_norm_rope(y, ms, hw, cos, sin, scale):
    yn = y * lax.rsqrt(ms + EPS) * (hw * scale)
    return yn * cos + _rope_partner(yn) * sin


def _norm_proj_kernel(*refs, segs, use_rope, n_out, tm, has_extra):
    h_ref, nw_ref, w_ref = refs[:3]
    pos = 3
    if has_extra:
        wx_ref = refs[pos]
        pos += 1
    if use_rope:
        hw_ref, cos_ref, sin_ref, bd_ref = refs[pos:pos + 4]
        pos += 4
    n_main = w_ref.shape[1]
    out_refs = refs[pos:pos + n_out]
    scr = refs[pos + n_out] if len(refs) > pos + n_out else None
    x = h_ref[...]
    ms = jnp.mean(x * x, axis=-1, keepdims=True)
    u = (x * lax.rsqrt(ms + EPS) * nw_ref[...]).astype(BF16)
    slot = 0
    for start, width, rope, scale, sinks in segs:
        if start >= n_main:
            acc = _dot(u, wx_ref[:, start - n_main:start - n_main + width])
        else:
            acc = _dot(u, w_ref[:, start:start + width])
        if not rope and all(dil == 0 for _, _, dil in sinks):
            for oi, col_off, _ in sinks:
                out_refs[oi][:, col_off:col_off + width] = acc.astype(out_refs[oi].dtype)
            continue
        for cb in range(width // LANES):
            val = acc[:, cb * LANES:(cb + 1) * LANES]
            if rope:
                if cb % 2 == 0:
                    y2 = acc[:, cb * LANES:(cb + 2) * LANES]
                    ms2 = _dot((y2 * y2).astype(BF16), bd_ref[...])
                val = _head_norm_rope(val, ms2[:, (cb % 2) * LANES:(cb % 2 + 1) * LANES],
                                      hw_ref[...], cos_ref[...], sin_ref[...], scale)
            for oi, col_off, dil in sinks:
                o_ref = out_refs[oi]
                cs = slice(col_off + cb * LANES, col_off + (cb + 1) * LANES)
                if dil == 0:
                    o_ref[:, cs] = val.astype(o_ref.dtype)
                elif dil == 1:
                    o_ref[0, :, cs] = val.astype(o_ref.dtype)
                else:
                    s = slot % RELAYOUT_SLOTS
                    slot += 1
                    scr[s] = val
                    for r in range(dil):
                        o_ref[r, :, cs] = scr[s, pl.ds(r, tm // dil, stride=dil), :].astype(o_ref.dtype)


def _norm_proj(h, nw, w, segs, out_defs, rope_inputs=None, rope_rows=None, nbatch=None,
               w_layer=None, w_extra=None):
    m = h.shape[0]
    tm = min(ROW_TILE, m)
    use_rope = rope_inputs is not None
    if w.ndim == 3:
        w_spec = pl.BlockSpec((None,) + w.shape[1:], lambda i: (w_layer, 0, 0), pipeline_mode=pl.Buffered(1))
    else:
        w_spec = _const_spec(w.shape)
    in_specs = [pl.BlockSpec((tm, D_MODEL), lambda i: (i, 0)),
                _const_spec((1, D_MODEL)),
                w_spec]
    args = [h, nw, w]
    if w_extra is not None:
        in_specs.append(_const_spec(w_extra.shape))
        args.append(w_extra)
    if use_rope:
        hw, cos, sin, bd = rope_inputs
        nblk = rope_rows // tm
        in_specs += [_const_spec((1, LANES)),
                     pl.BlockSpec((tm, LANES), lambda i: (i % nblk, 0)),
                     pl.BlockSpec((tm, LANES), lambda i: (i % nblk, 0)),
                     _const_spec((2 * LANES, 2 * LANES))]
        args += [hw, cos, sin, bd]
    out_shape, out_specs = [], []
    for width, dt, dil in out_defs:
        if dil == 0:
            out_shape.append(jax.ShapeDtypeStruct((m, width), dt))
            out_specs.append(pl.BlockSpec((tm, width), lambda i: (i, 0)))
        else:
            tiles = m // nbatch // tm
            out_shape.append(jax.ShapeDtypeStruct((nbatch, dil, m // nbatch // dil, width), dt))
            out_specs.append(pl.BlockSpec((None, dil, tm // dil, width),
                                          lambda i, tiles=tiles: (i // tiles, 0, i % tiles, 0)))
    scratch = []
    if any(dil > 1 for _, _, dil in out_defs):
        scratch.append(pltpu.VMEM((RELAYOUT_SLOTS, tm, LANES), F32))
    return pl.pallas_call(
        functools.partial(_norm_proj_kernel, segs=segs, use_rope=use_rope, n_out=len(out_defs), tm=tm,
                          has_extra=w_extra is not None),
        grid=(m // tm,),
        in_specs=in_specs,
        out_specs=tuple(out_specs),
        out_shape=tuple(out_shape),
        scratch_shapes=scratch,
        compiler_params=_cparams(("parallel",)),
        name="norm_proj",
    )(*args)


def _ssd_kernel(z_ref, xbc_ref, dt_ref, cw_ref, cb_ref, dtb_ref, alog_ref, dexp_ref, nw_ref,
                tril_ref, e_ref, h0_ref, c0_ref, y_ref, hout_ref, ht_scr, xpad_scr,
                *, rows_in, nchunks):
    q = SSD_CHUNK
    c = pl.program_id(1)

    @pl.when(c == 0)
    def _():
        ht_scr[...] = jnp.transpose(h0_ref[...])
        xpad_scr[...] = c0_ref[...]

    def pad_rows(v):
        if rows_in == q:
            return v
        return jnp.concatenate([v, jnp.zeros((q - rows_in, v.shape[1]), v.dtype)], axis=0)

    y = _ssd_chunk(pad_rows(z_ref[...]), pad_rows(xbc_ref[...]), pad_rows(dt_ref[...]),
                   cw_ref, cb_ref, dtb_ref, alog_ref, dexp_ref, nw_ref, tril_ref, e_ref,
                   ht_scr, xpad_scr, rows_in)
    y_ref[...] = y[:rows_in].astype(y_ref.dtype)

    @pl.when(c == nchunks - 1)
    def _():
        hout_ref[...] = jnp.transpose(ht_scr[...])


def _ssd_chunk(zz, xbc, dt_raw, cw_ref, cb_ref, dtb_ref, alog_ref, dexp_ref, nw_ref, tril_ref, e_ref,
               ht_scr, xpad_scr, rows_valid, side=None):
    q = SSD_CHUNK
    gw = D_INNER // SSM_GROUPS

    def run_side(n=1):
        for _ in range(n):
            if side:
                side.pop(0)()

    tail = xpad_scr[...]
    xbc3 = xbc.reshape(q // SUBLANES, SUBLANES, CONV_DIM)
    row8 = lax.broadcasted_iota(jnp.int32, (1, SUBLANES, CONV_DIM), 1)
    conv = cb_ref[...] + cw_ref[CONV_K - 1:CONV_K, :] * xbc
    for k in range(CONV_K - 1):
        sh = CONV_K - 1 - k
        rot3 = pltpu.roll(xbc3, sh, 1)
        prev3 = jnp.concatenate([pltpu.roll(tail, sh, 0)[None], rot3[:-1]], axis=0)
        shifted = jnp.where(row8 < sh, prev3, rot3).reshape(q, CONV_DIM)
        conv = conv + cw_ref[k:k + 1, :] * shifted
    xpad_scr[...] = xbc[q - SUBLANES:q]
    run_side()
    act = _silu(conv)
    xs = act[:, :D_INNER]
    bm = act[:, D_INNER:D_INNER + SSM_GROUPS * D_STATE]
    cm = act[:, D_INNER + SSM_GROUPS * D_STATE:]

    dtr = dt_raw + dtb_ref[...]
    dt = jnp.maximum(dtr, 0.0) + jnp.log1p(jnp.exp(-jnp.abs(dtr)))
    if rows_valid < q:
        row = lax.broadcasted_iota(jnp.int32, dt.shape, 0)
        dt = jnp.where(row < rows_valid, dt, 0.0)
    a = dt * (-LOG2E * jnp.exp(alog_ref[...]))
    tril = tril_ref[...]
    acum = sum(_dot(tril, part) for part in _split_bf16(a, 3))
    acum_t = jnp.transpose(acum)

    e = e_ref[...]
    grp = lax.broadcasted_iota(jnp.int32, (q, LANES), 1) >> SSM_HEADS_LOG2
    zero_l = jnp.zeros((q, LANES), BF16)

    def packed(parts):
        out = zero_l
        for idx, part in enumerate(parts):
            out = jnp.where(grp == idx, part, out)
        return out

    dt_e = _dot(packed(_split_bf16(dt, 2)), e)
    acum_e = _dot(packed(_split_bf16(acum, 3)), e)
    alast_e = acum_e[q - 1:q, :]
    exp_acum_e = jnp.exp2(acum_e)
    decay_end_e = jnp.exp2(alast_e - acum_e)
    chunk_decay_e = jnp.exp2(alast_e)

    xdt = xs * dt_e
    xdt_bf = xdt.astype(BF16)
    xdtw_bf = (xdt * decay_end_e).astype(BF16)

    li = lax.broadcasted_iota(jnp.int32, (q, q), 0)
    si = lax.broadcasted_iota(jnp.int32, (q, q), 1)
    causal = li >= si
    lo = _lo_mask(q)
    zero_bf = jnp.zeros((q, LANES), BF16)

    y_parts = []
    for g in range(SSM_GROUPS):
        gs = slice(g * gw, (g + 1) * gw)
        bg = bm[:, g * D_STATE:(g + 1) * D_STATE]
        cg_bf = cm[:, g * D_STATE:(g + 1) * D_STATE].astype(BF16)
        cb = jnp.where(causal, _dot_nt(cg_bf, bg.astype(BF16)), 0.0)
        htg = ht_scr[:, gs]
        y_off = _dot(cg_bf, htg.astype(BF16)) * exp_acum_e[:, gs]
        blocks = []
        for j in range(gw // LANES):
            hd = g * (SSM_HEADS // SSM_GROUPS) + 2 * j
            xpair = xdt_bf[:, hd * SSM_HEADDIM:hd * SSM_HEADDIM + LANES]
            mats = []
            for hh in (hd, hd + 1):
                seg = acum[:, hh:hh + 1] - acum_t[hh:hh + 1, :]
                mats.append((cb * jnp.exp2(jnp.minimum(seg, 0.0))).astype(BF16))
            xstack = jnp.concatenate([jnp.where(lo, xpair, zero_bf), jnp.where(lo, zero_bf, xpair)], axis=0)
            blocks.append(_dot(jnp.concatenate(mats, axis=1), xstack))
            run_side()
        y_diag = jnp.concatenate(blocks, axis=1)
        bgt_bf = jnp.transpose(bg).astype(BF16)
        st = _dot(bgt_bf, xdtw_bf[:, gs])
        ht_scr[:, gs] = htg * chunk_decay_e[:, gs] + st
        y_parts.append(y_diag + y_off)

    y = jnp.concatenate(y_parts, axis=1) + xs * dexp_ref[...]
    y = y * _silu(zz)
    run_side()
    normed = []
    for g in range(SSM_GROUPS):
        yg = y[:, g * gw:(g + 1) * gw]
        ms = jnp.mean(yg * yg, axis=-1, keepdims=True)
        normed.append(yg * lax.rsqrt(ms + EPS))
    return jnp.concatenate(normed, axis=1) * nw_ref[...]


def _ssd(z, xbc, dtp, lw, consts, h0, c0, layer, *, nbatch, rows_in, nchunks, y_dtype):
    m = z.shape[0]
    row_map = lambda b, c: (b * nchunks + c, 0)
    batch_map = lambda b, c: (b, 0, 0)
    state_map = lambda b, c: (layer, b, 0, 0)
    in_specs = [
        pl.BlockSpec((rows_in, D_INNER), row_map),
        pl.BlockSpec((rows_in, CONV_DIM), row_map),
        pl.BlockSpec((rows_in, LANES), row_map),
        _const_spec((CONV_K, CONV_DIM)),
        _const_spec((1, CONV_DIM)),
        _const_spec((1, LANES)),
        _const_spec((1, LANES)),
        _const_spec((1, D_INNER)),
        _const_spec((1, D_INNER)),
        _const_spec((SSD_CHUNK, SSD_CHUNK)),
        _const_spec((LANES, D_INNER)),
        pl.BlockSpec((None, None, D_INNER, D_STATE), state_map),
        pl.BlockSpec((None, None, SUBLANES, CONV_DIM), state_map),
    ]
    out_specs = (pl.BlockSpec((rows_in, D_INNER), row_map),
                 pl.BlockSpec((None, D_INNER, D_STATE), batch_map))
    out_shape = (jax.ShapeDtypeStruct((m, D_INNER), y_dtype),
                 jax.ShapeDtypeStruct((nbatch, D_INNER, D_STATE), F32))
    return pl.pallas_call(
        functools.partial(_ssd_kernel, rows_in=rows_in, nchunks=nchunks),
        grid=(nbatch, nchunks),
        in_specs=in_specs,
        out_specs=out_specs,
        out_shape=out_shape,
        scratch_shapes=[pltpu.VMEM((D_STATE, D_INNER), F32),
                        pltpu.VMEM((SUBLANES, CONV_DIM), F32)],
        compiler_params=_cparams(("parallel", "arbitrary")),
        name="ssd",
    )(z, xbc, dtp, lw["conv_w"], lw["conv_b"], lw["dt_bias"], lw["a_log"], lw["d_exp"],
      lw["norm_w"], consts["tril"], consts["expand"], h0, c0)


def _mamba_layer_kernel(ha_ref, hc_ref, p_ref, nw_ref, win_ref, wdt_ref, cw_ref, cb_ref, dtb_ref, alog_ref,
                        dexp_ref, mnw_ref, tril_ref, e_ref, wout_ref, pnw_ref, gw_ref, pw_ref,
                        out_ref, sout_ref, cout_ref,
                        proj0, proj1, y0, y1, ht_scr, xpad_scr, *, nsteps, rows):
    s = pl.program_id(0)
    q = SSD_CHUNK

    @pl.when(s == 0)
    def _():
        proj1[...] = jnp.zeros(proj1.shape, proj1.dtype)
        y0[...] = jnp.zeros(y0.shape, y0.dtype)
        y1[...] = jnp.zeros(y1.shape, y1.dtype)

    @pl.when((s == 0) | (s % nsteps == 1))
    def _():
        ht_scr[...] = jnp.zeros(ht_scr.shape, ht_scr.dtype)
        xpad_scr[...] = jnp.zeros(xpad_scr.shape, xpad_scr.dtype)

    def stages(proj_w, proj_r, y_w, y_r):
        vals = {}

        def a_norm():
            x = ha_ref[...]
            ms = jnp.mean(x * x, axis=-1, keepdims=True)
            vals["u"] = (x * lax.rsqrt(ms + EPS) * nw_ref[...]).astype(BF16)

        def a_piece(c0, c1):
            def emit():
                proj_w[:, c0:c1] = _dot(vals["u"], win_ref[:, c0:c1])
            return emit

        def c_out(c0, c1):
            def emit():
                vals["h1", c0] = hc_ref[:, c0:c1] + _dot(y_r[...], wout_ref[:, c0:c1])
            return emit

        def c_norm():
            h1 = jnp.concatenate([vals["h1", c0] for c0 in c_cols], axis=1)
            vals["h1"] = h1
            ms = jnp.mean(h1 * h1, axis=-1, keepdims=True)
            vals["un"] = (h1 * lax.rsqrt(ms + EPS) * pnw_ref[...]).astype(BF16)

        def c_gate(c0, c1):
            def emit():
                gate = _sigmoid(_dot(vals["un"], gw_ref[:, c0:c1]))
                pe = _dot(p_ref[...].astype(BF16), pw_ref[:, c0:c1])
                out_ref[:, c0:c1] = vals["h1"][:, c0:c1] + gate * pe
            return emit

        def a_dt():
            proj_w[:, n_main:] = _dot(vals["u"], wdt_ref[...])

        n_main = D_INNER + CONV_DIM
        step = MAMBA_PROJ_PIECE
        a_work = [a_piece(c0, c0 + step) for c0 in range(0, n_main, step)] + [a_dt]
        c_cols = list(range(0, D_MODEL, step))
        c_work = ([c_out(c0, c0 + step) for c0 in c_cols] + [c_norm]
                  + [c_gate(c0, c0 + step) for c0 in c_cols])
        side = [a_norm]
        while a_work or c_work:
            if c_work:
                side.append(c_work.pop(0))
            if a_work:
                side.append(a_work.pop(0))
            if a_work:
                side.append(a_work.pop(0))
        for j in range(rows // q):
            rs = slice(j * q, (j + 1) * q)
            y = _ssd_chunk(proj_r[rs, 0:D_INNER], proj_r[rs, D_INNER:D_INNER + CONV_DIM],
                           proj_r[rs, D_INNER + CONV_DIM:], cw_ref, cb_ref, dtb_ref, alog_ref,
                           dexp_ref, mnw_ref, tril_ref, e_ref, ht_scr, xpad_scr, q, side=side)
            y_w[rs, :] = y.astype(y_w.dtype)
        while side:
            side.pop(0)()

    @pl.when(s % 2 == 0)
    def _():
        stages(proj0, proj1, y1, y0)

    @pl.when(s % 2 == 1)
    def _():
        stages(proj1, proj0, y0, y1)

    @pl.when((s > 0) & (s % nsteps == 0))
    def _():
        sout_ref[...] = jnp.transpose(ht_scr[...])
        cout_ref[...] = xpad_scr[...]


def _mamba_layer(h, p_all, layer, nw, lw, consts, pnw, gw, pw, *, nbatch, seq):
    m = h.shape[0]
    rows = MAMBA_ROWS
    nsteps = seq // rows
    n_proj = D_INNER + CONV_DIM + LANES
    total = nbatch * nsteps
    a_map = lambda s: (jnp.minimum(s, total - 1), 0)
    c_map = lambda s: (jnp.clip(s - 2, 0, total - 1), 0)
    seq_map = lambda s: (jnp.clip((s - 1) // nsteps, 0, nbatch - 1), 0, 0)
    in_specs = [
        pl.BlockSpec((rows, D_MODEL), a_map),
        pl.BlockSpec((rows, D_MODEL), c_map),
        pl.BlockSpec((None, rows, PLE_DIM), lambda s: (layer, jnp.clip(s - 2, 0, total - 1), 0)),
        _const_spec((1, D_MODEL)),
        _const_spec(lw["in_w"].shape),
        _const_spec((D_MODEL, LANES)),
        _const_spec((CONV_K, CONV_DIM)),
        _const_spec((1, CONV_DIM)),
        _const_spec((1, LANES)),
        _const_spec((1, LANES)),
        _const_spec((1, D_INNER)),
        _const_spec((1, D_INNER)),
        _const_spec((SSD_CHUNK, SSD_CHUNK)),
        _const_spec((LANES, D_INNER)),
        _const_spec((D_INNER, D_MODEL)),
        _const_spec((1, D_MODEL)),
        _const_spec((D_MODEL, D_MODEL)),
        _const_spec((PLE_DIM, D_MODEL)),
    ]
    out_specs = (pl.BlockSpec((rows, D_MODEL), c_map),
                 pl.BlockSpec((None, D_INNER, D_STATE), seq_map),
                 pl.BlockSpec((None, SUBLANES, CONV_DIM), seq_map))
    out_shape = (jax.ShapeDtypeStruct((m, D_MODEL), F32),
                 jax.ShapeDtypeStruct((nbatch, D_INNER, D_STATE), F32),
                 jax.ShapeDtypeStruct((nbatch, SUBLANES, CONV_DIM), F32))
    return pl.pallas_call(
        functools.partial(_mamba_layer_kernel, nsteps=nsteps, rows=rows),
        grid=(total + 2,),
        in_specs=in_specs,
        out_specs=out_specs,
        out_shape=out_shape,
        scratch_shapes=[pltpu.VMEM((rows, n_proj), F32), pltpu.VMEM((rows, n_proj), F32),
                        pltpu.VMEM((rows, D_INNER), BF16), pltpu.VMEM((rows, D_INNER), BF16),
                        pltpu.VMEM((D_STATE, D_INNER), F32),
                        pltpu.VMEM((SUBLANES, CONV_DIM), F32)],
        compiler_params=_cparams(("arbitrary",)),
        name="mamba_layer",
    )(h, h, p_all, nw, lw["in_w"], lw["dt_w"], lw["conv_w"], lw["conv_b"], lw["dt_bias"], lw["a_log"],
      lw["d_exp"], lw["norm_w"], consts["tril"], consts["expand"], lw["out_w"], pnw, gw, pw)


def _attn_prompt_kernel(q_ref, kcur_ref, kprev_ref, vcur_ref, vprev_ref, o_ref, l_ref, *, nq):
    i = pl.program_id(2)
    t = SSD_CHUNK
    npair = GROUP_KV // ATT_HEAD_DIM
    qi = lax.broadcasted_iota(jnp.int32, (t, 2 * t), 0)
    kk = lax.broadcasted_iota(jnp.int32, (t, 2 * t), 1)
    band = (kk >= qi) & (kk <= qi + (N_KEYS - 1))
    bias_inner = jnp.where(band, 0.0, NEG_BIG)
    bias_first = jnp.where(band & ((kk >= t) | (i > 0)), 0.0, NEG_BIG)
    lo = _lo_mask(t)
    lo2 = _lo_mask(2 * t)
    zero_v = jnp.zeros((2 * t, LANES), BF16)
    ones_stack = jnp.concatenate([jnp.where(lo2, 1.0, 0.0), jnp.where(lo2, 0.0, 1.0)],
                                 axis=0).astype(BF16)
    lane = lax.broadcasted_iota(jnp.int32, (t, LANES), 1)
    lane_slot = (lane & (ATT_HEAD_DIM - 1)) >> LSE_SLOT_LOG2
    for s in range(nq):
        rows = slice(s * t, (s + 1) * t)
        bias = bias_first if s == 0 else bias_inner
        l_c = jnp.zeros((t, LANES), F32)
        for kc in range(GROUP_KV // LANES):
            ks = slice(kc * LANES, (kc + 1) * LANES)
            if s == 0:
                k2 = jnp.concatenate([kprev_ref[:, ks], kcur_ref[0:t, ks]], axis=0)
                v2 = jnp.concatenate([vprev_ref[:, ks], vcur_ref[0:t, ks]], axis=0)
            else:
                k2 = kcur_ref[(s - 1) * t:(s + 1) * t, ks]
                v2 = vcur_ref[(s - 1) * t:(s + 1) * t, ks]
            k_stack = jnp.concatenate([jnp.where(lo2, k2, zero_v), jnp.where(lo2, zero_v, k2)], axis=0)
            v_stack = jnp.concatenate([jnp.where(lo2, v2, zero_v), jnp.where(lo2, zero_v, v2)], axis=0)
            rhs = jnp.concatenate([v_stack, ones_stack], axis=1)
            lhs = jnp.concatenate(
                [q_ref[rows, (kc * npair + pb) * LANES:(kc * npair + pb + 1) * LANES] for pb in range(npair)],
                axis=0)
            sc = _dot_nt(lhs, k_stack).reshape(npair, t, 4 * t)
            sc_a = sc[:, :, :2 * t] + bias[None]
            sc_b = sc[:, :, 2 * t:] + bias[None]
            mx_a = jnp.max(sc_a, axis=-1, keepdims=True)
            mx_b = jnp.max(sc_b, axis=-1, keepdims=True)
            p_cat = jnp.concatenate([jnp.exp2(sc_a - mx_a), jnp.exp2(sc_b - mx_b)], axis=-1)
            res = _dot(p_cat.astype(BF16).reshape(npair * t, 4 * t), rhs)
            for pb in range(npair):
                blk = kc * npair + pb
                num = res[pb * t:(pb + 1) * t, :LANES]
                den = res[pb * t:(pb + 1) * t, LANES:]
                o_ref[rows, blk * LANES:(blk + 1) * LANES] = (num / den).astype(o_ref.dtype)
                lse = jnp.where(lo, mx_a[pb], mx_b[pb]) * LN2 + jnp.log(den)
                l_c = jnp.where(lane_slot == blk, lse, l_c)
        l_ref[rows, :] = l_c


def _attn_prompt_group(q, k, v, gi):
    nbatch, dil, rows, _ = q.shape
    t = SSD_CHUNK
    nq = ATTN_Q_BLOCKS
    cur = lambda b, r, i: (b, r, i, 0)
    prev = lambda b, r, i: (b, r, jnp.maximum(nq * i - 1, 0), 0)
    return pl.pallas_call(
        functools.partial(_attn_prompt_kernel, nq=nq),
        grid=(nbatch, dil, rows // (nq * t)),
        in_specs=[pl.BlockSpec((None, None, nq * t, ATT_WIDTH), cur),
                  pl.BlockSpec((None, None, nq * t, GROUP_KV), cur),
                  pl.BlockSpec((None, None, t, GROUP_KV), prev),
                  pl.BlockSpec((None, None, nq * t, GROUP_KV), cur),
                  pl.BlockSpec((None, None, t, GROUP_KV), prev)],
        out_specs=(pl.BlockSpec((None, None, nq * t, ATT_WIDTH), cur),
                   pl.BlockSpec((None, None, nq * t, LANES), cur)),
        out_shape=(jax.ShapeDtypeStruct((nbatch, dil, rows, ATT_WIDTH), BF16),
                   jax.ShapeDtypeStruct((nbatch, dil, rows, LANES), F32)),
        compiler_params=_cparams(("parallel", "parallel", "arbitrary")),
        name=f"attn_prompt_g{gi}",
    )(q, k, k, v, v)


def _attn_sample_kernel(q_ref, kn_ref, vn_ref, c1_ref, c2_ref, c3_ref, o_ref, *, dec_seq, nseq):
    for bb in range(nseq):
        rs = slice(bb * dec_seq, (bb + 1) * dec_seq)
        _attn_sample_one(q_ref.at[rs], kn_ref.at[rs], vn_ref.at[rs],
                         (c1_ref.at[bb], c2_ref.at[bb], c3_ref.at[bb]), o_ref.at[rs], dec_seq)


def _attn_sample_one(q_ref, kn_ref, vn_ref, caches, o_ref, dec_seq):
    nrow = 2 * ATT_KV_HEADS * dec_seq
    lo_row = _lo_mask(dec_seq)
    pad = LANES - dec_seq
    outs = [[None] * N_DGROUPS for _ in range(ATT_WIDTH // LANES)]
    lses = [[None] * N_DGROUPS for _ in range(ATT_WIDTH // LANES)]
    for gi, (_, dil) in enumerate(DILATION_GROUPS):
        cref = caches[gi]
        w = cref.shape[1]
        tq = lax.broadcasted_iota(jnp.int32, (nrow, w), 0) & (dec_seq - 1)
        rho = lax.broadcasted_iota(jnp.int32, (nrow, w), 1)
        delta = w + tq - rho
        valid_c = (rho >= tq) & ((delta & (dil - 1)) == 0) & (delta <= (N_KEYS - 1) * dil)
        bias_c = jnp.where(valid_c, 0.0, NEG_BIG)
        tq_n = lax.broadcasted_iota(jnp.int32, (nrow, LANES), 0) & (dec_seq - 1)
        tn = lax.broadcasted_iota(jnp.int32, (nrow, LANES), 1)
        valid_n = (tn <= tq_n) & (((tq_n - tn) & (dil - 1)) == 0)
        bias_n = jnp.where(valid_n, 0.0, NEG_BIG)
        gsl = slice(gi * GROUP_KV, (gi + 1) * GROUP_KV)
        k_new = jnp.concatenate([kn_ref[:, gsl], jnp.zeros((pad, GROUP_KV), F32)], axis=0)
        v_new = jnp.concatenate([vn_ref[:, gsl], jnp.zeros((pad, GROUP_KV), F32)], axis=0)
        for kc in range(GROUP_KV // LANES):
            ks = slice(kc * LANES, (kc + 1) * LANES)
            rows = []
            for pb in range(4):
                blk = kc * 4 + pb
                q2 = q_ref[:, gi * ATT_WIDTH + blk * LANES:gi * ATT_WIDTH + (blk + 1) * LANES]
                rows.append(jnp.where(lo_row, q2, 0.0))
                rows.append(jnp.where(lo_row, 0.0, q2))
            lhs = jnp.concatenate(rows, axis=0).astype(BF16)
            kcache_t = cref[ks, :].astype(BF16)
            vcache_t = cref[GROUP_KV + kc * LANES:GROUP_KV + (kc + 1) * LANES, :].astype(BF16)
            s_c = _dot(lhs, kcache_t) + bias_c
            s_n = _dot_nt(lhs, k_new[:, ks].astype(BF16)) + bias_n
            mx = jnp.maximum(jnp.max(s_c, axis=-1, keepdims=True),
                             jnp.max(s_n, axis=-1, keepdims=True))
            p_c = jnp.exp(s_c - mx)
            p_n = jnp.exp(s_n - mx)
            den = jnp.sum(p_c, axis=-1, keepdims=True) + jnp.sum(p_n, axis=-1, keepdims=True)
            num = _dot_nt(p_c.astype(BF16), vcache_t) + _dot(p_n.astype(BF16), v_new[:, ks].astype(BF16))
            on = num / den
            lse = mx + jnp.log(den)
            for pb in range(4):
                blk = kc * 4 + pb
                r0 = pb * 2 * dec_seq
                r1 = r0 + dec_seq
                outs[blk][gi] = jnp.where(lo_row, on[r0:r1], on[r1:r1 + dec_seq])
                lses[blk][gi] = jnp.where(lo_row, lse[r0:r1], lse[r1:r1 + dec_seq])
    for blk in range(ATT_WIDTH // LANES):
        ls = lses[blk]
        mx = jnp.maximum(jnp.maximum(ls[0], ls[1]), ls[2])
        ws = [jnp.exp(l - mx) for l in ls]
        tot = ws[0] + ws[1] + ws[2]
        o = (outs[blk][0] * ws[0] + outs[blk][1] * ws[1] + outs[blk][2] * ws[2]) / tot
        o_ref[:, blk * LANES:(blk + 1) * LANES] = o


def _attn_sample(q, k, v, caches, nbatch, dec_seq):
    m = q.shape[0]
    nseq = SAMPLE_SEQS_PER_STEP
    rows = nseq * dec_seq
    row_map = lambda b: (b, 0)
    in_specs = [pl.BlockSpec((rows, N_DGROUPS * ATT_WIDTH), row_map),
                pl.BlockSpec((rows, KV_HALF), row_map),
                pl.BlockSpec((rows, KV_HALF), row_map)]
    for cch in caches:
        in_specs.append(pl.BlockSpec((nseq, 2 * GROUP_KV, cch.shape[2]), lambda b: (b, 0, 0)))
    return pl.pallas_call(
        functools.partial(_attn_sample_kernel, dec_seq=dec_seq, nseq=nseq),
        grid=(nbatch // nseq,),
        in_specs=in_specs,
        out_specs=pl.BlockSpec((rows, ATT_WIDTH), row_map),
        out_shape=jax.ShapeDtypeStruct((m, ATT_WIDTH), F32),
        compiler_params=_cparams(("parallel",)),
        name="attn_sample",
    )(q, k, v, *caches)


def _token_order(ref, scr, slot, cb, dil, tm):
    cs = slice(cb * LANES, (cb + 1) * LANES)
    if dil == 1:
        return ref[0, :, cs].astype(F32)
    for r in range(dil):
        scr[slot, pl.ds(r, tm // dil, stride=dil), :] = ref[r, :, cs].astype(F32)
    return scr[slot]


def _out_ple_kernel(*refs, mode, tm):
    if mode == "mamba":
        (y_ref,) = refs[:1]
        pos = 1
        mix = y_ref[...].astype(BF16)
    elif mode == "attn_merge":
        o_refs = refs[0:3]
        l_refs = refs[3:6]
        gate_ref, x_ref = refs[6:8]
        pos = 8
        scr, mix_scr = refs[-2:]
        refs = refs[:-2]
        dils = [d for _, d in DILATION_GROUPS]
        ls = [_token_order(l_refs[g], scr, g, 0, dils[g], tm) for g in range(N_DGROUPS)]
        mx = jnp.maximum(jnp.maximum(ls[0], ls[1]), ls[2])
        es = [jnp.exp(l - mx) for l in ls]
        tot = es[0] + es[1] + es[2]
        sub = lax.broadcasted_iota(jnp.int32, (tm, LANES), 1) & (LSE_SLOT_LANES - 1)
        packed = jnp.zeros((tm, LANES), BF16)
        for g in range(N_DGROUPS):
            for idx, part in enumerate(_split_bf16(es[g] / tot, 2)):
                packed = jnp.where(sub == 2 * g + idx, part, packed)
        ws_all = _dot(packed, x_ref[...])
        ws = [ws_all[:, g * ATT_WIDTH:(g + 1) * ATT_WIDTH] for g in range(N_DGROUPS)]
        for cb in range(ATT_WIDTH // LANES):
            cs = slice(cb * LANES, (cb + 1) * LANES)
            o = None
            for g in range(N_DGROUPS):
                og = _token_order(o_refs[g], scr, N_DGROUPS + (cb * N_DGROUPS + g) % RELAYOUT_SLOTS,
                                  cb, dils[g], tm)
                term = og * ws[g][:, cs]
                o = term if o is None else o + term
            gate = gate_ref[:, cs]
            mix_scr[:, cs] = (o * _silu(gate)).astype(BF16)
        mix = mix_scr[...]
    else:
        o_ref_in, gate_ref = refs[:2]
        pos = 2
        gate = gate_ref[...]
        mix = (o_ref_in[...] * _silu(gate)).astype(BF16)
    h_ref, p_ref, wout_ref, pnw_ref, gw_ref, pw_ref, out_ref = refs[pos:]
    out_ref[...] = _residual_ple(mix, h_ref[...], p_ref[...], wout_ref, pnw_ref, gw_ref, pw_ref)


def _residual_ple(mix, h, p, wout_ref, pnw_ref, gw_ref, pw_ref):
    h1 = h + _dot(mix, wout_ref[...])
    ms = jnp.mean(h1 * h1, axis=-1, keepdims=True)
    un = (h1 * lax.rsqrt(ms + EPS) * pnw_ref[...]).astype(BF16)
    gate_p = _sigmoid(_dot(un, gw_ref[...]))
    pe = _dot(p.astype(BF16), pw_ref[...])
    return h1 + gate_p * pe


def _out_ple(mix_inputs, mode, h, p_all, layer, wout, pnw, gw, pw):
    m = h.shape[0]
    tm = min(ROW_TILE, m)
    row_map = lambda i: (i, 0)
    in_specs = []
    for a in mix_inputs:
        if a.ndim == 4:
            _, dil, rows, width = a.shape
            tiles = rows * dil // tm
            in_specs.append(pl.BlockSpec((None, dil, tm // dil, width),
                                         lambda i, tiles=tiles: (i // tiles, 0, i % tiles, 0)))
        elif a.shape[0] == m:
            in_specs.append(pl.BlockSpec((tm, a.shape[1]), row_map))
        else:
            in_specs.append(_const_spec(a.shape))
    scratch = []
    if mode == "attn_merge":
        scratch = [pltpu.VMEM((N_DGROUPS + RELAYOUT_SLOTS, tm, LANES), F32),
                   pltpu.VMEM((tm, ATT_WIDTH), BF16)]
    in_specs += [pl.BlockSpec((tm, D_MODEL), row_map),
                 pl.BlockSpec((None, tm, PLE_DIM), lambda i: (layer, i, 0)),
                 _const_spec(wout.shape),
                 _const_spec((1, D_MODEL)),
                 _const_spec((D_MODEL, D_MODEL)),
                 _const_spec((PLE_DIM, D_MODEL))]
    return pl.pallas_call(
        functools.partial(_out_ple_kernel, mode=mode, tm=tm),
        grid=(m // tm,),
        in_specs=in_specs,
        out_specs=pl.BlockSpec((tm, D_MODEL), row_map),
        out_shape=jax.ShapeDtypeStruct((m, D_MODEL), F32),
        scratch_shapes=scratch,
        compiler_params=_cparams(("parallel",)),
        name=f"out_ple_{mode}",
    )(*mix_inputs, h, p_all, wout, pnw, gw, pw)


def _rope_tables(pos):
    half = ATT_HEAD_DIM // 2
    inv = 1.0 / (ROPE_THETA ** (jnp.arange(half, dtype=F32) / half))
    ang = pos.astype(F32)[:, None] * inv[None, :]
    cos, sin = jnp.cos(ang), jnp.sin(ang)
    cos128 = jnp.concatenate([cos, cos, cos, cos], axis=1)
    sin128 = jnp.concatenate([-sin, sin, -sin, sin], axis=1)
    return cos128, sin128


def _permute_cols_kernel(w_ref, o_ref):
    lo = _lo_mask(w_ref.shape[0])
    for pos in range(ATT_HEADS // 2):
        ha, hb = HEAD_ORDER[2 * pos], HEAD_ORDER[2 * pos + 1]
        xa = w_ref[:, (ha // 2) * LANES:(ha // 2 + 1) * LANES]
        xb = w_ref[:, (hb // 2) * LANES:(hb // 2 + 1) * LANES]
        if ha % 2 == 1:
            xa = pltpu.roll(xa, ATT_HEAD_DIM, 1)
        if hb % 2 == 0:
            xb = pltpu.roll(xb, ATT_HEAD_DIM, 1)
        o_ref[:, pos * LANES:(pos + 1) * LANES] = jnp.where(lo, xa, xb).astype(o_ref.dtype)


def _permute_rows_kernel(w_ref, o_ref):
    for pos, head in enumerate(HEAD_ORDER):
        o_ref[pos * ATT_HEAD_DIM:(pos + 1) * ATT_HEAD_DIM, :] = (
            w_ref[head * ATT_HEAD_DIM:(head + 1) * ATT_HEAD_DIM, :].astype(o_ref.dtype))


def _permute_attn_weights(a_in_w, a_out_w):
    nl = a_in_w.shape[0]
    ngrp = a_in_w.shape[2] // ATT_WIDTH
    blk = (None, D_MODEL, ATT_WIDTH)
    in_w = pl.pallas_call(
        _permute_cols_kernel,
        grid=(nl, ngrp),
        in_specs=[pl.BlockSpec(blk, lambda j, g: (j, 0, g))],
        out_specs=pl.BlockSpec(blk, lambda j, g: (j, 0, g)),
        out_shape=jax.ShapeDtypeStruct(a_in_w.shape, BF16),
        compiler_params=_cparams(("parallel", "parallel")),
        name="permute_cols",
    )(a_in_w)
    out_w = pl.pallas_call(
        _permute_rows_kernel,
        grid=(nl,),
        in_specs=[pl.BlockSpec(blk, lambda j: (j, 0, 0))],
        out_specs=pl.BlockSpec(blk, lambda j: (j, 0, 0)),
        out_shape=jax.ShapeDtypeStruct(a_out_w.shape, BF16),
        compiler_params=_cparams(("parallel",)),
        name="permute_rows",
    )(a_out_w)
    return in_w, out_w


def _prep_weights(norm_w, m_in_w, m_conv_w, m_conv_b, m_dt_bias, m_A_log, m_D, m_norm_w, m_out_w,
                  kv_norm_w, kv_w, k_norm_w, a_in_w, a_q_norm_w, a_out_w, ple_w, ple_gate_w,
                  ple_norm_w):
    row = lambda v: v.reshape(1, -1).astype(F32)
    reps = LANES // SSM_HEADS
    lane_pad = lambda v: jnp.tile(v.astype(F32), reps).reshape(1, LANES)
    mamba = []
    for i in range(N_A_LAYERS):
        mamba.append(dict(
            in_w=m_in_w[i].astype(BF16),
            dt_w=jnp.tile(m_in_w[i][:, D_INNER + CONV_DIM:], (1, reps)).astype(BF16),
            conv_w=m_conv_w[i].astype(F32),
            conv_b=row(m_conv_b[i]),
            dt_bias=lane_pad(m_dt_bias[i]),
            a_log=lane_pad(m_A_log[i]),
            d_exp=row(jnp.repeat(m_D[i], SSM_HEADDIM)),
            norm_w=row(m_norm_w[i]),
            out_w=m_out_w[i].astype(BF16),
        ))
    attn = []
    a_in_bf, a_out_bf = _permute_attn_weights(a_in_w, a_out_w)
    for j in range(DEPTH - N_A_LAYERS):
        attn.append(dict(
            q_norm_w=row(jnp.tile(a_q_norm_w[j], LANES // ATT_HEAD_DIM)),
            out_w=a_out_bf[j],
        ))
    return dict(
        norm_w=[row(norm_w[i]) for i in range(DEPTH)],
        mamba=mamba,
        attn=attn,
        attn_in_w=a_in_bf,
        kv_norm_w=row(kv_norm_w),
        kv_w=kv_w.astype(BF16),
        k_norm_w=row(jnp.tile(k_norm_w, LANES // ATT_HEAD_DIM)),
        ple_w=[ple_w[i].astype(BF16) for i in range(DEPTH)],
        ple_gate_w=[ple_gate_w[i].astype(BF16) for i in range(DEPTH)],
        ple_norm_w=[row(ple_norm_w[i]) for i in range(DEPTH)],
    )


def _constants():
    t = SSD_CHUNK
    tril = jnp.tril(jnp.ones((t, t), F32)).astype(BF16)
    head = jnp.arange(LANES, dtype=jnp.int32)[:, None]
    col = jnp.arange(D_INNER, dtype=jnp.int32)[None, :]
    expand = (head % SSM_HEADS == col // SSM_HEADDIM).astype(BF16)
    li = jnp.arange(LANES, dtype=jnp.int32)
    l2 = jnp.arange(2 * LANES, dtype=jnp.int32)
    bd = ((l2[:, None] // ATT_HEAD_DIM) == (l2[None, :] // ATT_HEAD_DIM)).astype(F32) / ATT_HEAD_DIM
    ccol = jnp.arange(N_DGROUPS * ATT_WIDTH, dtype=jnp.int32)[None, :]
    cgrp, cc = ccol // ATT_WIDTH, ccol % ATT_WIDTH
    slot = cc // LANES + jnp.where(cc % LANES < ATT_HEAD_DIM, 0, ATT_WIDTH // LANES)
    lrow = li[:, None]
    lse_spread = ((lrow >> LSE_SLOT_LOG2 == slot)
                  & ((lrow & (LSE_SLOT_LANES - 1)) >> 1 == cgrp)).astype(BF16)
    return dict(tril=tril, expand=expand, bd=bd.astype(BF16), lse_spread=lse_spread)


_Q_SCALE = ATT_HEAD_DIM ** -0.5
_DILS = tuple(d for _, d in DILATION_GROUPS)
M_IN_COLS = D_INNER + CONV_DIM + SSM_HEADS
MAMBA_SEGS = ((0, D_INNER, False, 1.0, ((0, 0, 0),)),
              (D_INNER, CONV_DIM, False, 1.0, ((1, 0, 0),)),
              (M_IN_COLS, LANES, False, 1.0, ((2, 0, 0),)))
MAMBA_OUTS = ((D_INNER, F32, 0), (CONV_DIM, F32, 0), (LANES, F32, 0))
KV_SEGS_TOK = ((0, KV_HALF, True, 1.0, ((0, 0, 0),)), (KV_HALF, KV_HALF, False, 1.0, ((1, 0, 0),)))
KV_OUTS_TOK = ((KV_HALF, F32, 0), (KV_HALF, F32, 0))
ATTN_SEGS_TOK = ((0, N_DGROUPS * ATT_WIDTH, True, _Q_SCALE, ((0, 0, 0),)),
                 (N_DGROUPS * ATT_WIDTH, ATT_WIDTH, False, 1.0, ((1, 0, 0),)))
ATTN_OUTS_TOK = ((N_DGROUPS * ATT_WIDTH, F32, 0), (ATT_WIDTH, F32, 0))
KV_SEGS_RES = tuple(
    (g * GROUP_KV, GROUP_KV, True, 1.0, ((0, g * GROUP_KV, 0), (2 + g, 0, _DILS[g])))
    for g in range(N_DGROUPS)) + tuple(
    (KV_HALF + g * GROUP_KV, GROUP_KV, False, 1.0, ((1, g * GROUP_KV, 0), (2 + N_DGROUPS + g, 0, _DILS[g])))
    for g in range(N_DGROUPS))
KV_OUTS_RES = KV_OUTS_TOK + tuple((GROUP_KV, BF16, d) for d in _DILS) * 2
ATTN_SEGS_RES = tuple((g * ATT_WIDTH, ATT_WIDTH, True, _Q_SCALE * LOG2E, ((g, 0, _DILS[g]),))
                      for g in range(N_DGROUPS)) + (
    (N_DGROUPS * ATT_WIDTH, ATT_WIDTH, False, 1.0, ((N_DGROUPS, 0, 0),)),)
ATTN_OUTS_RES = tuple((ATT_WIDTH, BF16, d) for d in _DILS) + ((ATT_WIDTH, F32, 0),)


def _trunk(x, p_all, wts, consts, rope_tabs, rope_rows, *, nbatch, seq, ssm0, conv0, caches):
    prompt = caches is None
    m = x.shape[0]
    h = x
    new_ssm, new_conv = [], []
    for i in range(N_A_LAYERS):
        lw = wts["mamba"][i]
        ple = (wts["ple_norm_w"][i], wts["ple_gate_w"][i], wts["ple_w"][i])
        if prompt:
            h, h_fin, xbc_tail = _mamba_layer(h, p_all, i, wts["norm_w"][i], lw, consts, *ple,
                                              nbatch=nbatch, seq=seq)
            new_conv.append(xbc_tail[:, SUBLANES - (CONV_K - 1):])
        else:
            z, xbc, dtp = _norm_proj(h, wts["norm_w"][i], lw["in_w"], MAMBA_SEGS, MAMBA_OUTS,
                                     w_extra=lw["dt_w"])
            y, h_fin = _ssd(z, xbc, dtp, lw, consts, ssm0, conv0, i, nbatch=nbatch,
                            rows_in=seq, nchunks=1, y_dtype=F32)
            new_conv.append(xbc.reshape(nbatch, seq, CONV_DIM)[:, seq - (CONV_K - 1):])
            h = _out_ple([y], "mamba", h, p_all, i, lw["out_w"], *ple)
        new_ssm.append(h_fin.reshape(nbatch, SSM_HEADS, SSM_HEADDIM, D_STATE))
    cos, sin = rope_tabs
    kv_outs = _norm_proj(h, wts["kv_norm_w"], wts["kv_w"],
                         KV_SEGS_RES if prompt else KV_SEGS_TOK,
                         KV_OUTS_RES if prompt else KV_OUTS_TOK,
                         rope_inputs=(wts["k_norm_w"], cos, sin, consts["bd"]), rope_rows=rope_rows,
                         nbatch=nbatch)
    k, v = kv_outs[:2]
    for j in range(DEPTH - N_A_LAYERS):
        i = N_A_LAYERS + j
        aw = wts["attn"][j]
        q_outs = _norm_proj(h, wts["norm_w"][i], wts["attn_in_w"],
                            ATTN_SEGS_RES if prompt else ATTN_SEGS_TOK,
                            ATTN_OUTS_RES if prompt else ATTN_OUTS_TOK,
                            rope_inputs=(aw["q_norm_w"], cos, sin, consts["bd"]), rope_rows=rope_rows,
                            nbatch=nbatch, w_layer=j)
        gate = q_outs[-1]
        if prompt:
            os_, ls_ = [], []
            for gi in range(N_DGROUPS):
                o_g, l_g = _attn_prompt_group(q_outs[gi], kv_outs[2 + gi], kv_outs[2 + N_DGROUPS + gi], gi)
                os_.append(o_g)
                ls_.append(l_g)
            mix_inputs, mode = os_ + ls_ + [gate, consts["lse_spread"]], "attn_merge"
        else:
            o = _attn_sample(q_outs[0], k, v, caches, nbatch, seq)
            mix_inputs, mode = [o, gate], "attn"
        h = _out_ple(mix_inputs, mode, h, p_all, i, aw["out_w"], wts["ple_norm_w"][i],
                     wts["ple_gate_w"][i], wts["ple_w"][i])
    return h, jnp.stack(new_ssm, axis=0), jnp.stack(new_conv, axis=0), k, v


def kernel(x_prompt, x_sample, state_ssm, state_conv, cache_kv_g1, cache_kv_g2, cache_kv_g3,
           p_prompt, p_sample, norm_w, m_in_w, m_conv_w, m_conv_b, m_dt_bias, m_A_log, m_D,
           m_norm_w, m_out_w, kv_norm_w, kv_w, k_norm_w, a_in_w, a_q_norm_w, a_out_w,
           ple_w, ple_gate_w, ple_norm_w):
    wts = _prep_weights(norm_w, m_in_w, m_conv_w, m_conv_b, m_dt_bias, m_A_log, m_D, m_norm_w,
                        m_out_w, kv_norm_w, kv_w, k_norm_w, a_in_w, a_q_norm_w, a_out_w, ple_w,
                        ple_gate_w, ple_norm_w)
    consts = _constants()

    b_p, seq = x_prompt.shape[0], x_prompt.shape[1]
    m_p = b_p * seq
    tabs_p = _rope_tables(jnp.arange(seq, dtype=jnp.int32))
    y_p, ssm_p, conv_p, k_p, v_p = _trunk(
        x_prompt.reshape(m_p, D_MODEL), p_prompt.reshape(DEPTH, m_p, PLE_DIM), wts, consts,
        tabs_p, seq, nbatch=b_p, seq=seq, ssm0=None, conv0=None, caches=None)

    b_s, dec = x_sample.shape[0], x_sample.shape[1]
    m_s = b_s * dec
    pos_s = PAST_LEN + jnp.arange(dec, dtype=jnp.int32)
    tabs_s = tuple(jnp.tile(t, (b_s, 1)) for t in _rope_tables(pos_s))
    ssm0_s = state_ssm.reshape(N_A_LAYERS, b_s, D_INNER, D_STATE)
    conv0_s = jnp.pad(state_conv, ((0, 0), (0, 0), (SUBLANES - (CONV_K - 1), 0), (0, 0)))
    caches = tuple(jnp.transpose(cch, (0, 2, 3, 4, 1)).reshape(b_s, 2 * GROUP_KV, cch.shape[1])
                   for cch in (cache_kv_g1, cache_kv_g2, cache_kv_g3))
    y_s, ssm_s, conv_s, k_s, v_s = _trunk(
        x_sample.reshape(m_s, D_MODEL), p_sample.reshape(DEPTH, m_s, PLE_DIM), wts, consts,
        tabs_s, m_s, nbatch=b_s, seq=dec, ssm0=ssm0_s, conv0=conv0_s, caches=caches)

    def kv_out(k, v, nbatch, length, gi, keep):
        def tail(a):
            a = a.reshape(nbatch, length, KV_HALF)[:, length - keep:, gi * GROUP_KV:(gi + 1) * GROUP_KV]
            return a.reshape(nbatch, keep, ATT_KV_HEADS, ATT_HEAD_DIM)
        return jnp.stack([tail(k), tail(v)], axis=2)

    kv_p = [kv_out(k_p, v_p, b_p, seq, gi, min(w, seq)) for gi, (w, _) in enumerate(DILATION_GROUPS)]
    kv_s = [kv_out(k_s, v_s, b_s, dec, gi, dec) for gi in range(N_DGROUPS)]
    return (y_p.reshape(b_p, seq, D_MODEL), y_s.reshape(b_s, dec, D_MODEL),
            ssm_p, conv_p, ssm_s, conv_s, kv_p[0], kv_p[1], kv_p[2], kv_s[0], kv_s[1], kv_s[2])
```

```python
import functools

import jax
import jax.numpy as jnp
from jax import lax
from jax.experimental import pallas as pl
from jax.experimental.pallas import tpu as pltpu

F32 = jnp.float32
BF16 = jnp.bfloat16

D_MODEL = 1024
SEQ = 8192
DEPTH = 4
PAST_LEN = 8192
N_A_LAYERS = DEPTH // 2
D_INNER = 2048
SSM_HEADDIM = 64
SSM_HEADS = 32
SSM_GROUPS = 4
D_STATE = 128
CONV_K = 4
BC_DIM = 2 * SSM_GROUPS * D_STATE
CONV_DIM = D_INNER + BC_DIM
SSD_CHUNK = 128
ATT_HEAD_DIM = 64
ATT_HEADS = 16
ATT_KV_HEADS = 4
DILATION_GROUPS = ((128, 1), (512, 4), (2048, 16))
N_DGROUPS = 3
ATT_WIDTH = 1024
KV_HALF = N_DGROUPS * ATT_KV_HEADS * ATT_HEAD_DIM
GROUP_KV = ATT_KV_HEADS * ATT_HEAD_DIM
N_KEYS = 129
ROPE_THETA = 10000.0
PLE_DIM = 256
EPS = 1e-6
SSM_HEADS_LOG2 = 5
ROPE_HALF = ATT_HEAD_DIM // 2
LSE_SLOT_LANES = 8
LSE_SLOT_LOG2 = 3
LOG2E = 1.4426950408889634
LN2 = 0.6931471805599453

LANES = 128
SUBLANES = 8
VMEM_LIMIT_BYTES = 56 * 1024 * 1024
ROW_TILE = 512
NEG_BIG = -1e30
RELAYOUT_SLOTS = 4
ATTN_Q_BLOCKS = 4
SAMPLE_SEQS_PER_STEP = 2
MAMBA_ROWS = 256
MAMBA_PROJ_PIECE = 256

HEAD_ORDER = (0, 4, 1, 5, 2, 6, 3, 7, 8, 12, 9, 13, 10, 14, 11, 15)


def _cparams(semantics):
    return pltpu.CompilerParams(dimension_semantics=semantics,
                                vmem_limit_bytes=VMEM_LIMIT_BYTES)


def _const_spec(shape):
    nd = len(shape)
    return pl.BlockSpec(shape, lambda *_: (0,) * nd, pipeline_mode=pl.Buffered(1))


def _split_bf16(x, parts):
    out = []
    rem = x
    for _ in range(parts):
        hi = rem.astype(BF16)
        out.append(hi)
        rem = rem - hi.astype(F32)
    return out


def _dot(a, b):
    return jnp.dot(a, b, preferred_element_type=F32)


def _dot_nt(a, b):
    return lax.dot_general(a, b, (((1,), (1,)), ((), ())), preferred_element_type=F32)


def _sigmoid(x):
    return 0.5 + 0.5 * jnp.tanh(0.5 * x)


def _silu(x):
    hx = 0.5 * x
    return hx + hx * jnp.tanh(hx)


def _lo_mask(rows):
    return lax.broadcasted_iota(jnp.int32, (rows, LANES), 1) < ATT_HEAD_DIM


def _rope_partner(v):
    lane = lax.broadcasted_iota(jnp.int32, v.shape, 1)
    return jnp.where((lane & ROPE_HALF) == 0, pltpu.roll(v, LANES - ROPE_HALF, 1), pltpu.roll(v, ROPE_HALF, 1))


def _head_norm_rope(y, ms, hw, cos, sin, scale):
    yn = y * lax.rsqrt(ms + EPS) * (hw * scale)
    return yn * cos + _rope_partner(yn) * sin


def _norm_proj_kernel(*refs, segs, use_rope, n_out, tm, has_extra):
    h_ref, nw_ref, w_ref = refs[:3]
    pos = 3
    if has_extra:
        wx_ref = refs[pos]
        pos += 1
    if use_rope:
        hw_ref, cos_ref, sin_ref, bd_ref = refs[pos:pos + 4]
        pos += 4
    n_main = w_ref.shape[1]
    out_refs = refs[pos:pos + n_out]
    scr = refs[pos + n_out] if len(refs) > pos + n_out else None
    x = h_ref[...]
    ms = jnp.mean(x * x, axis=-1, keepdims=True)
    u = (x * lax.rsqrt(ms + EPS) * nw_ref[...]).astype(BF16)
    slot = 0
    for start, width, rope, scale, sinks in segs:
        if start >= n_main:
            acc = _dot(u, wx_ref[:, start - n_main:start - n_main + width])
        else:
            acc = _dot(u, w_ref[:, start:start + width])
        if not rope and all(dil == 0 for _, _, dil in sinks):
            for oi, col_off, _ in sinks:
                out_refs[oi][:, col_off:col_off + width] = acc.astype(out_refs[oi].dtype)
            continue
        for cb in range(width // LANES):
            val = acc[:, cb * LANES:(cb + 1) * LANES]
            if rope:
                if cb % 2 == 0:
                    y2 = acc[:, cb * LANES:(cb + 2) * LANES]
                    ms2 = _dot((y2 * y2).astype(BF16), bd_ref[...])
                val = _head_norm_rope(val, ms2[:, (cb % 2) * LANES:(cb % 2 + 1) * LANES],
                                      hw_ref[...], cos_ref[...], sin_ref[...], scale)
            for oi, col_off, dil in sinks:
                o_ref = out_refs[oi]
                cs = slice(col_off + cb * LANES, col_off + (cb + 1) * LANES)
                if dil == 0:
                    o_ref[:, cs] = val.astype(o_ref.dtype)
                elif dil == 1:
                    o_ref[0, :, cs] = val.astype(o_ref.dtype)
                else:
                    s = slot % RELAYOUT_SLOTS
                    slot += 1
                    scr[s] = val
                    for r in range(dil):
                        o_ref[r, :, cs] = scr[s, pl.ds(r, tm // dil, stride=dil), :].astype(o_ref.dtype)


def _norm_proj(h, nw, w, segs, out_defs, rope_inputs=None, rope_rows=None, nbatch=None,
               w_layer=None, w_extra=None):
    m = h.shape[0]
    tm = min(ROW_TILE, m)
    use_rope = rope_inputs is not None
    if w.ndim == 3:
        w_spec = pl.BlockSpec((None,) + w.shape[1:], lambda i: (w_layer, 0, 0), pipeline_mode=pl.Buffered(1))
    else:
        w_spec = _const_spec(w.shape)
    in_specs = [pl.BlockSpec((tm, D_MODEL), lambda i: (i, 0)),
                _const_spec((1, D_MODEL)),
                w_spec]
    args = [h, nw, w]
    if w_extra is not None:
        in_specs.append(_const_spec(w_extra.shape))
        args.append(w_extra)
    if use_rope:
        hw, cos, sin, bd = rope_inputs
        nblk = rope_rows // tm
        in_specs += [_const_spec((1, LANES)),
                     pl.BlockSpec((tm, LANES), lambda i: (i % nblk, 0)),
                     pl.BlockSpec((tm, LANES), lambda i: (i % nblk, 0)),
                     _const_spec((2 * LANES, 2 * LANES))]
        args += [hw, cos, sin, bd]
    out_shape, out_specs = [], []
    for width, dt, dil in out_defs:
        if dil == 0:
            out_shape.append(jax.ShapeDtypeStruct((m, width), dt))
            out_specs.append(pl.BlockSpec((tm, width), lambda i: (i, 0)))
        else:
            tiles = m // nbatch // tm
            out_shape.append(jax.ShapeDtypeStruct((nbatch, dil, m // nbatch // dil, width), dt))
            out_specs.append(pl.BlockSpec((None, dil, tm // dil, width),
                                          lambda i, tiles=tiles: (i // tiles, 0, i % tiles, 0)))
    scratch = []
    if any(dil > 1 for _, _, dil in out_defs):
        scratch.append(pltpu.VMEM((RELAYOUT_SLOTS, tm, LANES), F32))
    return pl.pallas_call(
        functools.partial(_norm_proj_kernel, segs=segs, use_rope=use_rope, n_out=len(out_defs), tm=tm,
                          has_extra=w_extra is not None),
        grid=(m // tm,),
        in_specs=in_specs,
        out_specs=tuple(out_specs),
        out_shape=tuple(out_shape),
        scratch_shapes=scratch,
        compiler_params=_cparams(("parallel",)),
        name="norm_proj",
    )(*args)


def _ssd_kernel(z_ref, xbc_ref, dt_ref, cw_ref, cb_ref, dtb_ref, alog_ref, dexp_ref, nw_ref,
                tril_ref, e_ref, h0_ref, c0_ref, *rest, rows_in, nchunks, stack_prev):
    if stack_prev:
        prev_ref, y_ref, hout_ref, ht_scr, xpad_scr = rest
    else:
        y_ref, hout_ref, ht_scr, xpad_scr = rest
    q = SSD_CHUNK
    c = pl.program_id(1)

    @pl.when(c == 0)
    def _():
        ht_scr[...] = jnp.transpose(h0_ref[...])
        xpad_scr[...] = c0_ref[...]

    def pad_rows(v):
        if rows_in == q:
            return v
        return jnp.concatenate([v, jnp.zeros((q - rows_in, v.shape[1]), v.dtype)], axis=0)

    y = _ssd_chunk(pad_rows(z_ref[...]), pad_rows(xbc_ref[...]), pad_rows(dt_ref[...]),
                   cw_ref, cb_ref, dtb_ref, alog_ref, dexp_ref, nw_ref, tril_ref, e_ref,
                   ht_scr, xpad_scr, rows_in)
    y_ref[...] = y[:rows_in].astype(y_ref.dtype)

    @pl.when(c == nchunks - 1)
    def _():
        if stack_prev:
            nprev = prev_ref.shape[0]
            hout_ref[0:nprev] = prev_ref[...]
            hout_ref[nprev] = jnp.transpose(ht_scr[...])
        else:
            hout_ref[...] = jnp.transpose(ht_scr[...])


def _ssd_chunk(zz, xbc, dt_raw, cw_ref, cb_ref, dtb_ref, alog_ref, dexp_ref, nw_ref, tril_ref, e_ref,
               ht_scr, xpad_scr, rows_valid, side=None):
    q = SSD_CHUNK
    gw = D_INNER // SSM_GROUPS

    def run_side(n=1):
        for _ in range(n):
            if side:
                side.pop(0)()

    tail = xpad_scr[...]
    xbc3 = xbc.reshape(q // SUBLANES, SUBLANES, CONV_DIM)
    row8 = lax.broadcasted_iota(jnp.int32, (1, SUBLANES, CONV_DIM), 1)
    conv = cb_ref[...] + cw_ref[CONV_K - 1:CONV_K, :] * xbc
    for k in range(CONV_K - 1):
        sh = CONV_K - 1 - k
        rot3 = pltpu.roll(xbc3, sh, 1)
        prev3 = jnp.concatenate([pltpu.roll(tail, sh, 0)[None], rot3[:-1]], axis=0)
        shifted = jnp.where(row8 < sh, prev3, rot3).reshape(q, CONV_DIM)
        conv = conv + cw_ref[k:k + 1, :] * shifted
    xpad_scr[...] = xbc[q - SUBLANES:q]
    run_side()
    act = _silu(conv)
    xs = act[:, :D_INNER]
    bm = act[:, D_INNER:D_INNER + SSM_GROUPS * D_STATE]
    cm = act[:, D_INNER + SSM_GROUPS * D_STATE:]

    dtr = dt_raw + dtb_ref[...]
    dt = jnp.maximum(dtr, 0.0) + jnp.log1p(jnp.exp(-jnp.abs(dtr)))
    if rows_valid < q:
        row = lax.broadcasted_iota(jnp.int32, dt.shape, 0)
        dt = jnp.where(row < rows_valid, dt, 0.0)
    a = dt * (-LOG2E * jnp.exp(alog_ref[...]))
    tril = tril_ref[...]
    acum = sum(_dot(tril, part) for part in _split_bf16(a, 3))
    acum_t = jnp.transpose(acum)

    e = e_ref[...]
    grp = lax.broadcasted_iota(jnp.int32, (q, LANES), 1) >> SSM_HEADS_LOG2
    zero_l = jnp.zeros((q, LANES), BF16)

    def packed(parts):
        out = zero_l
        for idx, part in enumerate(parts):
            out = jnp.where(grp == idx, part, out)
        return out

    dt_e = _dot(packed(_split_bf16(dt, 2)), e)
    acum_e = _dot(packed(_split_bf16(acum, 3)), e)
    alast_e = acum_e[q - 1:q, :]
    exp_acum_e = jnp.exp2(acum_e)
    decay_end_e = jnp.exp2(alast_e - acum_e)
    chunk_decay_e = jnp.exp2(alast_e)

    xdt = xs * dt_e
    xdt_bf = xdt.astype(BF16)
    xdtw_bf = (xdt * decay_end_e).astype(BF16)

    li = lax.broadcasted_iota(jnp.int32, (q, q), 0)
    si = lax.broadcasted_iota(jnp.int32, (q, q), 1)
    causal = li >= si
    lo = _lo_mask(q)
    zero_bf = jnp.zeros((q, LANES), BF16)

    y_parts = []
    for g in range(SSM_GROUPS):
        gs = slice(g * gw, (g + 1) * gw)
        bg = bm[:, g * D_STATE:(g + 1) * D_STATE]
        cg_bf = cm[:, g * D_STATE:(g + 1) * D_STATE].astype(BF16)
        cb = jnp.where(causal, _dot_nt(cg_bf, bg.astype(BF16)), 0.0)
        htg = ht_scr[:, gs]
        y_off = _dot(cg_bf, htg.astype(BF16)) * exp_acum_e[:, gs]
        blocks = []
        for j in range(gw // LANES):
            hd = g * (SSM_HEADS // SSM_GROUPS) + 2 * j
            xpair = xdt_bf[:, hd * SSM_HEADDIM:hd * SSM_HEADDIM + LANES]
            mats = []
            for hh in (hd, hd + 1):
                seg = acum[:, hh:hh + 1] - acum_t[hh:hh + 1, :]
                mats.append((cb * jnp.exp2(jnp.minimum(seg, 0.0))).astype(BF16))
            xstack = jnp.concatenate([jnp.where(lo, xpair, zero_bf), jnp.where(lo, zero_bf, xpair)], axis=0)
            blocks.append(_dot(jnp.concatenate(mats, axis=1), xstack))
            run_side()
        y_diag = jnp.concatenate(blocks, axis=1)
        bgt_bf = jnp.transpose(bg).astype(BF16)
        st = _dot(bgt_bf, xdtw_bf[:, gs])
        ht_scr[:, gs] = htg * chunk_decay_e[:, gs] + st
        y_parts.append(y_diag + y_off)

    y = jnp.concatenate(y_parts, axis=1) + xs * dexp_ref[...]
    y = y * _silu(zz)
    run_side()
    normed = []
    for g in range(SSM_GROUPS):
        yg = y[:, g * gw:(g + 1) * gw]
        ms = jnp.mean(yg * yg, axis=-1, keepdims=True)
        normed.append(yg * lax.rsqrt(ms + EPS))
    return jnp.concatenate(normed, axis=1) * nw_ref[...]


def _ssd(z, xbc, dtp, lw, consts, h0, c0, layer, *, nbatch, rows_in, nchunks, y_dtype, prev_states=None):
    m = z.shape[0]
    row_map = lambda b, c: (b * nchunks + c, 0)
    batch_map = lambda b, c: (b, 0, 0)
    state_map = lambda b, c: (layer, b, 0, 0)
    in_specs = [
        pl.BlockSpec((rows_in, D_INNER), row_map),
        pl.BlockSpec((rows_in, CONV_DIM), row_map),
        pl.BlockSpec((rows_in, LANES), row_map),
        _const_spec((CONV_K, CONV_DIM)),
        _const_spec((1, CONV_DIM)),
        _const_spec((1, LANES)),
        _const_spec((1, LANES)),
        _const_spec((1, D_INNER)),
        _const_spec((1, D_INNER)),
        _const_spec((SSD_CHUNK, SSD_CHUNK)),
        _const_spec((LANES, D_INNER)),
        pl.BlockSpec((None, None, D_INNER, D_STATE), state_map),
        pl.BlockSpec((None, None, SUBLANES, CONV_DIM), state_map),
    ]
    args = [z, xbc, dtp, lw["conv_w"], lw["conv_b"], lw["dt_bias"], lw["a_log"], lw["d_exp"],
            lw["norm_w"], consts["tril"], consts["expand"], h0, c0]
    if prev_states is None:
        state_spec = pl.BlockSpec((None, D_INNER, D_STATE), batch_map)
        state_shape = (nbatch, D_INNER, D_STATE)
    else:
        nprev = prev_states.shape[0]
        stack_map = lambda b, c: (0, b, 0, 0)
        in_specs.append(pl.BlockSpec((nprev, None, D_INNER, D_STATE), stack_map))
        args.append(prev_states)
        state_spec = pl.BlockSpec((nprev + 1, None, D_INNER, D_STATE), stack_map)
        state_shape = (nprev + 1, nbatch, D_INNER, D_STATE)
    out_specs = (pl.BlockSpec((rows_in, D_INNER), row_map), state_spec)
    out_shape = (jax.ShapeDtypeStruct((m, D_INNER), y_dtype),
                 jax.ShapeDtypeStruct(state_shape, F32))
    return pl.pallas_call(
        functools.partial(_ssd_kernel, rows_in=rows_in, nchunks=nchunks,
                          stack_prev=prev_states is not None),
        grid=(nbatch, nchunks),
        in_specs=in_specs,
        out_specs=out_specs,
        out_shape=out_shape,
        scratch_shapes=[pltpu.VMEM((D_STATE, D_INNER), F32),
                        pltpu.VMEM((SUBLANES, CONV_DIM), F32)],
        compiler_params=_cparams(("parallel", "arbitrary")),
        name="ssd",
    )(*args)


def _mamba_layer_kernel(ha_ref, hc_ref, p_ref, nw_ref, win_ref, wdt_ref, cw_ref, cb_ref, dtb_ref, alog_ref,
                        dexp_ref, mnw_ref, tril_ref, e_ref, wout_ref, pnw_ref, gw_ref, pw_ref,
                        out_ref, sout_ref, cout_ref,
                        proj0, proj1, y0, y1, ht_scr, xpad_scr, *, nsteps, rows):
    s = pl.program_id(0)
    q = SSD_CHUNK

    @pl.when(s == 0)
    def _():
        proj1[...] = jnp.zeros(proj1.shape, proj1.dtype)
        y0[...] = jnp.zeros(y0.shape, y0.dtype)
        y1[...] = jnp.zeros(y1.shape, y1.dtype)

    @pl.when((s == 0) | (s % nsteps == 1))
    def _():
        ht_scr[...] = jnp.zeros(ht_scr.shape, ht_scr.dtype)
        xpad_scr[...] = jnp.zeros(xpad_scr.shape, xpad_scr.dtype)

    def stages(proj_w, proj_r, y_w, y_r):
        vals = {}

        def a_norm():
            x = ha_ref[...]
            ms = jnp.mean(x * x, axis=-1, keepdims=True)
            vals["u"] = (x * lax.rsqrt(ms + EPS) * nw_ref[...]).astype(BF16)

        def a_piece(c0, c1):
            def emit():
                proj_w[:, c0:c1] = _dot(vals["u"], win_ref[:, c0:c1])
            return emit

        def c_out(c0, c1):
            def emit():
                vals["h1", c0] = hc_ref[:, c0:c1] + _dot(y_r[...], wout_ref[:, c0:c1])
            return emit

        def c_norm():
            h1 = jnp.concatenate([vals["h1", c0] for c0 in c_cols], axis=1)
            vals["h1"] = h1
            ms = jnp.mean(h1 * h1, axis=-1, keepdims=True)
            vals["un"] = (h1 * lax.rsqrt(ms + EPS) * pnw_ref[...]).astype(BF16)

        def c_gate(c0, c1):
            def emit():
                gate = _sigmoid(_dot(vals["un"], gw_ref[:, c0:c1]))
                pe = _dot(p_ref[...].astype(BF16), pw_ref[:, c0:c1])
                out_ref[:, c0:c1] = vals["h1"][:, c0:c1] + gate * pe
            return emit

        def a_dt():
            proj_w[:, n_main:] = _dot(vals["u"], wdt_ref[...])

        n_main = D_INNER + CONV_DIM
        step = MAMBA_PROJ_PIECE
        a_work = [a_piece(c0, c0 + step) for c0 in range(0, n_main, step)] + [a_dt]
        c_cols = list(range(0, D_MODEL, step))
        c_work = ([c_out(c0, c0 + step) for c0 in c_cols] + [c_norm]
                  + [c_gate(c0, c0 + step) for c0 in c_cols])
        side = [a_norm]
        while a_work or c_work:
            if c_work:
                side.append(c_work.pop(0))
            if a_work:
                side.append(a_work.pop(0))
            if a_work:
                side.append(a_work.pop(0))
        for j in range(rows // q):
            rs = slice(j * q, (j + 1) * q)
            y = _ssd_chunk(proj_r[rs, 0:D_INNER], proj_r[rs, D_INNER:D_INNER + CONV_DIM],
                           proj_r[rs, D_INNER + CONV_DIM:], cw_ref, cb_ref, dtb_ref, alog_ref,
                           dexp_ref, mnw_ref, tril_ref, e_ref, ht_scr, xpad_scr, q, side=side)
            y_w[rs, :] = y.astype(y_w.dtype)
        while side:
            side.pop(0)()

    @pl.when(s % 2 == 0)
    def _():
        stages(proj0, proj1, y1, y0)

    @pl.when(s % 2 == 1)
    def _():
        stages(proj1, proj0, y0, y1)

    @pl.when((s > 0) & (s % nsteps == 0))
    def _():
        sout_ref[...] = jnp.transpose(ht_scr[...])
        cout_ref[...] = xpad_scr[...]


def _mamba_layer(h, p_all, layer, nw, lw, consts, pnw, gw, pw, *, nbatch, seq):
    m = h.shape[0]
    rows = MAMBA_ROWS
    nsteps = seq // rows
    n_proj = D_INNER + CONV_DIM + LANES
    total = nbatch * nsteps
    a_map = lambda s: (jnp.minimum(s, total - 1), 0)
    c_map = lambda s: (jnp.clip(s - 2, 0, total - 1), 0)
    seq_map = lambda s: (jnp.clip((s - 1) // nsteps, 0, nbatch - 1), 0, 0)
    in_specs = [
        pl.BlockSpec((rows, D_MODEL), a_map),
        pl.BlockSpec((rows, D_MODEL), c_map),
        pl.BlockSpec((None, rows, PLE_DIM), lambda s: (layer, jnp.clip(s - 2, 0, total - 1), 0)),
        _const_spec((1, D_MODEL)),
        _const_spec(lw["in_w"].shape),
        _const_spec((D_MODEL, LANES)),
        _const_spec((CONV_K, CONV_DIM)),
        _const_spec((1, CONV_DIM)),
        _const_spec((1, LANES)),
        _const_spec((1, LANES)),
        _const_spec((1, D_INNER)),
        _const_spec((1, D_INNER)),
        _const_spec((SSD_CHUNK, SSD_CHUNK)),
        _const_spec((LANES, D_INNER)),
        _const_spec((D_INNER, D_MODEL)),
        _const_spec((1, D_MODEL)),
        _const_spec((D_MODEL, D_MODEL)),
        _const_spec((PLE_DIM, D_MODEL)),
    ]
    out_specs = (pl.BlockSpec((rows, D_MODEL), c_map),
                 pl.BlockSpec((None, D_INNER, D_STATE), seq_map),
                 pl.BlockSpec((None, SUBLANES, CONV_DIM), seq_map))
    out_shape = (jax.ShapeDtypeStruct((m, D_MODEL), F32),
                 jax.ShapeDtypeStruct((nbatch, D_INNER, D_STATE), F32),
                 jax.ShapeDtypeStruct((nbatch, SUBLANES, CONV_DIM), F32))
    return pl.pallas_call(
        functools.partial(_mamba_layer_kernel, nsteps=nsteps, rows=rows),
        grid=(total + 2,),
        in_specs=in_specs,
        out_specs=out_specs,
        out_shape=out_shape,
        scratch_shapes=[pltpu.VMEM((rows, n_proj), F32), pltpu.VMEM((rows, n_proj), F32),
                        pltpu.VMEM((rows, D_INNER), BF16), pltpu.VMEM((rows, D_INNER), BF16),
                        pltpu.VMEM((D_STATE, D_INNER), F32),
                        pltpu.VMEM((SUBLANES, CONV_DIM), F32)],
        compiler_params=_cparams(("arbitrary",)),
        name="mamba_layer",
    )(h, h, p_all, nw, lw["in_w"], lw["dt_w"], lw["conv_w"], lw["conv_b"], lw["dt_bias"], lw["a_log"],
      lw["d_exp"], lw["norm_w"], consts["tril"], consts["expand"], lw["out_w"], pnw, gw, pw)


def _attn_prompt_kernel(q_ref, kcur_ref, kprev_ref, vcur_ref, vprev_ref, o_ref, l_ref, *, nq):
    i = pl.program_id(2)
    t = SSD_CHUNK
    npair = GROUP_KV // ATT_HEAD_DIM
    qi = lax.broadcasted_iota(jnp.int32, (t, 2 * t), 0)
    kk = lax.broadcasted_iota(jnp.int32, (t, 2 * t), 1)
    band = (kk >= qi) & (kk <= qi + (N_KEYS - 1))
    bias_inner = jnp.where(band, 0.0, NEG_BIG)
    bias_first = jnp.where(band & ((kk >= t) | (i > 0)), 0.0, NEG_BIG)
    lo = _lo_mask(t)
    lo2 = _lo_mask(2 * t)
    zero_v = jnp.zeros((2 * t, LANES), BF16)
    ones_stack = jnp.concatenate([jnp.where(lo2, 1.0, 0.0), jnp.where(lo2, 0.0, 1.0)],
                                 axis=0).astype(BF16)
    lane = lax.broadcasted_iota(jnp.int32, (t, LANES), 1)
    lane_slot = (lane & (ATT_HEAD_DIM - 1)) >> LSE_SLOT_LOG2
    for s in range(nq):
        rows = slice(s * t, (s + 1) * t)
        bias = bias_first if s == 0 else bias_inner
        l_c = jnp.zeros((t, LANES), F32)
        for kc in range(GROUP_KV // LANES):
            ks = slice(kc * LANES, (kc + 1) * LANES)
            if s == 0:
                k2 = jnp.concatenate([kprev_ref[:, ks], kcur_ref[0:t, ks]], axis=0)
                v2 = jnp.concatenate([vprev_ref[:, ks], vcur_ref[0:t, ks]], axis=0)
            else:
                k2 = kcur_ref[(s - 1) * t:(s + 1) * t, ks]
                v2 = vcur_ref[(s - 1) * t:(s + 1) * t, ks]
            k_stack = jnp.concatenate([jnp.where(lo2, k2, zero_v), jnp.where(lo2, zero_v, k2)], axis=0)
            v_stack = jnp.concatenate([jnp.where(lo2, v2, zero_v), jnp.where(lo2, zero_v, v2)], axis=0)
            rhs = jnp.concatenate([v_stack, ones_stack], axis=1)
            lhs = jnp.concatenate(
                [q_ref[rows, (kc * npair + pb) * LANES:(kc * npair + pb + 1) * LANES] for pb in range(npair)],
                axis=0)
            sc = _dot_nt(lhs, k_stack).reshape(npair, t, 4 * t)
            sc_a = sc[:, :, :2 * t] + bias[None]
            sc_b = sc[:, :, 2 * t:] + bias[None]
            mx_a = jnp.max(sc_a, axis=-1, keepdims=True)
            mx_b = jnp.max(sc_b, axis=-1, keepdims=True)
            p_cat = jnp.concatenate([jnp.exp2(sc_a - mx_a), jnp.exp2(sc_b - mx_b)], axis=-1)
            res = _dot(p_cat.astype(BF16).reshape(npair * t, 4 * t), rhs)
            for pb in range(npair):
                blk = kc * npair + pb
                num = res[pb * t:(pb + 1) * t, :LANES]
                den = res[pb * t:(pb + 1) * t, LANES:]
                o_ref[rows, blk * LANES:(blk + 1) * LANES] = (num / den).astype(o_ref.dtype)
                lse = jnp.where(lo, mx_a[pb], mx_b[pb]) * LN2 + jnp.log(den)
                l_c = jnp.where(lane_slot == blk, lse, l_c)
        l_ref[rows, :] = l_c


def _attn_prompt_group(q, k, v, gi):
    nbatch, dil, rows, _ = q.shape
    t = SSD_CHUNK
    nq = ATTN_Q_BLOCKS
    cur = lambda b, r, i: (b, r, i, 0)
    prev = lambda b, r, i: (b, r, jnp.maximum(nq * i - 1, 0), 0)
    return pl.pallas_call(
        functools.partial(_attn_prompt_kernel, nq=nq),
        grid=(nbatch, dil, rows // (nq * t)),
        in_specs=[pl.BlockSpec((None, None, nq * t, ATT_WIDTH), cur),
                  pl.BlockSpec((None, None, nq * t, GROUP_KV), cur),
                  pl.BlockSpec((None, None, t, GROUP_KV), prev),
                  pl.BlockSpec((None, None, nq * t, GROUP_KV), cur),
                  pl.BlockSpec((None, None, t, GROUP_KV), prev)],
        out_specs=(pl.BlockSpec((None, None, nq * t, ATT_WIDTH), cur),
                   pl.BlockSpec((None, None, nq * t, LANES), cur)),
        out_shape=(jax.ShapeDtypeStruct((nbatch, dil, rows, ATT_WIDTH), BF16),
                   jax.ShapeDtypeStruct((nbatch, dil, rows, LANES), F32)),
        compiler_params=_cparams(("parallel", "parallel", "arbitrary")),
        name=f"attn_prompt_g{gi}",
    )(q, k, k, v, v)


def _attn_sample_kernel(q_ref, kn_ref, vn_ref, c1_ref, c2_ref, c3_ref, o_ref, *, dec_seq, nseq):
    for bb in range(nseq):
        rs = slice(bb * dec_seq, (bb + 1) * dec_seq)
        _attn_sample_one(q_ref.at[rs], kn_ref.at[rs], vn_ref.at[rs],
                         (c1_ref.at[bb], c2_ref.at[bb], c3_ref.at[bb]), o_ref.at[rs], dec_seq)


def _attn_sample_one(q_ref, kn_ref, vn_ref, caches, o_ref, dec_seq):
    nrow = 2 * ATT_KV_HEADS * dec_seq
    lo_row = _lo_mask(dec_seq)
    pad = LANES - dec_seq
    outs = [[None] * N_DGROUPS for _ in range(ATT_WIDTH // LANES)]
    lses = [[None] * N_DGROUPS for _ in range(ATT_WIDTH // LANES)]
    for gi, (_, dil) in enumerate(DILATION_GROUPS):
        cref = caches[gi]
        w = cref.shape[1]
        tq = lax.broadcasted_iota(jnp.int32, (nrow, w), 0) & (dec_seq - 1)
        rho = lax.broadcasted_iota(jnp.int32, (nrow, w), 1)
        delta = w + tq - rho
        valid_c = (rho >= tq) & ((delta & (dil - 1)) == 0) & (delta <= (N_KEYS - 1) * dil)
        bias_c = jnp.where(valid_c, 0.0, NEG_BIG)
        tq_n = lax.broadcasted_iota(jnp.int32, (nrow, LANES), 0) & (dec_seq - 1)
        tn = lax.broadcasted_iota(jnp.int32, (nrow, LANES), 1)
        valid_n = (tn <= tq_n) & (((tq_n - tn) & (dil - 1)) == 0)
        bias_n = jnp.where(valid_n, 0.0, NEG_BIG)
        gsl = slice(gi * GROUP_KV, (gi + 1) * GROUP_KV)
        k_new = jnp.concatenate([kn_ref[:, gsl], jnp.zeros((pad, GROUP_KV), F32)], axis=0)
        v_new = jnp.concatenate([vn_ref[:, gsl], jnp.zeros((pad, GROUP_KV), F32)], axis=0)
        for kc in range(GROUP_KV // LANES):
            ks = slice(kc * LANES, (kc + 1) * LANES)
            rows = []
            for pb in range(4):
                blk = kc * 4 + pb
                q2 = q_ref[:, gi * ATT_WIDTH + blk * LANES:gi * ATT_WIDTH + (blk + 1) * LANES]
                rows.append(jnp.where(lo_row, q2, 0.0))
                rows.append(jnp.where(lo_row, 0.0, q2))
            lhs = jnp.concatenate(rows, axis=0).astype(BF16)
            kcache_t = cref[ks, :].astype(BF16)
            vcache_t = cref[GROUP_KV + kc * LANES:GROUP_KV + (kc + 1) * LANES, :].astype(BF16)
            s_c = _dot(lhs, kcache_t) + bias_c
            s_n = _dot_nt(lhs, k_new[:, ks].astype(BF16)) + bias_n
            mx = jnp.maximum(jnp.max(s_c, axis=-1, keepdims=True),
                             jnp.max(s_n, axis=-1, keepdims=True))
            p_c = jnp.exp(s_c - mx)
            p_n = jnp.exp(s_n - mx)
            den = jnp.sum(p_c, axis=-1, keepdims=True) + jnp.sum(p_n, axis=-1, keepdims=True)
            num = _dot_nt(p_c.astype(BF16), vcache_t) + _dot(p_n.astype(BF16), v_new[:, ks].astype(BF16))
            on = num / den
            lse = mx + jnp.log(den)
            for pb in range(4):
                blk = kc * 4 + pb
                r0 = pb * 2 * dec_seq
                r1 = r0 + dec_seq
                outs[blk][gi] = jnp.where(lo_row, on[r0:r1], on[r1:r1 + dec_seq])
                lses[blk][gi] = jnp.where(lo_row, lse[r0:r1], lse[r1:r1 + dec_seq])
    for blk in range(ATT_WIDTH // LANES):
        ls = lses[blk]
        mx = jnp.maximum(jnp.maximum(ls[0], ls[1]), ls[2])
        ws = [jnp.exp(l - mx) for l in ls]
        tot = ws[0] + ws[1] + ws[2]
        o = (outs[blk][0] * ws[0] + outs[blk][1] * ws[1] + outs[blk][2] * ws[2]) / tot
        o_ref[:, blk * LANES:(blk + 1) * LANES] = o


def _attn_sample(q, k, v, caches, nbatch, dec_seq):
    m = q.shape[0]
    nseq = SAMPLE_SEQS_PER_STEP
    rows = nseq * dec_seq
    row_map = lambda b: (b, 0)
    in_specs = [pl.BlockSpec((rows, N_DGROUPS * ATT_WIDTH), row_map),
                pl.BlockSpec((rows, KV_HALF), row_map),
                pl.BlockSpec((rows, KV_HALF), row_map)]
    for cch in caches:
        in_specs.append(pl.BlockSpec((nseq, 2 * GROUP_KV, cch.shape[2]), lambda b: (b, 0, 0)))
    return pl.pallas_call(
        functools.partial(_attn_sample_kernel, dec_seq=dec_seq, nseq=nseq),
        grid=(nbatch // nseq,),
        in_specs=in_specs,
        out_specs=pl.BlockSpec((rows, ATT_WIDTH), row_map),
        out_shape=jax.ShapeDtypeStruct((m, ATT_WIDTH), F32),
        compiler_params=_cparams(("parallel",)),
        name="attn_sample",
    )(q, k, v, *caches)


def _token_order(ref, scr, slot, cb, dil, tm):
    cs = slice(cb * LANES, (cb + 1) * LANES)
    if dil == 1:
        return ref[0, :, cs].astype(F32)
    for r in range(dil):
        scr[slot, pl.ds(r, tm // dil, stride=dil), :] = ref[r, :, cs].astype(F32)
    return scr[slot]


def _out_ple_kernel(*refs, mode, tm):
    if mode == "mamba":
        (y_ref,) = refs[:1]
        pos = 1
        mix = y_ref[...].astype(BF16)
    elif mode == "attn_merge":
        o_refs = refs[0:3]
        l_refs = refs[3:6]
        gate_ref, x_ref = refs[6:8]
        pos = 8
        scr, mix_scr = refs[-2:]
        refs = refs[:-2]
        dils = [d for _, d in DILATION_GROUPS]
        ls = [_token_order(l_refs[g], scr, g, 0, dils[g], tm) for g in range(N_DGROUPS)]
        mx = jnp.maximum(jnp.maximum(ls[0], ls[1]), ls[2])
        es = [jnp.exp(l - mx) for l in ls]
        tot = es[0] + es[1] + es[2]
        sub = lax.broadcasted_iota(jnp.int32, (tm, LANES), 1) & (LSE_SLOT_LANES - 1)
        packed = jnp.zeros((tm, LANES), BF16)
        for g in range(N_DGROUPS):
            for idx, part in enumerate(_split_bf16(es[g] / tot, 2)):
                packed = jnp.where(sub == 2 * g + idx, part, packed)
        ws_all = _dot(packed, x_ref[...])
        ws = [ws_all[:, g * ATT_WIDTH:(g + 1) * ATT_WIDTH] for g in range(N_DGROUPS)]
        for cb in range(ATT_WIDTH // LANES):
            cs = slice(cb * LANES, (cb + 1) * LANES)
            o = None
            for g in range(N_DGROUPS):
                og = _token_order(o_refs[g], scr, N_DGROUPS + (cb * N_DGROUPS + g) % RELAYOUT_SLOTS,
                                  cb, dils[g], tm)
                term = og * ws[g][:, cs]
                o = term if o is None else o + term
            gate = gate_ref[:, cs]
            mix_scr[:, cs] = (o * _silu(gate)).astype(BF16)
        mix = mix_scr[...]
    else:
        o_ref_in, gate_ref = refs[:2]
        pos = 2
        gate = gate_ref[...]
        mix = (o_ref_in[...] * _silu(gate)).astype(BF16)
    h_ref, p_ref, wout_ref, pnw_ref, gw_ref, pw_ref, out_ref = refs[pos:]
    out_ref[...] = _residual_ple(mix, h_ref[...], p_ref[...], wout_ref, pnw_ref, gw_ref, pw_ref)


def _residual_ple(mix, h, p, wout_ref, pnw_ref, gw_ref, pw_ref):
    h1 = h + _dot(mix, wout_ref[...])
    ms = jnp.mean(h1 * h1, axis=-1, keepdims=True)
    un = (h1 * lax.rsqrt(ms + EPS) * pnw_ref[...]).astype(BF16)
    gate_p = _sigmoid(_dot(un, gw_ref[...]))
    pe = _dot(p.astype(BF16), pw_ref[...])
    return h1 + gate_p * pe


def _out_ple(mix_inputs, mode, h, p_all, layer, wout, pnw, gw, pw):
    m = h.shape[0]
    tm = min(ROW_TILE, m)
    row_map = lambda i: (i, 0)
    in_specs = []
    for a in mix_inputs:
        if a.ndim == 4:
            _, dil, rows, width = a.shape
            tiles = rows * dil // tm
            in_specs.append(pl.BlockSpec((None, dil, tm // dil, width),
                                         lambda i, tiles=tiles: (i // tiles, 0, i % tiles, 0)))
        elif a.shape[0] == m:
            in_specs.append(pl.BlockSpec((tm, a.shape[1]), row_map))
        else:
            in_specs.append(_const_spec(a.shape))
    scratch = []
    if mode == "attn_merge":
        scratch = [pltpu.VMEM((N_DGROUPS + RELAYOUT_SLOTS, tm, LANES), F32),
                   pltpu.VMEM((tm, ATT_WIDTH), BF16)]
    in_specs += [pl.BlockSpec((tm, D_MODEL), row_map),
                 pl.BlockSpec((None, tm, PLE_DIM), lambda i: (layer, i, 0)),
                 _const_spec(wout.shape),
                 _const_spec((1, D_MODEL)),
                 _const_spec((D_MODEL, D_MODEL)),
                 _const_spec((PLE_DIM, D_MODEL))]
    return pl.pallas_call(
        functools.partial(_out_ple_kernel, mode=mode, tm=tm),
        grid=(m // tm,),
        in_specs=in_specs,
        out_specs=pl.BlockSpec((tm, D_MODEL), row_map),
        out_shape=jax.ShapeDtypeStruct((m, D_MODEL), F32),
        scratch_shapes=scratch,
        compiler_params=_cparams(("parallel",)),
        name=f"out_ple_{mode}",
    )(*mix_inputs, h, p_all, wout, pnw, gw, pw)


def _rope_tables(pos):
    half = ATT_HEAD_DIM // 2
    inv = 1.0 / (ROPE_THETA ** (jnp.arange(half, dtype=F32) / half))
    ang = pos.astype(F32)[:, None] * inv[None, :]
    cos, sin = jnp.cos(ang), jnp.sin(ang)
    cos128 = jnp.concatenate([cos, cos, cos, cos], axis=1)
    sin128 = jnp.concatenate([-sin, sin, -sin, sin], axis=1)
    return cos128, sin128


def _permute_cols_kernel(w_ref, o_ref):
    lo = _lo_mask(w_ref.shape[0])
    for pos in range(ATT_HEADS // 2):
        ha, hb = HEAD_ORDER[2 * pos], HEAD_ORDER[2 * pos + 1]
        xa = w_ref[:, (ha // 2) * LANES:(ha // 2 + 1) * LANES]
        xb = w_ref[:, (hb // 2) * LANES:(hb // 2 + 1) * LANES]
        if ha % 2 == 1:
            xa = pltpu.roll(xa, ATT_HEAD_DIM, 1)
        if hb % 2 == 0:
            xb = pltpu.roll(xb, ATT_HEAD_DIM, 1)
        o_ref[:, pos * LANES:(pos + 1) * LANES] = jnp.where(lo, xa, xb).astype(o_ref.dtype)


def _permute_rows_kernel(w_ref, o_ref):
    for pos, head in enumerate(HEAD_ORDER):
        o_ref[pos * ATT_HEAD_DIM:(pos + 1) * ATT_HEAD_DIM, :] = (
            w_ref[head * ATT_HEAD_DIM:(head + 1) * ATT_HEAD_DIM, :].astype(o_ref.dtype))


def _permute_attn_weights(a_in_w, a_out_w):
    nl = a_in_w.shape[0]
    ngrp = a_in_w.shape[2] // ATT_WIDTH
    blk = (None, D_MODEL, ATT_WIDTH)
    in_w = pl.pallas_call(
        _permute_cols_kernel,
        grid=(nl, ngrp),
        in_specs=[pl.BlockSpec(blk, lambda j, g: (j, 0, g))],
        out_specs=pl.BlockSpec(blk, lambda j, g: (j, 0, g)),
        out_shape=jax.ShapeDtypeStruct(a_in_w.shape, BF16),
        compiler_params=_cparams(("parallel", "parallel")),
        name="permute_cols",
    )(a_in_w)
    out_w = pl.pallas_call(
        _permute_rows_kernel,
        grid=(nl,),
        in_specs=[pl.BlockSpec(blk, lambda j: (j, 0, 0))],
        out_specs=pl.BlockSpec(blk, lambda j: (j, 0, 0)),
        out_shape=jax.ShapeDtypeStruct(a_out_w.shape, BF16),
        compiler_params=_cparams(("parallel",)),
        name="permute_rows",
    )(a_out_w)
    return in_w, out_w


def _prep_weights(norm_w, m_in_w, m_conv_w, m_conv_b, m_dt_bias, m_A_log, m_D, m_norm_w, m_out_w,
                  kv_norm_w, kv_w, k_norm_w, a_in_w, a_q_norm_w, a_out_w, ple_w, ple_gate_w,
                  ple_norm_w):
    row = lambda v: v.reshape(1, -1).astype(F32)
    reps = LANES // SSM_HEADS
    lane_pad = lambda v: jnp.tile(v.astype(F32), reps).reshape(1, LANES)
    mamba = []
    for i in range(N_A_LAYERS):
        mamba.append(dict(
            in_w=m_in_w[i].astype(BF16),
            dt_w=jnp.tile(m_in_w[i][:, D_INNER + CONV_DIM:], (1, reps)).astype(BF16),
            conv_w=m_conv_w[i].astype(F32),
            conv_b=row(m_conv_b[i]),
            dt_bias=lane_pad(m_dt_bias[i]),
            a_log=lane_pad(m_A_log[i]),
            d_exp=row(jnp.repeat(m_D[i], SSM_HEADDIM)),
            norm_w=row(m_norm_w[i]),
            out_w=m_out_w[i].astype(BF16),
        ))
    attn = []
    a_in_bf, a_out_bf = _permute_attn_weights(a_in_w, a_out_w)
    for j in range(DEPTH - N_A_LAYERS):
        attn.append(dict(
            q_norm_w=row(jnp.tile(a_q_norm_w[j], LANES // ATT_HEAD_DIM)),
            out_w=a_out_bf[j],
        ))
    return dict(
        norm_w=[row(norm_w[i]) for i in range(DEPTH)],
        mamba=mamba,
        attn=attn,
        attn_in_w=a_in_bf,
        kv_norm_w=row(kv_norm_w),
        kv_w=kv_w.astype(BF16),
        k_norm_w=row(jnp.tile(k_norm_w, LANES // ATT_HEAD_DIM)),
        ple_w=[ple_w[i].astype(BF16) for i in range(DEPTH)],
        ple_gate_w=[ple_gate_w[i].astype(BF16) for i in range(DEPTH)],
        ple_norm_w=[row(ple_norm_w[i]) for i in range(DEPTH)],
    )


def _constants():
    t = SSD_CHUNK
    tril = jnp.tril(jnp.ones((t, t), F32)).astype(BF16)
    head = jnp.arange(LANES, dtype=jnp.int32)[:, None]
    col = jnp.arange(D_INNER, dtype=jnp.int32)[None, :]
    expand = (head % SSM_HEADS == col // SSM_HEADDIM).astype(BF16)
    li = jnp.arange(LANES, dtype=jnp.int32)
    l2 = jnp.arange(2 * LANES, dtype=jnp.int32)
    bd = ((l2[:, None] // ATT_HEAD_DIM) == (l2[None, :] // ATT_HEAD_DIM)).astype(F32) / ATT_HEAD_DIM
    ccol = jnp.arange(N_DGROUPS * ATT_WIDTH, dtype=jnp.int32)[None, :]
    cgrp, cc = ccol // ATT_WIDTH, ccol % ATT_WIDTH
    slot = cc // LANES + jnp.where(cc % LANES < ATT_HEAD_DIM, 0, ATT_WIDTH // LANES)
    lrow = li[:, None]
    lse_spread = ((lrow >> LSE_SLOT_LOG2 == slot)
                  & ((lrow & (LSE_SLOT_LANES - 1)) >> 1 == cgrp)).astype(BF16)
    return dict(tril=tril, expand=expand, bd=bd.astype(BF16), lse_spread=lse_spread)


_Q_SCALE = ATT_HEAD_DIM ** -0.5
_DILS = tuple(d for _, d in DILATION_GROUPS)
M_IN_COLS = D_INNER + CONV_DIM + SSM_HEADS
MAMBA_SEGS = ((0, D_INNER, False, 1.0, ((0, 0, 0),)),
              (D_INNER, CONV_DIM, False, 1.0, ((1, 0, 0),)),
              (M_IN_COLS, LANES, False, 1.0, ((2, 0, 0),)))
MAMBA_OUTS = ((D_INNER, F32, 0), (CONV_DIM, F32, 0), (LANES, F32, 0))
KV_SEGS_TOK = ((0, KV_HALF, True, 1.0, ((0, 0, 0),)), (KV_HALF, KV_HALF, False, 1.0, ((1, 0, 0),)))
KV_OUTS_TOK = ((KV_HALF, F32, 0), (KV_HALF, F32, 0))
ATTN_SEGS_TOK = ((0, N_DGROUPS * ATT_WIDTH, True, _Q_SCALE, ((0, 0, 0),)),
                 (N_DGROUPS * ATT_WIDTH, ATT_WIDTH, False, 1.0, ((1, 0, 0),)))
ATTN_OUTS_TOK = ((N_DGROUPS * ATT_WIDTH, F32, 0), (ATT_WIDTH, F32, 0))
KV_SEGS_RES = tuple(
    (g * GROUP_KV, GROUP_KV, True, 1.0, ((0, g * GROUP_KV, 0), (2 + g, 0, _DILS[g])))
    for g in range(N_DGROUPS)) + tuple(
    (KV_HALF + g * GROUP_KV, GROUP_KV, False, 1.0, ((1, g * GROUP_KV, 0), (2 + N_DGROUPS + g, 0, _DILS[g])))
    for g in range(N_DGROUPS))
KV_OUTS_RES = KV_OUTS_TOK + tuple((GROUP_KV, BF16, d) for d in _DILS) * 2
ATTN_SEGS_RES = tuple((g * ATT_WIDTH, ATT_WIDTH, True, _Q_SCALE * LOG2E, ((g, 0, _DILS[g]),))
                      for g in range(N_DGROUPS)) + (
    (N_DGROUPS * ATT_WIDTH, ATT_WIDTH, False, 1.0, ((N_DGROUPS, 0, 0),)),)
ATTN_OUTS_RES = tuple((ATT_WIDTH, BF16, d) for d in _DILS) + ((ATT_WIDTH, F32, 0),)


def _trunk(x, p_all, wts, consts, rope_tabs, rope_rows, *, nbatch, seq, ssm0, conv0, caches):
    prompt = caches is None
    m = x.shape[0]
    h = x
    new_ssm, new_conv = [], []
    for i in range(N_A_LAYERS):
        lw = wts["mamba"][i]
        ple = (wts["ple_norm_w"][i], wts["ple_gate_w"][i], wts["ple_w"][i])
        if prompt:
            h, h_fin, xbc_tail = _mamba_layer(h, p_all, i, wts["norm_w"][i], lw, consts, *ple,
                                              nbatch=nbatch, seq=seq)
            new_conv.append(xbc_tail[:, SUBLANES - (CONV_K - 1):])
        else:
            z, xbc, dtp = _norm_proj(h, wts["norm_w"][i], lw["in_w"], MAMBA_SEGS, MAMBA_OUTS,
                                     w_extra=lw["dt_w"])
            y, h_fin = _ssd(z, xbc, dtp, lw, consts, ssm0, conv0, i, nbatch=nbatch,
                            rows_in=seq, nchunks=1, y_dtype=F32,
                            prev_states=None if i == 0 else stacked_ssm)
            stacked_ssm = h_fin[None] if i == 0 else h_fin
            new_conv.append(xbc.reshape(nbatch, seq, CONV_DIM)[:, seq - (CONV_K - 1):])
            h = _out_ple([y], "mamba", h, p_all, i, lw["out_w"], *ple)
            continue
        new_ssm.append(h_fin.reshape(nbatch, SSM_HEADS, SSM_HEADDIM, D_STATE))
    if prompt:
        stacked_ssm = jnp.stack(new_ssm, axis=0)
    else:
        stacked_ssm = stacked_ssm.reshape(N_A_LAYERS, nbatch, SSM_HEADS, SSM_HEADDIM, D_STATE)
    cos, sin = rope_tabs
    kv_outs = _norm_proj(h, wts["kv_norm_w"], wts["kv_w"],
                         KV_SEGS_RES if prompt else KV_SEGS_TOK,
                         KV_OUTS_RES if prompt else KV_OUTS_TOK,
                         rope_inputs=(wts["k_norm_w"], cos, sin, consts["bd"]), rope_rows=rope_rows,
                         nbatch=nbatch)
    k, v = kv_outs[:2]
    for j in range(DEPTH - N_A_LAYERS):
        i = N_A_LAYERS + j
        aw = wts["attn"][j]
        q_outs = _norm_proj(h, wts["norm_w"][i], wts["attn_in_w"],
                            ATTN_SEGS_RES if prompt else ATTN_SEGS_TOK,
                            ATTN_OUTS_RES if prompt else ATTN_OUTS_TOK,
                            rope_inputs=(aw["q_norm_w"], cos, sin, consts["bd"]), rope_rows=rope_rows,
                            nbatch=nbatch, w_layer=j)
        gate = q_outs[-1]
        if prompt:
            os_, ls_ = [], []
            for gi in range(N_DGROUPS):
                o_g, l_g = _attn_prompt_group(q_outs[gi], kv_outs[2 + gi], kv_outs[2 + N_DGROUPS + gi], gi)
                os_.append(o_g)
                ls_.append(l_g)
            mix_inputs, mode = os_ + ls_ + [gate, consts["lse_spread"]], "attn_merge"
        else:
            o = _attn_sample(q_outs[0], k, v, caches, nbatch, seq)
            mix_inputs, mode = [o, gate], "attn"
        h = _out_ple(mix_inputs, mode, h, p_all, i, aw["out_w"], wts["ple_norm_w"][i],
                     wts["ple_gate_w"][i], wts["ple_w"][i])
    return h, stacked_ssm, jnp.stack(new_conv, axis=0), k, v


def kernel(x_prompt, x_sample, state_ssm, state_conv, cache_kv_g1, cache_kv_g2, cache_kv_g3,
           p_prompt, p_sample, norm_w, m_in_w, m_conv_w, m_conv_b, m_dt_bias, m_A_log, m_D,
           m_norm_w, m_out_w, kv_norm_w, kv_w, k_norm_w, a_in_w, a_q_norm_w, a_out_w,
           ple_w, ple_gate_w, ple_norm_w):
    wts = _prep_weights(norm_w, m_in_w, m_conv_w, m_conv_b, m_dt_bias, m_A_log, m_D, m_norm_w,
                        m_out_w, kv_norm_w, kv_w, k_norm_w, a_in_w, a_q_norm_w, a_out_w, ple_w,
                        ple_gate_w, ple_norm_w)
    consts = _constants()

    b_p, seq = x_prompt.shape[0], x_prompt.shape[1]
    m_p = b_p * seq
    tabs_p = _rope_tables(jnp.arange(seq, dtype=jnp.int32))
    y_p, ssm_p, conv_p, k_p, v_p = _trunk(
        x_prompt.reshape(m_p, D_MODEL), p_prompt.reshape(DEPTH, m_p, PLE_DIM), wts, consts,
        tabs_p, seq, nbatch=b_p, seq=seq, ssm0=None, conv0=None, caches=None)

    b_s, dec = x_sample.shape[0], x_sample.shape[1]
    m_s = b_s * dec
    pos_s = PAST_LEN + jnp.arange(dec, dtype=jnp.int32)
    tabs_s = tuple(jnp.tile(t, (b_s, 1)) for t in _rope_tables(pos_s))
    ssm0_s = state_ssm.reshape(N_A_LAYERS, b_s, D_INNER, D_STATE)
    conv0_s = jnp.pad(state_conv, ((0, 0), (0, 0), (SUBLANES - (CONV_K - 1), 0), (0, 0)))
    caches = tuple(jnp.transpose(cch, (0, 2, 3, 4, 1)).reshape(b_s, 2 * GROUP_KV, cch.shape[1])
                   for cch in (cache_kv_g1, cache_kv_g2, cache_kv_g3))
    y_s, ssm_s, conv_s, k_s, v_s = _trunk(
        x_sample.reshape(m_s, D_MODEL), p_sample.reshape(DEPTH, m_s, PLE_DIM), wts, consts,
        tabs_s, m_s, nbatch=b_s, seq=dec, ssm0=ssm0_s, conv0=conv0_s, caches=caches)

    def kv_out(k, v, nbatch, length, gi, keep):
        def tail(a):
            a = a.reshape(nbatch, length, KV_HALF)[:, length - keep:, gi * GROUP_KV:(gi + 1) * GROUP_KV]
            return a.reshape(nbatch, keep, ATT_KV_HEADS, ATT_HEAD_DIM)
        return jnp.stack([tail(k), tail(v)], axis=2)

    kv_p = [kv_out(k_p, v_p, b_p, seq, gi, min(w, seq)) for gi, (w, _) in enumerate(DILATION_GROUPS)]
    kv_s = [kv_out(k_s, v_s, b_s, dec, gi, dec) for gi in range(N_DGROUPS)]
    return (y_p.reshape(b_p, seq, D_MODEL), y_s.reshape(b_s, dec, D_MODEL),
            ssm_p, conv_p, ssm_s, conv_s, kv_p[0], kv_p[1], kv_p[2], kv_s[0], kv_s[1], kv_s[2])
```
